```python
import math
import jax, jax.numpy as jnp
from jax import lax
import numpy as np

D_MODEL = 1024
BATCH = 32
SEQ = 256
DEPTH = 2
DEC_BATCH = 2
DEC_SEQ = 2048
PAST_LEN = 512

GRID_W = 64
N_ATT_LAYERS = (DEPTH + 1) // 2
N_SSD_LAYERS = DEPTH // 2
EPS = 1e-6
MIX_W = D_MODEL
DA_HEADS = 4
DA_HD = 64
DA_QK_W = DA_HEADS * 2 * DA_HD
DA_V_W = DA_HEADS * 2 * DA_HD
POOL_W = MIX_W - DA_V_W
POOL_WINDOWS = (2, 4, 8, 16)
POOL_GROUPS = 4
POOL_GC = POOL_W // POOL_GROUPS
ATT_IN_W = 2 * DA_QK_W + DA_V_W + POOL_W
Q_BLOCK = 128
ROPE_THETA = 10000.0
ROPE_NF = DA_HD // 4
SSD_DI = 2 * D_MODEL
SSD_HEADDIM = 64
SSD_HEADS = SSD_DI // SSD_HEADDIM
SSD_GROUPS = 4
SSD_STATE = 128
SSD_CONV = 3
SSD_CHUNK = 128
SSD_XBC_W = SSD_DI + 2 * SSD_GROUPS * SSD_STATE
SSD_IN_W = SSD_DI + SSD_XBC_W + 2 * SSD_HEADS
D_FF = 2816
FFN_CONV = 3

kernel_name = "hybrid_diffusion_diffattn_pool_ssd_step"


def rmsnorm(x, g):
    xf = x.astype(jnp.float32)
    y = xf * lax.rsqrt(jnp.mean(xf * xf, axis=-1, keepdims=True) + EPS)
    return (y * g.astype(jnp.float32)).astype(x.dtype)


def adaln(cvec, w, b):
    m = jax.nn.silu(cvec) @ w + b
    return jnp.split(m, 6, axis=-1)


def modulate(x, g, shift, scale):
    return rmsnorm(x, g) * (1 + scale[..., None, :]) + shift[..., None, :]


def dwconv_centered(x, w, b):
    k = w.shape[0]
    y = lax.conv_general_dilated(x, w[:, None, :].astype(x.dtype), window_strides=(1,),
                                 padding=[(k // 2, k - 1 - k // 2)],
                                 dimension_numbers=('NWC', 'WIO', 'NWC'),
                                 feature_group_count=x.shape[-1])
    return y + b


def axial_rope_tables(n_lat):
    rows = n_lat // GRID_W
    r, col = jnp.meshgrid(jnp.arange(rows, dtype=jnp.float32), jnp.arange(GRID_W, dtype=jnp.float32), indexing='ij')
    inv = ROPE_THETA ** (-jnp.arange(ROPE_NF, dtype=jnp.float32) / ROPE_NF)
    ang = jnp.stack([r.reshape(-1)[:, None] * inv, col.reshape(-1)[:, None] * inv], axis=1)
    return jnp.cos(ang), jnp.sin(ang)


def apply_axial_rope(x, cos, sin):
    shp = x.shape
    xr = x.astype(jnp.float32).reshape(shp[:-1] + (2, 2, ROPE_NF))
    x1, x2 = xr[..., 0, :], xr[..., 1, :]
    c = cos[None, :, None, None]
    s = sin[None, :, None, None]
    out = jnp.stack([x1 * c - x2 * s, x1 * s + x2 * c], axis=-2)
    return out.reshape(shp).astype(x.dtype)


def diff_attention(q, k, v, lam):
    b, lq = q.shape[:2]
    nb = lq // Q_BLOCK
    qb = jnp.moveaxis(q.reshape(b, nb, Q_BLOCK, DA_HEADS, 2, DA_HD), 1, 0)
    scale = DA_HD ** -0.5

    def block(qi):
        s = jnp.einsum('bqhcd,bkhcd->bhcqk', qi, k).astype(jnp.float32) * scale
        p = jax.nn.softmax(s, axis=-1)
        a = p[:, :, 0] - lam * p[:, :, 1]
        return jnp.einsum('bhqk,bkhe->bqhe', a.astype(v.dtype), v)

    o = lax.map(block, qb)
    return jnp.moveaxis(o, 0, 1).reshape(b, lq, DA_HEADS, 2 * DA_HD)


def multiscale_pool(x, w, scale):
    b, n, _ = x.shape
    xg = x.reshape(b, n, POOL_GROUPS, POOL_GC)
    cs = jnp.pad(jnp.cumsum(xg.astype(jnp.float32), axis=1), ((0, 0), (1, 0), (0, 0), (0, 0)))
    t = jnp.arange(n)
    pooled = []
    for gi, win in enumerate(POOL_WINDOWS):
        lo = jnp.clip(t - win // 2, 0, n)
        hi = jnp.clip(t + win // 2, 0, n)
        cg = cs[:, :, gi]
        mean = (cg[:, hi] - cg[:, lo]) / (hi - lo).astype(jnp.float32)[:, None]
        pooled.append(mean - xg[:, :, gi].astype(jnp.float32))
    pg = jnp.stack(pooled, axis=2).astype(x.dtype)
    y = jnp.einsum('bngc,gcd->bngd', pg, w)
    return y.reshape(b, n, POOL_W) * scale


def even_mixer(h, w_in, lam_vecs, subln_g, pool_w, pool_scale, w_out, lam_init, rope=None, ctx_k=None, ctx_v=None):
    b, n = h.shape[:2]
    proj = h @ w_in
    q, k, v, xpool = jnp.split(proj, [DA_QK_W, 2 * DA_QK_W, 2 * DA_QK_W + DA_V_W], axis=-1)
    q = q.reshape(b, n, DA_HEADS, 2, DA_HD)
    k = k.reshape(b, n, DA_HEADS, 2, DA_HD)
    v = v.reshape(b, n, DA_HEADS, 2 * DA_HD)
    if rope is None:
        q_att, k_all, v_all = q, k, v
    else:
        q_att = apply_axial_rope(q, *rope)
        k_all = jnp.concatenate([apply_axial_rope(k, *rope), ctx_k.astype(k.dtype)], axis=1)
        v_all = jnp.concatenate([v, ctx_v.astype(v.dtype)], axis=1)
    lv = lam_vecs.astype(jnp.float32)
    lam = jnp.exp(jnp.sum(lv[0] * lv[1])) - jnp.exp(jnp.sum(lv[2] * lv[3])) + lam_init
    o = diff_attention(q_att, k_all, v_all, lam)
    o = rmsnorm(o, subln_g) * (1 - lam_init)
    pool_out = multiscale_pool(xpool, pool_w, pool_scale)
    out = jnp.concatenate([o.reshape(b, n, DA_V_W), pool_out], axis=-1) @ w_out
    return out, k, v


def ssd_chunked(x, dt, a, bm, cm, init_state):
    b, n = x.shape[:2]
    nc = n // SSD_CHUNK
    r = SSD_HEADS // SSD_GROUPS
    xc = x.astype(jnp.float32).reshape(b, nc, SSD_CHUNK, SSD_GROUPS, r, SSD_HEADDIM)
    dtc = dt.reshape(b, nc, SSD_CHUNK, SSD_GROUPS, r)
    bc = bm.astype(jnp.float32).reshape(b, nc, SSD_CHUNK, SSD_GROUPS, SSD_STATE)
    cc = cm.astype(jnp.float32).reshape(b, nc, SSD_CHUNK, SSD_GROUPS, SSD_STATE)
    acs = jnp.cumsum(dtc * a.reshape(SSD_GROUPS, r), axis=2)
    seg = acs[:, :, :, None] - acs[:, :, None, :]
    lower = jnp.tril(jnp.ones((SSD_CHUNK, SSD_CHUNK), dtype=bool))[:, :, None, None]
    lmat = jnp.exp(jnp.where(lower, seg, -jnp.inf))
    cb = jnp.einsum('bcign,bcjgn->bcijg', cc, bc)
    wmat = cb[..., None] * lmat * dtc[:, :, None]
    y_diag = jnp.einsum('bcijgr,bcjgrp->bcigrp', wmat, xc)
    xw = (jnp.exp(acs[:, :, -1:] - acs) * dtc)[..., None] * xc
    chunk_states = jnp.einsum('bcjgn,bcjgrp->bcgrpn', bc, xw)
    chunk_decay = jnp.exp(acs[:, :, -1])

    def step(state, inp):
        cs_, dec = inp
        return state * dec[..., None, None] + cs_, state

    s0 = init_state.astype(jnp.float32).reshape(b, SSD_GROUPS, r, SSD_HEADDIM, SSD_STATE)
    final, starts = lax.scan(step, s0, (jnp.moveaxis(chunk_states, 1, 0), jnp.moveaxis(chunk_decay, 1, 0)))
    starts = jnp.moveaxis(starts, 0, 1)
    y_off = jnp.einsum('bcign,bcgrpn->bcigrp', cc, starts) * jnp.exp(acs)[..., None]
    y = (y_diag + y_off).reshape(b, n, SSD_HEADS, SSD_HEADDIM)
    return y, final.reshape(b, SSD_HEADS, SSD_HEADDIM, SSD_STATE)


def ssd_mixer(h, w_in, conv_w, conv_b, dt_bias, a_log, d_skip, norm_g, w_out, init_f=None, init_b=None):
    b, n = h.shape[:2]
    proj = h @ w_in
    z, xbc, dt = jnp.split(proj, [SSD_DI, SSD_DI + SSD_XBC_W], axis=-1)
    xbc = jax.nn.silu(dwconv_centered(xbc, conv_w, conv_b))
    xs, bm, cm = jnp.split(xbc, [SSD_DI, SSD_DI + SSD_GROUPS * SSD_STATE], axis=-1)
    xs = xs.reshape(b, n, SSD_HEADS, SSD_HEADDIM)
    bm = bm.reshape(b, n, SSD_GROUPS, SSD_STATE)
    cm = cm.reshape(b, n, SSD_GROUPS, SSD_STATE)
    dt = jax.nn.softplus(dt.astype(jnp.float32).reshape(b, n, 2, SSD_HEADS) + dt_bias.astype(jnp.float32))
    a = -jnp.exp(a_log.astype(jnp.float32))
    if init_f is None:
        init_f = jnp.zeros((b, SSD_HEADS, SSD_HEADDIM, SSD_STATE), jnp.float32)
        init_b = jnp.zeros((b, SSD_HEADS, SSD_HEADDIM, SSD_STATE), jnp.float32)
    y_f, s_f = ssd_chunked(xs, dt[:, :, 0], a[0], bm, cm, init_f)
    y_b, s_b = ssd_chunked(jnp.flip(xs, 1), jnp.flip(dt[:, :, 1], 1), a[1], jnp.flip(bm, 1), jnp.flip(cm, 1), init_b)
    y = y_f + jnp.flip(y_b, 1) + xs.astype(jnp.float32) * d_skip.astype(jnp.float32)[:, None]
    y = y.reshape(b, n, SSD_DI) * jax.nn.silu(z.astype(jnp.float32))
    y = rmsnorm(y, norm_g).astype(h.dtype)
    return y @ w_out, s_f, s_b


def conv_ffn(h, w_up, conv_w, conv_b, w_down):
    u = dwconv_centered(h @ w_up, conv_w, conv_b)
    g, v = jnp.split(u, 2, axis=-1)
    return (jax.nn.silu(g) * v) @ w_down


def setup_inputs(seed: int = 0) -> dict:
    key = jax.random.key(seed)
    ks = iter(jax.random.split(key, 48))

    def nrm(shape, scale):
        return jax.random.normal(next(ks), shape, jnp.float32) * scale

    def gain(shape):
        return 1.0 + nrm(shape, 0.05)

    def dt_bias(shape):
        dt = jnp.exp(jax.random.uniform(next(ks), shape, jnp.float32, math.log(1e-3), math.log(1e-1)))
        return dt + jnp.log(-jnp.expm1(-dt))

    d = D_MODEL
    return {
        "x_prompt": nrm((BATCH, SEQ, d), 1.0),
        "x_sample": nrm((DEC_BATCH, DEC_SEQ, d), 1.0),
        "cache_k": nrm((DEC_BATCH, N_ATT_LAYERS, PAST_LEN, DA_HEADS, 2, DA_HD), 1.0),
        "cache_v": nrm((DEC_BATCH, N_ATT_LAYERS, PAST_LEN, DA_HEADS, 2 * DA_HD), 1.0),
        "state_ssm_fwd": nrm((DEC_BATCH, N_SSD_LAYERS, SSD_HEADS, SSD_HEADDIM, SSD_STATE), 0.1),
        "state_ssm_bwd": nrm((DEC_BATCH, N_SSD_LAYERS, SSD_HEADS, SSD_HEADDIM, SSD_STATE), 0.1),
        "c": nrm((DEC_BATCH, d), 1.0),
        "c_ctx": nrm((d,), 1.0),
        "ada_w": nrm((DEPTH, d, 6 * d), 0.5 * d ** -0.5),
        "ada_b": nrm((DEPTH, 6 * d), 0.02),
        "norm_mix_g": gain((DEPTH, d)),
        "norm_ffn_g": gain((DEPTH, d)),
        "att_w_in": nrm((N_ATT_LAYERS, d, ATT_IN_W), d ** -0.5),
        "att_lambda": nrm((N_ATT_LAYERS, 4, DA_HD), 0.1),
        "att_subln_g": gain((N_ATT_LAYERS, 2 * DA_HD)),
        "pool_w": nrm((N_ATT_LAYERS, POOL_GROUPS, POOL_GC, POOL_GC), POOL_GC ** -0.5),
        "pool_scale": gain((N_ATT_LAYERS, POOL_W)),
        "att_w_out": nrm((N_ATT_LAYERS, MIX_W, d), MIX_W ** -0.5),
        "ssd_w_in": nrm((N_SSD_LAYERS, d, SSD_IN_W), d ** -0.5),
        "ssd_conv_w": nrm((N_SSD_LAYERS, SSD_CONV, SSD_XBC_W), SSD_CONV ** -0.5),
        "ssd_conv_b": nrm((N_SSD_LAYERS, SSD_XBC_W), 0.02),
        "ssd_dt_bias": dt_bias((N_SSD_LAYERS, 2, SSD_HEADS)),
        "ssd_a_log": jnp.log(jax.random.uniform(next(ks), (N_SSD_LAYERS, 2, SSD_HEADS), jnp.float32, 1.0, 16.0)),
        "ssd_d": gain((N_SSD_LAYERS, SSD_HEADS)),
        "ssd_norm_g": gain((N_SSD_LAYERS, SSD_DI)),
        "ssd_w_out": nrm((N_SSD_LAYERS, SSD_DI, d), SSD_DI ** -0.5),
        "ffn_w_up": nrm((DEPTH, d, 2 * D_FF), d ** -0.5),
        "ffn_conv_w": nrm((DEPTH, FFN_CONV, 2 * D_FF), FFN_CONV ** -0.5),
        "ffn_conv_b": nrm((DEPTH, 2 * D_FF), 0.02),
        "ffn_w_down": nrm((DEPTH, D_FF, d), D_FF ** -0.5),
        "final_norm_g": gain((d,)),
    }


def reference(x_prompt, x_sample, cache_k, cache_v, state_ssm_fwd, state_ssm_bwd, c, c_ctx,
              ada_w, ada_b, norm_mix_g, norm_ffn_g,
              att_w_in, att_lambda, att_subln_g, pool_w, pool_scale, att_w_out,
              ssd_w_in, ssd_conv_w, ssd_conv_b, ssd_dt_bias, ssd_a_log, ssd_d, ssd_norm_g, ssd_w_out,
              ffn_w_up, ffn_conv_w, ffn_conv_b, ffn_w_down, final_norm_g):
    rope = axial_rope_tables(x_sample.shape[1])
    xp, xs = x_prompt, x_sample
    k_new, v_new, sf_new, sb_new = [], [], [], []
    for l in range(DEPTH):
        i = l // 2
        mod_p = adaln(c_ctx, ada_w[l], ada_b[l])
        mod_s = adaln(c, ada_w[l], ada_b[l])
        hp = modulate(xp, norm_mix_g[l], mod_p[0], mod_p[1])
        hs = modulate(xs, norm_mix_g[l], mod_s[0], mod_s[1])
        if l % 2 == 0:
            lam_init = 0.8 - 0.6 * math.exp(-0.3 * l)
            att = (att_w_in[i], att_lambda[i], att_subln_g[i], pool_w[i], pool_scale[i], att_w_out[i])
            mp, kp, vp = even_mixer(hp, *att, lam_init)
            ms = even_mixer(hs, *att, lam_init, rope=rope, ctx_k=cache_k[:, i], ctx_v=cache_v[:, i])[0]
            k_new.append(kp)
            v_new.append(vp)
        else:
            ssd = (ssd_w_in[i], ssd_conv_w[i], ssd_conv_b[i], ssd_dt_bias[i], ssd_a_log[i], ssd_d[i], ssd_norm_g[i], ssd_w_out[i])
            mp, sf, sb = ssd_mixer(hp, *ssd)
            ms = ssd_mixer(hs, *ssd, init_f=state_ssm_fwd[:, i], init_b=state_ssm_bwd[:, i])[0]
            sf_new.append(sf.astype(x_prompt.dtype))
            sb_new.append(sb.astype(x_prompt.dtype))
        xp = xp + mod_p[2][..., None, :] * mp
        xs = xs + mod_s[2][..., None, :] * ms
        ffn = (ffn_w_up[l], ffn_conv_w[l], ffn_conv_b[l], ffn_w_down[l])
        xp = xp + mod_p[5][..., None, :] * conv_ffn(modulate(xp, norm_ffn_g[l], mod_p[3], mod_p[4]), *ffn)
        xs = xs + mod_s[5][..., None, :] * conv_ffn(modulate(xs, norm_ffn_g[l], mod_s[3], mod_s[4]), *ffn)
    y_prompt = rmsnorm(xp, final_norm_g)
    y_sample = rmsnorm(xs, final_norm_g)
    new_cache_k = jnp.stack(k_new, axis=1)
    new_cache_v = jnp.stack(v_new, axis=1)
    new_state_fwd = jnp.stack(sf_new, axis=1)
    new_state_bwd = jnp.stack(sb_new, axis=1)
    return (y_prompt, y_sample, new_cache_k, new_cache_v, new_state_fwd, new_state_bwd)
```

```python
import functools
import math

import jax
import jax.numpy as jnp
from jax import lax
from jax.experimental import pallas as pl
from jax.experimental.pallas import tpu as pltpu

F32 = jnp.float32
BF16 = jnp.bfloat16
HIGHEST = lax.Precision.HIGHEST

D_MODEL = 1024
BATCH = 32
SEQ = 256
DEPTH = 2
DEC_BATCH = 2
DEC_SEQ = 2048
PAST_LEN = 512
GRID_W = 64
EPS = 1e-6
DA_HEADS = 4
DA_HD = 64
DA_QK_W = DA_HEADS * 2 * DA_HD
DA_V_W = DA_HEADS * 2 * DA_HD
POOL_W = D_MODEL - DA_V_W
POOL_WINDOWS = (2, 4, 8, 16)
POOL_GROUPS = 4
POOL_GC = POOL_W // POOL_GROUPS
ATT_IN_W = 2 * DA_QK_W + DA_V_W + POOL_W
ROPE_THETA = 10000.0
ROPE_NF = DA_HD // 4
SSD_DI = 2 * D_MODEL
SSD_HEADDIM = 64
SSD_HEADS = SSD_DI // SSD_HEADDIM
SSD_GROUPS = 4
SSD_STATE = 128
SSD_CHUNK = 128
SSD_BC_W = SSD_GROUPS * SSD_STATE
SSD_XBC_W = SSD_DI + 2 * SSD_BC_W
D_FF = 2816

ROWS_P = BATCH * SEQ
ROWS_S = DEC_BATCH * DEC_SEQ
ROWS = ROWS_P + ROWS_S
N_MOD = 1 + DEC_BATCH
MOD_PAD = 8

LANES = 128
BF16_ROWS = 16
VMEM_LIMIT = 56 * 1024 * 1024

TM = 512
TM_SSD_IN = 256
TF = 256
TQ = 256


def _cparams(*sem):
    return pltpu.CompilerParams(dimension_semantics=sem, vmem_limit_bytes=VMEM_LIMIT)


def _mod_index(i, tm):
    n_p, per_seq = ROWS_P // tm, DEC_SEQ // tm
    return jnp.where(i < n_p, 0, 1 + (i - n_p) // per_seq)


def _silu(x):
    return x / (1.0 + jnp.exp(-x))


def _modulate(x, g, shift, scale):
    ms = jnp.mean(x * x, axis=-1, keepdims=True)
    y = x * lax.rsqrt(ms + EPS) * g
    return y * (1.0 + scale) + shift


def _const_spec(shape):
    nd = len(shape)
    return pl.BlockSpec(shape, lambda *_: (0,) * nd)


def _halo_specs(tm, width):
    per = tm // BF16_ROWS
    last = ROWS // BF16_ROWS - 1
    prev = pl.BlockSpec((BF16_ROWS, width), lambda i, *_: (jnp.maximum(i * per - 1, 0), 0))
    nxt = pl.BlockSpec((BF16_ROWS, width), lambda i, *_: (jnp.minimum((i + 1) * per, last), 0))
    return prev, nxt


def _seq_edge_masks(i, tm, shape):
    r = i * tm + lax.broadcasted_iota(jnp.int32, shape, 0)
    m = jnp.where(i < ROWS_P // tm, SEQ - 1, DEC_SEQ - 1)
    pos = r & m
    return pos == 0, pos == m


def _conv3(u, cw, cb, first, last, tm):
    n = u.shape[0]
    up = pltpu.roll(u, 1, axis=0)[BF16_ROWS:BF16_ROWS + tm]
    un = pltpu.roll(u, n - 1, axis=0)[BF16_ROWS:BF16_ROWS + tm]
    uc = u[BF16_ROWS:BF16_ROWS + tm]
    return (cw[1:2] * uc + cb + cw[0:1] * jnp.where(first, 0.0, up)
            + cw[2:3] * jnp.where(last, 0.0, un))


def _adaln_kernel(c_ref, w_ref, b_ref, o_ref):
    s = _silu(c_ref[...])
    o_ref[0] = jnp.dot(s, w_ref[0], precision=HIGHEST, preferred_element_type=F32) + b_ref[0]


def _adaln(cvec, ada_w, ada_b):
    tn = 1536
    n = 6 * D_MODEL
    return pl.pallas_call(
        _adaln_kernel,
        grid=(DEPTH, n // tn),
        in_specs=[_const_spec((MOD_PAD, D_MODEL)),
                  pl.BlockSpec((1, D_MODEL, tn), lambda l, j: (l, 0, j)),
                  pl.BlockSpec((1, 1, tn), lambda l, j: (l, 0, j))],
        out_specs=pl.BlockSpec((1, MOD_PAD, tn), lambda l, j: (l, 0, j)),
        out_shape=jax.ShapeDtypeStruct((DEPTH, MOD_PAD, n), F32),
        compiler_params=_cparams("parallel", "parallel"),
        name="adaln",
    )(cvec, ada_w, ada_b.reshape(DEPTH, 1, n))


def _rope(x, cos, sin_signed):
    lane = lax.broadcasted_iota(jnp.int32, cos.shape, 1)
    lower = (lane & 31) < ROPE_NF
    out = []
    for s in range(x.shape[1] // LANES):
        xs = x[:, s * LANES:(s + 1) * LANES]
        partner = jnp.where(lower, pltpu.roll(xs, LANES - ROPE_NF, axis=1), pltpu.roll(xs, ROPE_NF, axis=1))
        out.append(xs * cos + partner * sin_signed)
    return jnp.concatenate(out, axis=1)


def _att_in_kernel(x_ref, mod_ref, g_ref, w_ref, cos_ref, sin_ref, q_ref, k_ref, v_ref, p_ref):
    i = pl.program_id(0)
    h = _modulate(x_ref[...], g_ref[...], mod_ref[0, 0:1], mod_ref[0, 1:2]).astype(BF16)
    proj = jnp.dot(h, w_ref[...], preferred_element_type=F32)
    q = proj[:, :DA_QK_W]
    k = proj[:, DA_QK_W:2 * DA_QK_W]
    v_ref[...] = proj[:, 2 * DA_QK_W:2 * DA_QK_W + DA_V_W]
    p_ref[...] = proj[:, 2 * DA_QK_W + DA_V_W:]
    latent = i >= ROWS_P // TM

    @pl.when(latent)
    def _():
        q_ref[...] = _rope(q, cos_ref[...], sin_ref[...])
        k_ref[...] = _rope(k, cos_ref[...], sin_ref[...])

    @pl.when(jnp.logical_not(latent))
    def _():
        q_ref[...] = q
        k_ref[...] = k


def _rope_tables():
    t = jnp.arange(DEC_SEQ, dtype=F32)
    r, col = jnp.floor(t / GRID_W), t % GRID_W
    inv = ROPE_THETA ** (-jnp.arange(ROPE_NF, dtype=F32) / ROPE_NF)
    ar, ac = r[:, None] * inv, col[:, None] * inv
    cos = jnp.concatenate([jnp.cos(ar), jnp.cos(ar), jnp.cos(ac), jnp.cos(ac)], axis=1)
    sin = jnp.concatenate([-jnp.sin(ar), jnp.sin(ar), -jnp.sin(ac), jnp.sin(ac)], axis=1)
    return jnp.tile(cos, (1, 2)), jnp.tile(sin, (1, 2))


def _att_in(x, mods, g, w, cos, sin):
    n_p, per_seq = ROWS_P // TM, DEC_SEQ // TM
    tab = pl.BlockSpec((TM, LANES), lambda i: (jnp.maximum(i - n_p, 0) % per_seq, 0))
    out = pl.BlockSpec((TM, DA_QK_W), lambda i: (i, 0))
    return pl.pallas_call(
        _att_in_kernel,
        grid=(ROWS // TM,),
        in_specs=[pl.BlockSpec((TM, D_MODEL), lambda i: (i, 0)),
                  pl.BlockSpec((1, 6, D_MODEL), lambda i: (_mod_index(i, TM), 0, 0)),
                  _const_spec((1, D_MODEL)),
                  _const_spec((D_MODEL, ATT_IN_W)),
                  tab, tab],
        out_specs=[out, out, out, out],
        out_shape=[jax.ShapeDtypeStruct((ROWS, DA_QK_W), F32)] * 4,
        compiler_params=_cparams("parallel"),
        name="att_in",
    )(x, mods, g, w, cos, sin)


def _attn_kernel(lam_ref, g_ref, q_ref, k_ref, v_ref, o_ref, *, lam_init):
    lv = lam_ref[...]
    lam = (jnp.exp(jnp.sum(lv[0:1] * lv[1:2], keepdims=True))
           - jnp.exp(jnp.sum(lv[2:3] * lv[3:4], keepdims=True)) + lam_init)
    q = q_ref[...]
    lane = lax.broadcasted_iota(jnp.int32, q.shape, 1)
    kb = k_ref[...].astype(BF16)
    dn = (((1,), (1,)), ((), ()))
    scale = DA_HD ** -0.5

    def probs(qc):
        s = lax.dot_general(qc.astype(BF16), kb, dn, preferred_element_type=F32) * scale
        e = jnp.exp(s - jnp.max(s, axis=-1, keepdims=True))
        return e, 1.0 / jnp.sum(e, axis=-1, keepdims=True)

    e1, r1 = probs(jnp.where(lane < DA_HD, q, 0.0))
    e2, r2 = probs(jnp.where(lane < DA_HD, 0.0, q))
    a = e1 * r1 - e2 * (lam * r2)
    o = jnp.dot(a.astype(BF16), v_ref[...].astype(BF16), preferred_element_type=F32)
    ms = jnp.mean(o * o, axis=-1, keepdims=True)
    o_ref[...] = o * lax.rsqrt(ms + EPS) * g_ref[...] * (1.0 - lam_init)


def _attention(q, k, v, lam_vecs, subln_g, *, batch, lq, lk, tq, q_row0, lam_init):
    nq = lq // tq
    q0 = q_row0 // tq
    hw = 2 * DA_HD
    return pl.pallas_call(
        functools.partial(_attn_kernel, lam_init=lam_init),
        grid=(batch, DA_HEADS, nq),
        in_specs=[_const_spec((4, DA_HD)),
                  _const_spec((1, hw)),
                  pl.BlockSpec((tq, hw), lambda b, h, i: (q0 + b * nq + i, h)),
                  pl.BlockSpec((lk, hw), lambda b, h, i: (b, h)),
                  pl.BlockSpec((lk, hw), lambda b, h, i: (b, h))],
        out_specs=pl.BlockSpec((tq, hw), lambda b, h, i: (b * nq + i, h)),
        out_shape=jax.ShapeDtypeStruct((batch * lq, DA_V_W), F32),
        compiler_params=_cparams("parallel", "parallel", "parallel"),
        name="diff_attn",
    )(lam_vecs, subln_g, q, k, v)


def _pool_kernel(x_ref, w_ref, sc_ref, o_ref):
    n = x_ref.shape[0]
    t = lax.broadcasted_iota(jnp.int32, (n, POOL_GC), 0)
    for gi, win in enumerate(POOL_WINDOWS):
        half = win // 2
        sl = slice(gi * POOL_GC, (gi + 1) * POOL_GC)
        xg = x_ref[:, sl]
        acc = xg
        for off in range(-half, half):
            if off == 0:
                continue
            sh = pltpu.roll(xg, (-off) % n, axis=0)
            acc = acc + jnp.where((t + off >= 0) & (t + off < n), sh, 0.0)
        cnt = (jnp.minimum(t + half, n) - jnp.maximum(t - half, 0)).astype(F32)
        pooled = acc / cnt - xg
        y = jnp.dot(pooled.astype(BF16), w_ref[gi], preferred_element_type=F32)
        o_ref[:, sl] = y * sc_ref[:, sl]


def _pool(xpool, w, scale, *, seq, row0, nseq):
    s0 = row0 // seq
    return pl.pallas_call(
        _pool_kernel,
        grid=(nseq,),
        in_specs=[pl.BlockSpec((seq, POOL_W), lambda b: (s0 + b, 0)),
                  _const_spec((POOL_GROUPS, POOL_GC, POOL_GC)),
                  _const_spec((1, POOL_W))],
        out_specs=pl.BlockSpec((seq, POOL_W), lambda b: (b, 0)),
        out_shape=jax.ShapeDtypeStruct((nseq * seq, POOL_W), F32),
        compiler_params=_cparams("parallel"),
        name="pool",
    )(xpool, w, scale)


def _proj_res_kernel(*refs, gate_row, n_in):
    x_ref, mod_ref = refs[0], refs[1]
    a_refs, w_refs, o_ref = refs[2:2 + n_in], refs[2 + n_in:2 + 2 * n_in], refs[-1]
    acc = None
    for a_ref, w_ref in zip(a_refs, w_refs):
        d = jnp.dot(a_ref[...].astype(BF16), w_ref[...], preferred_element_type=F32)
        acc = d if acc is None else acc + d
    o_ref[...] = x_ref[...] + mod_ref[0, gate_row:gate_row + 1] * acc


def _proj_res(x, mods, acts, ws, *, gate_row):
    n_in = len(acts)
    row = lambda width: pl.BlockSpec((TM, width), lambda i: (i, 0))
    return pl.pallas_call(
        functools.partial(_proj_res_kernel, gate_row=gate_row, n_in=n_in),
        grid=(ROWS // TM,),
        in_specs=[row(D_MODEL), pl.BlockSpec((1, 6, D_MODEL), lambda i: (_mod_index(i, TM), 0, 0))]
                 + [row(a.shape[1]) for a in acts] + [_const_spec(w.shape) for w in ws],
        out_specs=row(D_MODEL),
        out_shape=jax.ShapeDtypeStruct((ROWS, D_MODEL), F32),
        compiler_params=_cparams("parallel"),
        name="proj_res",
    )(x, mods, *acts, *ws)


def _ffn_kernel(x_ref, xp_ref, xn_ref, mod_ref, g_ref, wg_ref, wv_ref, cwg_ref, cwv_ref, cbg_ref, cbv_ref,
                wd_ref, fg_ref, o_ref, h_ref, acc_ref, *, final_norm):
    i, j = pl.program_id(0), pl.program_id(1)

    @pl.when(j == 0)
    def _():
        shift, scale, g = mod_ref[0, 3:4], mod_ref[0, 4:5], g_ref[...]
        h_ref[0:BF16_ROWS] = _modulate(xp_ref[...], g, shift, scale).astype(BF16)
        h_ref[BF16_ROWS:BF16_ROWS + TM] = _modulate(x_ref[...], g, shift, scale).astype(BF16)
        h_ref[BF16_ROWS + TM:] = _modulate(xn_ref[...], g, shift, scale).astype(BF16)
        acc_ref[...] = jnp.zeros_like(acc_ref)

    first, last = _seq_edge_masks(i, TM, (TM, TF))
    h = h_ref[...]
    ug = jnp.dot(h, wg_ref[...], preferred_element_type=F32)
    uv = jnp.dot(h, wv_ref[...], preferred_element_type=F32)
    cg = _conv3(ug, cwg_ref[...], cbg_ref[...], first, last, TM)
    cv = _conv3(uv, cwv_ref[...], cbv_ref[...], first, last, TM)
    act = (_silu(cg) * cv).astype(BF16)
    acc_ref[...] += jnp.dot(act, wd_ref[...], preferred_element_type=F32)

    @pl.when(j == pl.num_programs(1) - 1)
    def _():
        y = x_ref[...] + mod_ref[0, 5:6] * acc_ref[...]
        if final_norm:
            ms = jnp.mean(y * y, axis=-1, keepdims=True)
            y = y * lax.rsqrt(ms + EPS) * fg_ref[...]
        o_ref[...] = y


def _conv_ffn(x, mods, g, w_up, conv_w, conv_b, w_down, final_g, *, final_norm):
    nf = D_FF // TF
    prev, nxt = _halo_specs(TM, D_MODEL)
    return pl.pallas_call(
        functools.partial(_ffn_kernel, final_norm=final_norm),
        grid=(ROWS // TM, nf),
        in_specs=[pl.BlockSpec((TM, D_MODEL), lambda i, j: (i, 0)), prev, nxt,
                  pl.BlockSpec((1, 6, D_MODEL), lambda i, j: (_mod_index(i, TM), 0, 0)),
                  _const_spec((1, D_MODEL)),
                  pl.BlockSpec((D_MODEL, TF), lambda i, j: (0, j)),
                  pl.BlockSpec((D_MODEL, TF), lambda i, j: (0, nf + j)),
                  pl.BlockSpec((3, TF), lambda i, j: (0, j)),
                  pl.BlockSpec((3, TF), lambda i, j: (0, nf + j)),
                  pl.BlockSpec((1, TF), lambda i, j: (0, j)),
                  pl.BlockSpec((1, TF), lambda i, j: (0, nf + j)),
                  pl.BlockSpec((TF, D_MODEL), lambda i, j: (j, 0)),
                  _const_spec((1, D_MODEL))],
        out_specs=pl.BlockSpec((TM, D_MODEL), lambda i, j: (i, 0)),
        out_shape=jax.ShapeDtypeStruct((ROWS, D_MODEL), F32),
        scratch_shapes=[pltpu.VMEM((TM + 2 * BF16_ROWS, D_MODEL), BF16), pltpu.VMEM((TM, D_MODEL), F32)],
        compiler_params=_cparams("parallel", "arbitrary"),
        name="conv_ffn",
    )(x, x, x, mods, g, w_up, w_up, conv_w, conv_w, conv_b, conv_b, w_down, final_g)


XBC_COLS = 512


def _ssd_in_kernel(x_ref, xp_ref, xn_ref, mod_ref, g_ref, wz_ref, wx_ref, wdt_ref, cw_ref, cb_ref,
                   z_ref, xbc_ref, dt_ref, h_ref):
    i = pl.program_id(0)
    tm = TM_SSD_IN
    shift, scale, g = mod_ref[0, 0:1], mod_ref[0, 1:2], g_ref[...]
    h_ref[0:BF16_ROWS] = _modulate(xp_ref[...], g, shift, scale).astype(BF16)
    h_ref[BF16_ROWS:BF16_ROWS + tm] = _modulate(x_ref[...], g, shift, scale).astype(BF16)
    h_ref[BF16_ROWS + tm:] = _modulate(xn_ref[...], g, shift, scale).astype(BF16)
    hm = h_ref[BF16_ROWS:BF16_ROWS + tm]
    z_ref[...] = jnp.dot(hm, wz_ref[...], preferred_element_type=F32)
    dt_ref[...] = jnp.dot(hm, wdt_ref[...], preferred_element_type=F32)
    first, last = _seq_edge_masks(i, tm, (tm, XBC_COLS))
    h = h_ref[...]
    for c in range(SSD_XBC_W // XBC_COLS):
        sl = slice(c * XBC_COLS, (c + 1) * XBC_COLS)
        u = jnp.dot(h, wx_ref[:, sl], preferred_element_type=F32)
        xbc_ref[:, sl] = _silu(_conv3(u, cw_ref[:, sl], cb_ref[:, sl], first, last, tm))


def _ssd_in(x, mods, g, wz, wx, wdt, conv_w, conv_b):
    tm = TM_SSD_IN
    prev, nxt = _halo_specs(tm, D_MODEL)
    row = lambda width: pl.BlockSpec((tm, width), lambda i: (i, 0))
    return pl.pallas_call(
        _ssd_in_kernel,
        grid=(ROWS // tm,),
        in_specs=[row(D_MODEL), prev, nxt,
                  pl.BlockSpec((1, 6, D_MODEL), lambda i: (_mod_index(i, tm), 0, 0)),
                  _const_spec((1, D_MODEL)),
                  _const_spec(wz.shape), _const_spec(wx.shape), _const_spec(wdt.shape),
                  _const_spec(conv_w.shape), _const_spec(conv_b.shape)],
        out_specs=[row(SSD_DI), row(SSD_XBC_W), row(2 * SSD_HEADS)],
        out_shape=[jax.ShapeDtypeStruct((ROWS, SSD_DI), F32),
                   jax.ShapeDtypeStruct((ROWS, SSD_XBC_W), F32),
                   jax.ShapeDtypeStruct((ROWS, 2 * SSD_HEADS), F32)],
        scratch_shapes=[pltpu.VMEM((tm + 2 * BF16_ROWS, D_MODEL), BF16)],
        compiler_params=_cparams("parallel"),
        name="ssd_in",
    )(x, x, x, mods, g, wz, wx, wdt, conv_w, conv_b)


N_CHUNK_P = ROWS_P // SSD_CHUNK
NC_P = SEQ // SSD_CHUNK
NC_S = DEC_SEQ // SSD_CHUNK
N_CHUNK = ROWS // SSD_CHUNK


def _chunk_pos(c):
    nc = jnp.where(c < N_CHUNK_P, NC_P, NC_S)
    pos = jnp.where(c < N_CHUNK_P, c % NC_P, (c - N_CHUNK_P) % NC_S)
    return pos, nc


def _chunk_block(c, reverse):
    if not reverse:
        return c
    pos, nc = _chunk_pos(c)
    return c - pos + (nc - 1 - pos)


def _ssd_kernel(*refs, reverse, final):
    if final:
        (xbc_ref, dt_ref, dtb_ref, alog_ref, init_ref, yin_ref, z_ref, dexp_ref, ng_ref,
         y_ref, sfin_ref, s_ref, yacc_ref) = refs
    else:
        xbc_ref, dt_ref, dtb_ref, alog_ref, init_ref, y_ref, sfin_ref, s_ref = refs
    c = pl.program_id(0)
    pos, nc = _chunk_pos(c)
    q = SSD_CHUNK
    hh = SSD_HEADS

    @pl.when(pos == 0)
    def _():
        @pl.when(c < N_CHUNK_P)
        def _():
            s_ref[...] = jnp.zeros_like(s_ref)

        @pl.when(c >= N_CHUNK_P)
        def _():
            s_ref[...] = init_ref[0]

    d0 = hh if reverse else 0
    dt_in = dt_ref[:, d0:d0 + hh] + dtb_ref[:, d0:d0 + hh]
    dt = jnp.maximum(dt_in, 0.0) + jnp.log1p(jnp.exp(-jnp.abs(dt_in)))
    a = -jnp.exp(alog_ref[:, d0:d0 + hh])
    row = lax.broadcasted_iota(jnp.int32, (q, q), 0)
    col = lax.broadcasted_iota(jnp.int32, (q, q), 1)
    tri = (row <= col) if reverse else (row >= col)
    acs = jnp.dot(tri.astype(F32), dt * a, precision=HIGHEST, preferred_element_type=F32)
    a_end = acs[0:1] if reverse else acs[q - 1:q]
    wst = jnp.exp(a_end - acs) * dt
    dec = jnp.exp(a_end)
    eacs = jnp.exp(acs)
    mt = jnp.concatenate([dt, acs, wst, jnp.zeros_like(dt)], axis=1).T
    lane_lo = col < SSD_HEADDIM
    dn_t = (((1,), (1,)), ((), ()))

    def colb(m, h):
        return jnp.broadcast_to(m[:, h:h + 1], (q, q))

    ssq = jnp.zeros((q, 1), F32)
    for g in range(SSD_GROUPS):
        b_g = xbc_ref[:, SSD_DI + g * SSD_STATE:SSD_DI + (g + 1) * SSD_STATE]
        c_g = xbc_ref[:, SSD_DI + SSD_BC_W + g * SSD_STATE:SSD_DI + SSD_BC_W + (g + 1) * SSD_STATE]
        cb = lax.dot_general(c_g.astype(BF16), b_g.astype(BF16), dn_t, preferred_element_type=F32)
        bt = b_g.T
        for p in range(g * 4, (g + 1) * 4):
            sl = slice(p * LANES, (p + 1) * LANES)
            xp = xbc_ref[:, sl]
            sp = s_ref[:, sl]
            rhs_x = jnp.concatenate([jnp.where(lane_lo, xp, 0.0).astype(BF16),
                                     jnp.where(lane_lo, 0.0, xp).astype(BF16)], axis=0)
            rhs_s = jnp.concatenate([jnp.where(lane_lo, sp, 0.0).astype(BF16),
                                     jnp.where(lane_lo, 0.0, sp).astype(BF16)], axis=0)
            w_l, ce_l, bt_l = [], [], []
            for h in (2 * p, 2 * p + 1):
                seg = colb(acs, h) - mt[hh + h:hh + h + 1]
                lm = jnp.exp(jnp.where(tri, seg, -jnp.inf))
                w_l.append((cb * lm * mt[h:h + 1]).astype(BF16))
                ce_l.append((c_g * colb(eacs, h)).astype(BF16))
                bt_l.append((bt * mt[2 * hh + h:2 * hh + h + 1]).astype(BF16))
            yp = jnp.dot(jnp.concatenate(w_l + ce_l, axis=1), jnp.concatenate([rhs_x, rhs_s], axis=0),
                         preferred_element_type=F32)
            snew = jnp.dot(jnp.concatenate(bt_l, axis=1), rhs_x, preferred_element_type=F32)
            s_ref[:, sl] = sp * jnp.where(lane_lo, colb(dec, 2 * p), colb(dec, 2 * p + 1)) + snew
            if final:
                yt = (yp + yin_ref[:, sl] + xp * dexp_ref[:, sl]) * _silu(z_ref[:, sl])
                ssq = ssq + jnp.sum(yt * yt, axis=-1, keepdims=True)
                yacc_ref[:, sl] = yt
            else:
                y_ref[:, sl] = yp

    if final:
        inv = lax.rsqrt(ssq / SSD_DI + EPS)
        y_ref[...] = (yacc_ref[...] * inv * ng_ref[...]).astype(y_ref.dtype)

    @pl.when((pos == nc - 1) & (c < N_CHUNK_P))
    def _():
        sfin_ref[0] = s_ref[...].T


def _ssd_scan(xbc, dt, dt_bias, a_log, init, extra, *, reverse):
    final = extra is not None
    chunk = lambda width: pl.BlockSpec((SSD_CHUNK, width), lambda c: (_chunk_block(c, reverse), 0))
    in_specs = [chunk(SSD_XBC_W), chunk(2 * SSD_HEADS),
                _const_spec((1, 2 * SSD_HEADS)), _const_spec((1, 2 * SSD_HEADS)),
                pl.BlockSpec((1, SSD_STATE, SSD_DI),
                             lambda c: (jnp.clip((c - N_CHUNK_P) // NC_S, 0, DEC_BATCH - 1), 0, 0))]
    args = [xbc, dt, dt_bias, a_log, init]
    scratch = [pltpu.VMEM((SSD_STATE, SSD_DI), F32)]
    if final:
        in_specs += [chunk(SSD_DI), chunk(SSD_DI), _const_spec((1, SSD_DI)), _const_spec((1, SSD_DI))]
        args += list(extra)
        scratch.append(pltpu.VMEM((SSD_CHUNK, SSD_DI), F32))
    return pl.pallas_call(
        functools.partial(_ssd_kernel, reverse=reverse, final=final),
        grid=(N_CHUNK,),
        in_specs=in_specs,
        out_specs=[chunk(SSD_DI),
                   pl.BlockSpec((1, SSD_DI, SSD_STATE), lambda c: (jnp.minimum(c // NC_P, BATCH - 1), 0, 0))],
        out_shape=[jax.ShapeDtypeStruct((ROWS, SSD_DI), BF16 if final else F32),
                   jax.ShapeDtypeStruct((BATCH, SSD_DI, SSD_STATE), F32)],
        scratch_shapes=scratch,
        compiler_params=_cparams("arbitrary"),
        name="ssd_scan_bwd" if reverse else "ssd_scan_fwd",
    )(*args)


def kernel(x_prompt, x_sample, cache_k, cache_v, state_ssm_fwd, state_ssm_bwd, c, c_ctx, ada_w, ada_b, norm_mix_g, norm_ffn_g, att_w_in, att_lambda, att_subln_g, pool_w, pool_scale, att_w_out, ssd_w_in, ssd_conv_w, ssd_conv_b, ssd_dt_bias, ssd_a_log, ssd_d, ssd_norm_g, ssd_w_out, ffn_w_up, ffn_conv_w, ffn_conv_b, ffn_w_down, final_norm_g):
    x = jnp.concatenate([x_prompt.reshape(ROWS_P, D_MODEL), x_sample.reshape(ROWS_S, D_MODEL)], axis=0)
    cvec = jnp.concatenate([c_ctx[None], c, jnp.zeros((MOD_PAD - N_MOD, D_MODEL), F32)], axis=0)
    mods_all = _adaln(cvec, ada_w, ada_b).reshape(DEPTH, MOD_PAD, 6, D_MODEL)
    cos, sin = _rope_tables()
    outs = {}
    for l in range(DEPTH):
        i = l // 2
        mods = mods_all[l, :N_MOD]
        if l % 2 == 0:
            lam_init = 0.8 - 0.6 * math.exp(-0.3 * l)
            q, k, v, xpool = _att_in(x, mods, norm_mix_g[l][None], att_w_in[i].astype(BF16), cos, sin)
            k_lat = jnp.concatenate([k[ROWS_P:].reshape(DEC_BATCH, DEC_SEQ, DA_QK_W),
                                     cache_k[:, i].reshape(DEC_BATCH, PAST_LEN, DA_QK_W)], axis=1)
            v_lat = jnp.concatenate([v[ROWS_P:].reshape(DEC_BATCH, DEC_SEQ, DA_V_W),
                                     cache_v[:, i].reshape(DEC_BATCH, PAST_LEN, DA_V_W)], axis=1)
            lk = DEC_SEQ + PAST_LEN
            att = functools.partial(_attention, lam_vecs=att_lambda[i], subln_g=att_subln_g[i][None],
                                    lam_init=lam_init)
            o_p = att(q, k, v, batch=BATCH, lq=SEQ, lk=SEQ, tq=SEQ, q_row0=0)
            o_s = att(q, k_lat.reshape(DEC_BATCH * lk, DA_QK_W), v_lat.reshape(DEC_BATCH * lk, DA_V_W),
                      batch=DEC_BATCH, lq=DEC_SEQ, lk=lk, tq=TQ, q_row0=ROWS_P)
            pw, ps = pool_w[i].astype(BF16), pool_scale[i][None]
            pool_p = _pool(xpool, pw, ps, seq=SEQ, row0=0, nseq=BATCH)
            pool_s = _pool(xpool, pw, ps, seq=DEC_SEQ, row0=ROWS_P, nseq=DEC_BATCH)
            w_out = att_w_out[i].astype(BF16)
            x = _proj_res(x, mods, [jnp.concatenate([o_p, o_s], axis=0), jnp.concatenate([pool_p, pool_s], axis=0)],
                          [w_out[:DA_V_W], w_out[DA_V_W:]], gate_row=2)
            outs.setdefault("k", []).append(k[:ROWS_P].reshape(BATCH, SEQ, DA_HEADS, 2, DA_HD))
            outs.setdefault("v", []).append(v[:ROWS_P].reshape(BATCH, SEQ, DA_HEADS, 2 * DA_HD))
        else:
            w_in = ssd_w_in[i].astype(BF16)
            z, xbc, dt = _ssd_in(x, mods, norm_mix_g[l][None], w_in[:, :SSD_DI],
                                 w_in[:, SSD_DI:SSD_DI + SSD_XBC_W], w_in[:, SSD_DI + SSD_XBC_W:],
                                 ssd_conv_w[i], ssd_conv_b[i][None])
            to_scan_layout = lambda s: s.transpose(0, 3, 1, 2).reshape(DEC_BATCH, SSD_STATE, SSD_DI)
            dtb, alog = ssd_dt_bias[i].reshape(1, -1), ssd_a_log[i].reshape(1, -1)
            y_f, s_f = _ssd_scan(xbc, dt, dtb, alog, to_scan_layout(state_ssm_fwd[:, i]), None, reverse=False)
            d_exp = jnp.repeat(ssd_d[i], SSD_HEADDIM)[None]
            y, s_b = _ssd_scan(xbc, dt, dtb, alog, to_scan_layout(state_ssm_bwd[:, i]),
                               (y_f, z, d_exp, ssd_norm_g[i][None]), reverse=True)
            x = _proj_res(x, mods, [y], [ssd_w_out[i].astype(BF16)], gate_row=2)
            outs.setdefault("sf", []).append(s_f.reshape(BATCH, SSD_HEADS, SSD_HEADDIM, SSD_STATE))
            outs.setdefault("sb", []).append(s_b.reshape(BATCH, SSD_HEADS, SSD_HEADDIM, SSD_STATE))
        x = _conv_ffn(x, mods, norm_ffn_g[l][None], ffn_w_up[l].astype(BF16), ffn_conv_w[l], ffn_conv_b[l][None],
                      ffn_w_down[l].astype(BF16), final_norm_g[None], final_norm=(l == DEPTH - 1))
    y_prompt = x[:ROWS_P].reshape(BATCH, SEQ, D_MODEL)
    y_sample = x[ROWS_P:].reshape(DEC_BATCH, DEC_SEQ, D_MODEL)
    return (y_prompt, y_sample, jnp.stack(outs["k"], axis=1), jnp.stack(outs["v"], axis=1),
            jnp.stack(outs["sf"], axis=1), jnp.stack(outs["sb"], axis=1))
```

```python
import functools
import math

import jax
import jax.numpy as jnp
from jax import lax
from jax.experimental import pallas as pl
from jax.experimental.pallas import tpu as pltpu

F32 = jnp.float32
BF16 = jnp.bfloat16
HIGHEST = lax.Precision.HIGHEST

D_MODEL = 1024
BATCH = 32
SEQ = 256
DEPTH = 2
DEC_BATCH = 2
DEC_SEQ = 2048
PAST_LEN = 512
GRID_W = 64
EPS = 1e-6
DA_HEADS = 4
DA_HD = 64
DA_QK_W = DA_HEADS * 2 * DA_HD
DA_V_W = DA_HEADS * 2 * DA_HD
POOL_W = D_MODEL - DA_V_W
POOL_WINDOWS = (2, 4, 8, 16)
POOL_GROUPS = 4
POOL_GC = POOL_W // POOL_GROUPS
ATT_IN_W = 2 * DA_QK_W + DA_V_W + POOL_W
LOG2_E = math.log2(math.e)
ROPE_THETA = 10000.0
ROPE_NF = DA_HD // 4
SSD_DI = 2 * D_MODEL
SSD_HEADDIM = 64
SSD_HEADS = SSD_DI // SSD_HEADDIM
SSD_GROUPS = 4
SSD_STATE = 128
SSD_CHUNK = 128
SSD_BC_W = SSD_GROUPS * SSD_STATE
SSD_XBC_W = SSD_DI + 2 * SSD_BC_W
D_FF = 2816

ROWS_P = BATCH * SEQ
ROWS_S = DEC_BATCH * DEC_SEQ
ROWS = ROWS_P + ROWS_S
N_MOD = 1 + DEC_BATCH
MOD_PAD = 8

LANES = 128
SUBLANES = 8
HALO = 2 * SUBLANES
VMEM_LIMIT = 56 * 1024 * 1024

TM = 512
TM_CONV = 256
TF = 256
TQ = 256


def _cparams(*sem):
    return pltpu.CompilerParams(dimension_semantics=sem, vmem_limit_bytes=VMEM_LIMIT)


def _mod_index(i, tm):
    n_p, per_seq = ROWS_P // tm, DEC_SEQ // tm
    return jnp.where(i < n_p, 0, 1 + (i - n_p) // per_seq)


def _silu(x):
    return x / (1.0 + jnp.exp(-x))


def _modulate(x, g, shift, scale):
    ms = jnp.mean(x * x, axis=-1, keepdims=True)
    y = x * lax.rsqrt(ms + EPS) * g
    return y * (1.0 + scale) + shift


def _const_spec(shape):
    nd = len(shape)
    return pl.BlockSpec(shape, lambda *_: (0,) * nd, pipeline_mode=pl.Buffered(1))


def _group_specs(tm, width):
    n_p = ROWS_P // tm
    return (pl.BlockSpec((tm, width), lambda i: (jnp.minimum(i, n_p - 1), 0)),
            pl.BlockSpec((tm, width), lambda i: (jnp.maximum(i - n_p, 0), 0)))


def _group_rows(i, tm, p_ref, s_ref):
    return jnp.where(i < ROWS_P // tm, p_ref[...], s_ref[...])


def _halo_specs(tm, width):
    per = tm // SUBLANES
    last = ROWS // SUBLANES - 1
    prev = pl.BlockSpec((SUBLANES, width), lambda i: (jnp.maximum(i * per - 1, 0), 0))
    nxt = pl.BlockSpec((SUBLANES, width), lambda i: (jnp.minimum((i + 1) * per, last), 0))
    return prev, nxt


def _fill_with_halo(h_ref, x_ref, xp_ref, xn_ref, modulate, i, tm):
    m = jnp.where(i < ROWS_P // tm, SEQ - 1, DEC_SEQ - 1)
    starts = ((i * tm) & m) == 0
    ends = (((i + 1) * tm) & m) == 0
    h_ref[0:tm] = modulate(x_ref[...]).astype(BF16)
    halo = jnp.concatenate([jnp.where(ends, 0.0, modulate(xn_ref[...])),
                            jnp.where(starts, 0.0, modulate(xp_ref[...]))], axis=0)
    h_ref[tm:tm + HALO] = halo.astype(BF16)


def _conv3(u, cw, cb, tm):
    n = u.shape[0]
    up = pltpu.roll(u, 1, axis=0)[0:tm]
    un = pltpu.roll(u, n - 1, axis=0)[0:tm]
    return cw[1:2] * u[0:tm] + cb + cw[0:1] * up + cw[2:3] * un


def _adaln_kernel(c_ref, w_ref, b_ref, o_ref):
    s = _silu(c_ref[...])
    o_ref[0] = jnp.dot(s, w_ref[0], precision=HIGHEST, preferred_element_type=F32) + b_ref[0]


def _adaln(cvec, ada_w, ada_b):
    tn = 1536
    n = 6 * D_MODEL
    return pl.pallas_call(
        _adaln_kernel,
        grid=(DEPTH, n // tn),
        in_specs=[_const_spec((MOD_PAD, D_MODEL)),
                  pl.BlockSpec((1, D_MODEL, tn), lambda l, j: (l, 0, j)),
                  pl.BlockSpec((1, 1, tn), lambda l, j: (l, 0, j))],
        out_specs=pl.BlockSpec((1, MOD_PAD, tn), lambda l, j: (l, 0, j)),
        out_shape=jax.ShapeDtypeStruct((DEPTH, MOD_PAD, n), F32),
        compiler_params=_cparams("parallel", "parallel"),
        name="adaln",
    )(cvec, ada_w, ada_b.reshape(DEPTH, 1, n))


def _rope(x, cos, sin_signed):
    lane = lax.broadcasted_iota(jnp.int32, cos.shape, 1)
    lower = (lane & 31) < ROPE_NF
    out = []
    for s in range(x.shape[1] // LANES):
        xs = x[:, s * LANES:(s + 1) * LANES]
        partner = jnp.where(lower, pltpu.roll(xs, LANES - ROPE_NF, axis=1), pltpu.roll(xs, ROPE_NF, axis=1))
        out.append(xs * cos + partner * sin_signed)
    return jnp.concatenate(out, axis=1)


def _att_in_kernel(xp_ref, xs_ref, mod_ref, g_ref, w_ref, cos_ref, sin_ref, q_ref, k_ref, v_ref, p_ref):
    i = pl.program_id(0)
    x = _group_rows(i, TM, xp_ref, xs_ref)
    h = _modulate(x, g_ref[...], mod_ref[0, 0:1], mod_ref[0, 1:2]).astype(BF16)
    proj = jnp.dot(h, w_ref[...], preferred_element_type=F32)
    q = proj[:, :DA_QK_W]
    k = proj[:, DA_QK_W:2 * DA_QK_W]
    v_ref[...] = proj[:, 2 * DA_QK_W:2 * DA_QK_W + DA_V_W]
    p_ref[...] = proj[:, 2 * DA_QK_W + DA_V_W:]
    latent = i >= ROWS_P // TM

    @pl.when(latent)
    def _():
        q_ref[...] = _rope(q, cos_ref[...], sin_ref[...])
        k_ref[...] = _rope(k, cos_ref[...], sin_ref[...])

    @pl.when(jnp.logical_not(latent))
    def _():
        q_ref[...] = q
        k_ref[...] = k


def _rope_tables():
    t = jnp.arange(DEC_SEQ, dtype=F32)
    r, col = jnp.floor(t / GRID_W), t % GRID_W
    inv = ROPE_THETA ** (-jnp.arange(ROPE_NF, dtype=F32) / ROPE_NF)
    ar, ac = r[:, None] * inv, col[:, None] * inv
    cos = jnp.concatenate([jnp.cos(ar), jnp.cos(ar), jnp.cos(ac), jnp.cos(ac)], axis=1)
    sin = jnp.concatenate([-jnp.sin(ar), jnp.sin(ar), -jnp.sin(ac), jnp.sin(ac)], axis=1)
    return jnp.tile(cos, (1, 2)), jnp.tile(sin, (1, 2))


def _att_in(xp, xs, mods, g, w, cos, sin):
    n_p, per_seq = ROWS_P // TM, DEC_SEQ // TM
    tab = pl.BlockSpec((TM, LANES), lambda i: (jnp.maximum(i - n_p, 0) % per_seq, 0))
    out = pl.BlockSpec((TM, DA_QK_W), lambda i: (i, 0))
    return pl.pallas_call(
        _att_in_kernel,
        grid=(ROWS // TM,),
        in_specs=[*_group_specs(TM, D_MODEL),
                  pl.BlockSpec((1, 6, D_MODEL), lambda i: (_mod_index(i, TM), 0, 0)),
                  _const_spec((1, D_MODEL)),
                  _const_spec((D_MODEL, ATT_IN_W)),
                  tab, tab],
        out_specs=[out, out, out, out],
        out_shape=[jax.ShapeDtypeStruct((ROWS, DA_QK_W), F32)] * 4,
        compiler_params=_cparams("parallel"),
        name="att_in",
    )(xp, xs, mods, g, w, cos, sin)


def _attn_kernel(lam_ref, g_ref, q_ref, k_ref, v_ref, o_ref, *, lam_init):
    lv = lam_ref[...]
    lam = (jnp.exp(jnp.sum(lv[0:1] * lv[1:2], keepdims=True))
           - jnp.exp(jnp.sum(lv[2:3] * lv[3:4], keepdims=True)) + lam_init)
    dn = (((1,), (1,)), ((), ()))
    hw = 2 * DA_HD
    lane = lax.broadcasted_iota(jnp.int32, (q_ref.shape[0], hw), 1)
    for h in range(q_ref.shape[1] // hw):
        sl = slice(h * hw, (h + 1) * hw)
        q = q_ref[:, sl] * (DA_HD ** -0.5 * LOG2_E)
        kb = k_ref[:, sl].astype(BF16)
        vb = v_ref[:, sl].astype(BF16)

        def unnormalised(qc):
            s = lax.dot_general(qc.astype(BF16), kb, dn, preferred_element_type=F32)
            e = jnp.exp2(s - jnp.max(s, axis=-1, keepdims=True))
            return (jnp.dot(e.astype(BF16), vb, preferred_element_type=F32),
                    jnp.sum(e, axis=-1, keepdims=True))

        o1, l1 = unnormalised(jnp.where(lane < DA_HD, q, 0.0))
        o2, l2 = unnormalised(jnp.where(lane < DA_HD, 0.0, q))
        o = o1 / l1 - o2 * (lam / l2)
        ms = jnp.mean(o * o, axis=-1, keepdims=True)
        o_ref[:, sl] = o * lax.rsqrt(ms + EPS) * g_ref[...] * (1.0 - lam_init)


def _attention(q, k, v, lam_vecs, subln_g, *, batch, lq, lk, tq, q_row0, heads_per_step, lam_init):
    nq = lq // tq
    q0 = q_row0 // tq
    bw = heads_per_step * 2 * DA_HD
    return pl.pallas_call(
        functools.partial(_attn_kernel, lam_init=lam_init),
        grid=(batch, DA_HEADS // heads_per_step, nq),
        in_specs=[_const_spec((4, DA_HD)),
                  _const_spec((1, 2 * DA_HD)),
                  pl.BlockSpec((tq, bw), lambda b, h, i: (q0 + b * nq + i, h)),
                  pl.BlockSpec((lk, bw), lambda b, h, i: (b, h)),
                  pl.BlockSpec((lk, bw), lambda b, h, i: (b, h))],
        out_specs=pl.BlockSpec((tq, bw), lambda b, h, i: (b * nq + i, h)),
        out_shape=jax.ShapeDtypeStruct((batch * lq, DA_V_W), F32),
        compiler_params=_cparams("parallel", "parallel", "parallel"),
        name="diff_attn",
    )(lam_vecs, subln_g, q, k, v)


def _pool_kernel(x_ref, w_ref, sc_ref, o_ref):
    n = x_ref.shape[0]
    t = lax.broadcasted_iota(jnp.int32, (n, POOL_GC), 0)
    for gi, win in enumerate(POOL_WINDOWS):
        half = win // 2
        sl = slice(gi * POOL_GC, (gi + 1) * POOL_GC)
        xg = x_ref[:, sl]
        acc = xg
        for off in range(-half, half):
            if off == 0:
                continue
            sh = pltpu.roll(xg, (-off) % n, axis=0)
            acc = acc + jnp.where((t + off >= 0) & (t + off < n), sh, 0.0)
        cnt = (jnp.minimum(t + half, n) - jnp.maximum(t - half, 0)).astype(F32)
        pooled = acc / cnt - xg
        y = jnp.dot(pooled.astype(BF16), w_ref[gi], preferred_element_type=F32)
        o_ref[:, sl] = y * sc_ref[:, sl]


def _pool(xpool, w, scale, *, seq, row0, nseq):
    s0 = row0 // seq
    return pl.pallas_call(
        _pool_kernel,
        grid=(nseq,),
        in_specs=[pl.BlockSpec((seq, POOL_W), lambda b: (s0 + b, 0)),
                  _const_spec((POOL_GROUPS, POOL_GC, POOL_GC)),
                  _const_spec((1, POOL_W))],
        out_specs=pl.BlockSpec((seq, POOL_W), lambda b: (b, 0)),
        out_shape=jax.ShapeDtypeStruct((nseq * seq, POOL_W), F32),
        compiler_params=_cparams("parallel"),
        name="pool",
    )(xpool, w, scale)


def _proj_res_kernel(*refs, gate_row, grouped):
    i = pl.program_id(0)
    mod_ref, o_ref = refs[0], refs[-1]
    pos, rows = 1, []
    for is_pair in grouped:
        rows.append(_group_rows(i, TM, refs[pos], refs[pos + 1]) if is_pair else refs[pos][...])
        pos += 2 if is_pair else 1
    acc = None
    for a, w_ref in zip(rows[1:], refs[pos:-1]):
        d = jnp.dot(a.astype(BF16), w_ref[...], preferred_element_type=F32)
        acc = d if acc is None else acc + d
    o_ref[...] = rows[0] + mod_ref[0, gate_row:gate_row + 1] * acc


def _proj_res(x, mods, acts, ws, *, gate_row):
    operands, specs, grouped = [], [], []
    for a in (x, *acts):
        is_pair = isinstance(a, tuple)
        grouped.append(is_pair)
        if is_pair:
            operands += list(a)
            specs += list(_group_specs(TM, a[0].shape[1]))
        else:
            operands.append(a)
            specs.append(pl.BlockSpec((TM, a.shape[1]), lambda i: (i, 0)))
    return pl.pallas_call(
        functools.partial(_proj_res_kernel, gate_row=gate_row, grouped=tuple(grouped)),
        grid=(ROWS // TM,),
        in_specs=[pl.BlockSpec((1, 6, D_MODEL), lambda i: (_mod_index(i, TM), 0, 0))]
                 + specs + [_const_spec(w.shape) for w in ws],
        out_specs=pl.BlockSpec((TM, D_MODEL), lambda i: (i, 0)),
        out_shape=jax.ShapeDtypeStruct((ROWS, D_MODEL), F32),
        compiler_params=_cparams("parallel"),
        name="proj_res",
    )(mods, *operands, *ws)


def _ffn_kernel(x_ref, xp_ref, xn_ref, mod_ref, g_ref, wup_ref, cw_ref, cb_ref, wd_ref, fg_ref,
                *rest, last_layer):
    h_ref, act_ref = rest[-2:]
    i = pl.program_id(0)
    tm = TM_CONV
    modulate = functools.partial(_modulate, g=g_ref[...], shift=mod_ref[0, 3:4], scale=mod_ref[0, 4:5])
    _fill_with_halo(h_ref, x_ref, xp_ref, xn_ref, modulate, i, tm)
    h = h_ref[...]
    for c in range(D_FF // TF):
        gs, vs = slice(c * TF, (c + 1) * TF), slice(D_FF + c * TF, D_FF + (c + 1) * TF)
        ug = jnp.dot(h, wup_ref[:, gs], preferred_element_type=F32)
        uv = jnp.dot(h, wup_ref[:, vs], preferred_element_type=F32)
        cg = _conv3(ug, cw_ref[:, gs], cb_ref[:, gs], tm)
        cv = _conv3(uv, cw_ref[:, vs], cb_ref[:, vs], tm)
        act_ref[:, gs] = (_silu(cg) * cv).astype(BF16)
    y = x_ref[...] + mod_ref[0, 5:6] * jnp.dot(act_ref[...], wd_ref[...], preferred_element_type=F32)
    if not last_layer:
        rest[0][...] = y
        return
    ms = jnp.mean(y * y, axis=-1, keepdims=True)
    y = y * lax.rsqrt(ms + EPS) * fg_ref[...]
    op_ref, os_ref = rest[:2]
    prompt = i < ROWS_P // tm

    @pl.when(prompt)
    def _():
        op_ref[...] = y

    @pl.when(jnp.logical_not(prompt))
    def _():
        os_ref[...] = y


def _conv_ffn(x, mods, g, w_up, conv_w, conv_b, w_down, final_g, *, last_layer):
    tm = TM_CONV
    prev, nxt = _halo_specs(tm, D_MODEL)
    if last_layer:
        out_specs = list(_group_specs(tm, D_MODEL))
        out_shape = [jax.ShapeDtypeStruct((ROWS_P, D_MODEL), F32), jax.ShapeDtypeStruct((ROWS_S, D_MODEL), F32)]
    else:
        out_specs = pl.BlockSpec((tm, D_MODEL), lambda i: (i, 0))
        out_shape = jax.ShapeDtypeStruct((ROWS, D_MODEL), F32)
    return pl.pallas_call(
        functools.partial(_ffn_kernel, last_layer=last_layer),
        grid=(ROWS // tm,),
        in_specs=[pl.BlockSpec((tm, D_MODEL), lambda i: (i, 0)), prev, nxt,
                  pl.BlockSpec((1, 6, D_MODEL), lambda i: (_mod_index(i, tm), 0, 0)),
                  _const_spec((1, D_MODEL)),
                  _const_spec(w_up.shape), _const_spec(conv_w.shape), _const_spec(conv_b.shape),
                  _const_spec(w_down.shape), _const_spec((1, D_MODEL))],
        out_specs=out_specs,
        out_shape=out_shape,
        scratch_shapes=[pltpu.VMEM((tm + HALO, D_MODEL), BF16), pltpu.VMEM((tm, D_FF), BF16)],
        compiler_params=_cparams("arbitrary"),
        name="conv_ffn",
    )(x, x, x, mods, g, w_up, conv_w, conv_b, w_down, final_g)


XBC_COLS = 512


def _ssd_in_kernel(x_ref, xp_ref, xn_ref, mod_ref, g_ref, wz_ref, wx_ref, wdt_ref, cw_ref, cb_ref,
                   z_ref, xbc_ref, dt_ref, h_ref):
    i = pl.program_id(0)
    tm = TM_CONV
    modulate = functools.partial(_modulate, g=g_ref[...], shift=mod_ref[0, 0:1], scale=mod_ref[0, 1:2])
    _fill_with_halo(h_ref, x_ref, xp_ref, xn_ref, modulate, i, tm)
    hm = h_ref[0:tm]
    z_ref[...] = jnp.dot(hm, wz_ref[...], preferred_element_type=F32)
    dt_ref[...] = jnp.dot(hm, wdt_ref[...], preferred_element_type=F32)
    h = h_ref[...]
    for c in range(SSD_XBC_W // XBC_COLS):
        sl = slice(c * XBC_COLS, (c + 1) * XBC_COLS)
        u = jnp.dot(h, wx_ref[:, sl], preferred_element_type=F32)
        xbc_ref[:, sl] = _silu(_conv3(u, cw_ref[:, sl], cb_ref[:, sl], tm))


def _ssd_in(x, mods, g, wz, wx, wdt, conv_w, conv_b):
    tm = TM_CONV
    prev, nxt = _halo_specs(tm, D_MODEL)
    row = lambda width: pl.BlockSpec((tm, width), lambda i: (i, 0))
    return pl.pallas_call(
        _ssd_in_kernel,
        grid=(ROWS // tm,),
        in_specs=[row(D_MODEL), prev, nxt,
                  pl.BlockSpec((1, 6, D_MODEL), lambda i: (_mod_index(i, tm), 0, 0)),
                  _const_spec((1, D_MODEL)),
                  _const_spec(wz.shape), _const_spec(wx.shape), _const_spec(wdt.shape),
                  _const_spec(conv_w.shape), _const_spec(conv_b.shape)],
        out_specs=[row(SSD_DI), row(SSD_XBC_W), row(2 * SSD_HEADS)],
        out_shape=[jax.ShapeDtypeStruct((ROWS, SSD_DI), F32),
                   jax.ShapeDtypeStruct((ROWS, SSD_XBC_W), F32),
                   jax.ShapeDtypeStruct((ROWS, 2 * SSD_HEADS), F32)],
        scratch_shapes=[pltpu.VMEM((tm + HALO, D_MODEL), BF16)],
        compiler_params=_cparams("parallel"),
        name="ssd_in",
    )(x, x, x, mods, g, wz, wx, wdt, conv_w, conv_b)


N_CHUNK_P = ROWS_P // SSD_CHUNK
NC_P = SEQ // SSD_CHUNK
NC_S = DEC_SEQ // SSD_CHUNK
N_CHUNK = ROWS // SSD_CHUNK


def _chunk_pos(c):
    nc = jnp.where(c < N_CHUNK_P, NC_P, NC_S)
    pos = jnp.where(c < N_CHUNK_P, c % NC_P, (c - N_CHUNK_P) % NC_S)
    return pos, nc


def _chunk_block(c, reverse):
    if not reverse:
        return c
    pos, nc = _chunk_pos(c)
    return c - pos + (nc - 1 - pos)


def _ssd_kernel(*refs, reverse, final):
    if final:
        (xbc_ref, dt_ref, dtb_ref, alog_ref, init_ref, yin_ref, z_ref, dexp_ref, ng_ref,
         y_ref, sfin_ref, s_ref, yacc_ref) = refs
    else:
        xbc_ref, dt_ref, dtb_ref, alog_ref, init_ref, y_ref, sfin_ref, s_ref = refs
    c = pl.program_id(0)
    pos, nc = _chunk_pos(c)
    q = SSD_CHUNK
    hh = SSD_HEADS

    @pl.when(pos == 0)
    def _():
        @pl.when(c < N_CHUNK_P)
        def _():
            s_ref[...] = jnp.zeros_like(s_ref)

        @pl.when(c >= N_CHUNK_P)
        def _():
            s_ref[...] = init_ref[0]

    d0 = hh if reverse else 0
    dt_in = dt_ref[:, d0:d0 + hh] + dtb_ref[:, d0:d0 + hh]
    dt = jnp.maximum(dt_in, 0.0) + jnp.log1p(jnp.exp(-jnp.abs(dt_in)))
    a = -jnp.exp(alog_ref[:, d0:d0 + hh])
    row = lax.broadcasted_iota(jnp.int32, (q, q), 0)
    col = lax.broadcasted_iota(jnp.int32, (q, q), 1)
    tri = (row <= col) if reverse else (row >= col)
    acs = jnp.dot(tri.astype(F32), dt * a, precision=HIGHEST, preferred_element_type=F32)
    a_end = acs[0:1] if reverse else acs[q - 1:q]
    wst = jnp.exp(a_end - acs) * dt
    dec = jnp.exp(a_end)
    eacs = jnp.exp(acs)
    mt = jnp.concatenate([dt, acs, wst, jnp.zeros_like(dt)], axis=1).T
    lane_lo = col < SSD_HEADDIM
    dn_t = (((1,), (1,)), ((), ()))

    def colb(m, h):
        return jnp.broadcast_to(m[:, h:h + 1], (q, q))

    ssq = jnp.zeros((q, 1), F32)
    for g in range(SSD_GROUPS):
        b_g = xbc_ref[:, SSD_DI + g * SSD_STATE:SSD_DI + (g + 1) * SSD_STATE]
        c_g = xbc_ref[:, SSD_DI + SSD_BC_W + g * SSD_STATE:SSD_DI + SSD_BC_W + (g + 1) * SSD_STATE]
        cb = lax.dot_general(c_g.astype(BF16), b_g.astype(BF16), dn_t, preferred_element_type=F32)
        bt = b_g.T
        for p in range(g * 4, (g + 1) * 4):
            sl = slice(p * LANES, (p + 1) * LANES)
            xp = xbc_ref[:, sl]
            sp = s_ref[:, sl]
            rhs_x = jnp.concatenate([jnp.where(lane_lo, xp, 0.0).astype(BF16),
                                     jnp.where(lane_lo, 0.0, xp).astype(BF16)], axis=0)
            rhs_s = jnp.concatenate([jnp.where(lane_lo, sp, 0.0).astype(BF16),
                                     jnp.where(lane_lo, 0.0, sp).astype(BF16)], axis=0)
            w_l, ce_l, bt_l = [], [], []
            for h in (2 * p, 2 * p + 1):
                seg = colb(acs, h) - mt[hh + h:hh + h + 1]
                lm = jnp.exp(jnp.where(tri, seg, -jnp.inf))
                w_l.append((cb * lm * mt[h:h + 1]).astype(BF16))
                ce_l.append((c_g * colb(eacs, h)).astype(BF16))
                bt_l.append((bt * mt[2 * hh + h:2 * hh + h + 1]).astype(BF16))
            yp = jnp.dot(jnp.concatenate(w_l + ce_l, axis=1), jnp.concatenate([rhs_x, rhs_s], axis=0),
                         preferred_element_type=F32)
            snew = jnp.dot(jnp.concatenate(bt_l, axis=1), rhs_x, preferred_element_type=F32)
            s_ref[:, sl] = sp * jnp.where(lane_lo, colb(dec, 2 * p), colb(dec, 2 * p + 1)) + snew
            if final:
                yt = (yp + yin_ref[:, sl] + xp * dexp_ref[:, sl]) * _silu(z_ref[:, sl])
                ssq = ssq + jnp.sum(yt * yt, axis=-1, keepdims=True)
                yacc_ref[:, sl] = yt
            else:
                y_ref[:, sl] = yp

    if final:
        inv = lax.rsqrt(ssq / SSD_DI + EPS)
        y_ref[...] = (yacc_ref[...] * inv * ng_ref[...]).astype(y_ref.dtype)

    @pl.when((pos == nc - 1) & (c < N_CHUNK_P))
    def _():
        sfin_ref[0] = s_ref[...].T


def _ssd_scan(xbc, dt, dt_bias, a_log, init, extra, *, reverse):
    final = extra is not None
    chunk = lambda width: pl.BlockSpec((SSD_CHUNK, width), lambda c: (_chunk_block(c, reverse), 0))
    in_specs = [chunk(SSD_XBC_W), chunk(2 * SSD_HEADS),
                _const_spec((1, 2 * SSD_HEADS)), _const_spec((1, 2 * SSD_HEADS)),
                pl.BlockSpec((1, SSD_STATE, SSD_DI),
                             lambda c: (jnp.clip((c - N_CHUNK_P) // NC_S, 0, DEC_BATCH - 1), 0, 0))]
    args = [xbc, dt, dt_bias, a_log, init]
    scratch = [pltpu.VMEM((SSD_STATE, SSD_DI), F32)]
    if final:
        in_specs += [chunk(SSD_DI), chunk(SSD_DI), _const_spec((1, SSD_DI)), _const_spec((1, SSD_DI))]
        args += list(extra)
        scratch.append(pltpu.VMEM((SSD_CHUNK, SSD_DI), F32))
    return pl.pallas_call(
        functools.partial(_ssd_kernel, reverse=reverse, final=final),
        grid=(N_CHUNK,),
        in_specs=in_specs,
        out_specs=[chunk(SSD_DI),
                   pl.BlockSpec((1, SSD_DI, SSD_STATE), lambda c: (jnp.minimum(c // NC_P, BATCH - 1), 0, 0))],
        out_shape=[jax.ShapeDtypeStruct((ROWS, SSD_DI), BF16 if final else F32),
                   jax.ShapeDtypeStruct((BATCH, SSD_DI, SSD_STATE), F32)],
        scratch_shapes=scratch,
        compiler_params=_cparams("arbitrary"),
        name="ssd_scan_bwd" if reverse else "ssd_scan_fwd",
    )(*args)


def kernel(x_prompt, x_sample, cache_k, cache_v, state_ssm_fwd, state_ssm_bwd, c, c_ctx, ada_w, ada_b, norm_mix_g, norm_ffn_g, att_w_in, att_lambda, att_subln_g, pool_w, pool_scale, att_w_out, ssd_w_in, ssd_conv_w, ssd_conv_b, ssd_dt_bias, ssd_a_log, ssd_d, ssd_norm_g, ssd_w_out, ffn_w_up, ffn_conv_w, ffn_conv_b, ffn_w_down, final_norm_g):
    x = (x_prompt.reshape(ROWS_P, D_MODEL), x_sample.reshape(ROWS_S, D_MODEL))
    cvec =jnp.concatenate([c_ctx[None], c, jnp.zeros((MOD_PAD - N_MOD, D_MODEL), F32)], axis=0)
    mods_all = _adaln(cvec, ada_w, ada_b).reshape(DEPTH, MOD_PAD, 6, D_MODEL)
    cos, sin = _rope_tables()
    outs = {}
    for l in range(DEPTH):
        i = l // 2
        mods = mods_all[l, :N_MOD]
        if l % 2 == 0:
            lam_init = 0.8 - 0.6 * math.exp(-0.3 * l)
            if not isinstance(x, tuple):
                x = (x[:ROWS_P], x[ROWS_P:])
            q, k, v, xpool = _att_in(*x, mods, norm_mix_g[l][None], att_w_in[i].astype(BF16), cos, sin)
            k_lat = jnp.concatenate([k[ROWS_P:].reshape(DEC_BATCH, DEC_SEQ, DA_QK_W),
                                     cache_k[:, i].reshape(DEC_BATCH, PAST_LEN, DA_QK_W)], axis=1)
            v_lat = jnp.concatenate([v[ROWS_P:].reshape(DEC_BATCH, DEC_SEQ, DA_V_W),
                                     cache_v[:, i].reshape(DEC_BATCH, PAST_LEN, DA_V_W)], axis=1)
            lk = DEC_SEQ + PAST_LEN
            att = functools.partial(_attention, lam_vecs=att_lambda[i], subln_g=att_subln_g[i][None],
                                    lam_init=lam_init)
            o_p = att(q, k, v, batch=BATCH, lq=SEQ, lk=SEQ, tq=SEQ, q_row0=0, heads_per_step=DA_HEADS)
            o_s = att(q, k_lat.reshape(DEC_BATCH * lk, DA_QK_W), v_lat.reshape(DEC_BATCH * lk, DA_V_W),
                      batch=DEC_BATCH, lq=DEC_SEQ, lk=lk, tq=TQ, q_row0=ROWS_P, heads_per_step=1)
            pw, ps = pool_w[i].astype(BF16), pool_scale[i][None]
            pool_p = _pool(xpool, pw, ps, seq=SEQ, row0=0, nseq=BATCH)
            pool_s = _pool(xpool, pw, ps, seq=DEC_SEQ, row0=ROWS_P, nseq=DEC_BATCH)
            w_out = att_w_out[i].astype(BF16)
            x = _proj_res(x, mods, [(o_p, o_s), (pool_p, pool_s)], [w_out[:DA_V_W], w_out[DA_V_W:]], gate_row=2)
            outs.setdefault("k", []).append(k[:ROWS_P].reshape(BATCH, SEQ, DA_HEADS, 2, DA_HD))
            outs.setdefault("v", []).append(v[:ROWS_P].reshape(BATCH, SEQ, DA_HEADS, 2 * DA_HD))
        else:
            w_in = ssd_w_in[i].astype(BF16)
            z, xbc, dt = _ssd_in(x, mods, norm_mix_g[l][None], w_in[:, :SSD_DI],
                                 w_in[:, SSD_DI:SSD_DI + SSD_XBC_W], w_in[:, SSD_DI + SSD_XBC_W:],
                                 ssd_conv_w[i], ssd_conv_b[i][None])
            to_scan_layout = lambda s: s.transpose(0, 3, 1, 2).reshape(DEC_BATCH, SSD_STATE, SSD_DI)
            dtb, alog = ssd_dt_bias[i].reshape(1, -1), ssd_a_log[i].reshape(1, -1)
            y_f, s_f = _ssd_scan(xbc, dt, dtb, alog, to_scan_layout(state_ssm_fwd[:, i]), None, reverse=False)
            d_exp = jnp.repeat(ssd_d[i], SSD_HEADDIM)[None]
            y, s_b = _ssd_scan(xbc, dt, dtb, alog, to_scan_layout(state_ssm_bwd[:, i]),
                               (y_f, z, d_exp, ssd_norm_g[i][None]), reverse=True)
            x = _proj_res(x, mods, [y], [ssd_w_out[i].astype(BF16)], gate_row=2)
            outs.setdefault("sf", []).append(s_f.reshape(BATCH, SSD_HEADS, SSD_HEADDIM, SSD_STATE))
            outs.setdefault("sb", []).append(s_b.reshape(BATCH, SSD_HEADS, SSD_HEADDIM, SSD_STATE))
        x = _conv_ffn(x, mods, norm_ffn_g[l][None], ffn_w_up[l].astype(BF16), ffn_conv_w[l], ffn_conv_b[l][None],
                      ffn_w_down[l].astype(BF16), final_norm_g[None], last_layer=(l == DEPTH - 1))
    y_prompt = x[0].reshape(BATCH, SEQ, D_MODEL)
    y_sample = x[1].reshape(DEC_BATCH, DEC_SEQ, D_MODEL)
    return (y_prompt, y_sample, jnp.stack(outs["k"], axis=1), jnp.stack(outs["v"], axis=1),
            jnp.stack(outs["sf"], axis=1), jnp.stack(outs["sb"], axis=1))
```

```python
import functools
import math

import jax
import jax.numpy as jnp
from jax import lax
from jax.experimental import pallas as pl
from jax.experimental.pallas import tpu as pltpu

F32 = jnp.float32
BF16 = jnp.bfloat16
HIGHEST = lax.Precision.HIGHEST

D_MODEL = 1024
BATCH = 32
SEQ = 256
DEPTH = 2
DEC_BATCH = 2
DEC_SEQ = 2048
PAST_LEN = 512
GRID_W = 64
EPS = 1e-6
DA_HEADS = 4
DA_HD = 64
DA_QK_W = DA_HEADS * 2 * DA_HD
DA_V_W = DA_HEADS * 2 * DA_HD
POOL_W = D_MODEL - DA_V_W
POOL_WINDOWS = (2, 4, 8, 16)
POOL_GROUPS = 4
POOL_GC = POOL_W // POOL_GROUPS
ATT_IN_W = 2 * DA_QK_W + DA_V_W + POOL_W
LOG2_E = math.log2(math.e)
ROPE_THETA = 10000.0
ROPE_NF = DA_HD // 4
SSD_DI = 2 * D_MODEL
SSD_HEADDIM = 64
SSD_HEADS = SSD_DI // SSD_HEADDIM
SSD_GROUPS = 4
SSD_STATE = 128
SSD_CHUNK = 128
SSD_BC_W = SSD_GROUPS * SSD_STATE
SSD_XBC_W = SSD_DI + 2 * SSD_BC_W
D_FF = 2816

ROWS_P = BATCH * SEQ
ROWS_S = DEC_BATCH * DEC_SEQ
ROWS = ROWS_P + ROWS_S
N_MOD = 1 + DEC_BATCH
MOD_PAD = 8

LANES = 128
SUBLANES = 8
HALO = 2 * SUBLANES
VMEM_LIMIT = 56 * 1024 * 1024

TM = 512
TM_CONV = 256
TF = 256
TQ = 256


def _cparams(*sem):
    return pltpu.CompilerParams(dimension_semantics=sem, vmem_limit_bytes=VMEM_LIMIT)


def _mod_index(i, tm):
    n_p, per_seq = ROWS_P // tm, DEC_SEQ // tm
    return jnp.where(i < n_p, 0, 1 + (i - n_p) // per_seq)


def _silu(x):
    return x / (1.0 + jnp.exp(-x))


def _modulate(x, g, shift, scale):
    ms = jnp.mean(x * x, axis=-1, keepdims=True)
    y = x * lax.rsqrt(ms + EPS) * g
    return y * (1.0 + scale) + shift


def _const_spec(shape):
    nd = len(shape)
    return pl.BlockSpec(shape, lambda *_: (0,) * nd, pipeline_mode=pl.Buffered(1))


def _layer_spec(stacked_shape, layer):
    nd = len(stacked_shape) - 1
    return pl.BlockSpec((None, *stacked_shape[1:]), lambda *_: (layer,) + (0,) * nd, pipeline_mode=pl.Buffered(1))


def _group_specs(tm, width):
    n_p = ROWS_P // tm
    return (pl.BlockSpec((tm, width), lambda i: (jnp.minimum(i, n_p - 1), 0)),
            pl.BlockSpec((tm, width), lambda i: (jnp.maximum(i - n_p, 0), 0)))


def _group_rows(i, tm, p_ref, s_ref):
    return jnp.where(i < ROWS_P // tm, p_ref[...], s_ref[...])


def _halo_specs(tm, width):
    per = tm // SUBLANES
    last = ROWS // SUBLANES - 1
    prev = pl.BlockSpec((SUBLANES, width), lambda i: (jnp.maximum(i * per - 1, 0), 0))
    nxt = pl.BlockSpec((SUBLANES, width), lambda i: (jnp.minimum((i + 1) * per, last), 0))
    return prev, nxt


def _fill_with_halo(h_ref, x_ref, xp_ref, xn_ref, modulate, i, tm):
    m = jnp.where(i < ROWS_P // tm, SEQ - 1, DEC_SEQ - 1)
    starts = ((i * tm) & m) == 0
    ends = (((i + 1) * tm) & m) == 0
    h_ref[0:tm] = modulate(x_ref[...]).astype(BF16)
    halo = jnp.concatenate([jnp.where(ends, 0.0, modulate(xn_ref[...])),
                            jnp.where(starts, 0.0, modulate(xp_ref[...]))], axis=0)
    h_ref[tm:tm + HALO] = halo.astype(BF16)


def _conv3(u, cw, cb, tm):
    n = u.shape[0]
    up = pltpu.roll(u, 1, axis=0)[0:tm]
    un = pltpu.roll(u, n - 1, axis=0)[0:tm]
    return cw[1:2] * u[0:tm] + cb + cw[0:1] * up + cw[2:3] * un


def _adaln_kernel(c_ref, w_ref, b_ref, o_ref):
    s = _silu(c_ref[...])
    o_ref[0] = jnp.dot(s, w_ref[0], precision=HIGHEST, preferred_element_type=F32) + b_ref[0]


def _adaln(cvec, ada_w, ada_b):
    tn = 1536
    n = 6 * D_MODEL
    return pl.pallas_call(
        _adaln_kernel,
        grid=(DEPTH, n // tn),
        in_specs=[_const_spec((MOD_PAD, D_MODEL)),
                  pl.BlockSpec((1, D_MODEL, tn), lambda l, j: (l, 0, j)),
                  pl.BlockSpec((1, 1, tn), lambda l, j: (l, 0, j))],
        out_specs=pl.BlockSpec((1, MOD_PAD, tn), lambda l, j: (l, 0, j)),
        out_shape=jax.ShapeDtypeStruct((DEPTH, MOD_PAD, n), F32),
        compiler_params=_cparams("parallel", "parallel"),
        name="adaln",
    )(cvec, ada_w, ada_b.reshape(DEPTH, 1, n))


def _rope(x, cos, sin_signed):
    lane = lax.broadcasted_iota(jnp.int32, cos.shape, 1)
    lower = (lane & 31) < ROPE_NF
    out = []
    for s in range(x.shape[1] // LANES):
        xs = x[:, s * LANES:(s + 1) * LANES]
        partner = jnp.where(lower, pltpu.roll(xs, LANES - ROPE_NF, axis=1), pltpu.roll(xs, ROPE_NF, axis=1))
        out.append(xs * cos + partner * sin_signed)
    return jnp.concatenate(out, axis=1)


def _att_in_kernel(xp_ref, xs_ref, mod_ref, g_ref, w_ref, cos_ref, sin_ref,
                   q_ref, kp_ref, ks_ref, vp_ref, vs_ref, p_ref):
    i = pl.program_id(0)
    x = _group_rows(i, TM, xp_ref, xs_ref)
    h = _modulate(x, g_ref[...], mod_ref[0, 0:1], mod_ref[0, 1:2]).astype(BF16)
    proj = jnp.dot(h, w_ref[...], preferred_element_type=F32)
    q = proj[:, :DA_QK_W]
    k = proj[:, DA_QK_W:2 * DA_QK_W]
    v = proj[:, 2 * DA_QK_W:2 * DA_QK_W + DA_V_W]
    p_ref[...] = proj[:, 2 * DA_QK_W + DA_V_W:]
    latent = i >= ROWS_P // TM

    @pl.when(latent)
    def _():
        q_ref[...] = _rope(q, cos_ref[...], sin_ref[...])
        ks_ref[...] = _rope(k, cos_ref[...], sin_ref[...])
        vs_ref[...] = v

    @pl.when(jnp.logical_not(latent))
    def _():
        q_ref[...] = q
        kp_ref[...] = k
        vp_ref[...] = v


def _rope_tables():
    t = jnp.arange(DEC_SEQ, dtype=F32)
    r, col = jnp.floor(t / GRID_W), t % GRID_W
    inv = ROPE_THETA ** (-jnp.arange(ROPE_NF, dtype=F32) / ROPE_NF)
    ar, ac = r[:, None] * inv, col[:, None] * inv
    cos = jnp.concatenate([jnp.cos(ar), jnp.cos(ar), jnp.cos(ac), jnp.cos(ac)], axis=1)
    sin = jnp.concatenate([-jnp.sin(ar), jnp.sin(ar), -jnp.sin(ac), jnp.sin(ac)], axis=1)
    return jnp.tile(cos, (1, 2)), jnp.tile(sin, (1, 2))


def _att_in(xp, xs, mods, g, w, cos, sin):
    n_p, per_seq = ROWS_P // TM, DEC_SEQ // TM
    tab = pl.BlockSpec((TM, LANES), lambda i: (jnp.maximum(i - n_p, 0) % per_seq, 0))
    out = pl.BlockSpec((TM, DA_QK_W), lambda i: (i, 0))
    out_p, out_s = _group_specs(TM, DA_QK_W)
    full, grp_p, grp_s = (jax.ShapeDtypeStruct((r, DA_QK_W), F32) for r in (ROWS, ROWS_P, ROWS_S))
    return pl.pallas_call(
        _att_in_kernel,
        grid=(ROWS // TM,),
        in_specs=[*_group_specs(TM, D_MODEL),
                  pl.BlockSpec((1, 6, D_MODEL), lambda i: (_mod_index(i, TM), 0, 0)),
                  _const_spec((1, D_MODEL)),
                  _const_spec((D_MODEL, ATT_IN_W)),
                  tab, tab],
        out_specs=[out, out_p, out_s, out_p, out_s, out],
        out_shape=[full, grp_p, grp_s, grp_p, grp_s, full],
        compiler_params=_cparams("arbitrary"),
        name="att_in",
    )(xp, xs, mods, g, w, cos, sin)


def _attn_kernel(lam_ref, g_ref, q_ref, k_ref, v_ref, *rest, lam_init):
    o_ref = rest[-1]
    lv = lam_ref[...]
    lam = (jnp.exp(jnp.sum(lv[0:1] * lv[1:2], keepdims=True))
           - jnp.exp(jnp.sum(lv[2:3] * lv[3:4], keepdims=True)) + lam_init)
    dn = (((1,), (1,)), ((), ()))
    hw = 2 * DA_HD
    lane = lax.broadcasted_iota(jnp.int32, (q_ref.shape[0], hw), 1)
    for h in range(q_ref.shape[1] // hw):
        sl = slice(h * hw, (h + 1) * hw)
        q = q_ref[:, sl] * (DA_HD ** -0.5 * LOG2_E)
        kb = k_ref[:, sl].astype(BF16)
        vb = v_ref[:, sl].astype(BF16)
        if len(rest) == 3:
            kb = jnp.concatenate([kb, rest[0][:, sl].astype(BF16)], axis=0)
            vb = jnp.concatenate([vb, rest[1][:, sl].astype(BF16)], axis=0)

        def unnormalised(qc):
            s = lax.dot_general(qc.astype(BF16), kb, dn, preferred_element_type=F32)
            e = jnp.exp2(s - jnp.max(s, axis=-1, keepdims=True))
            return (jnp.dot(e.astype(BF16), vb, preferred_element_type=F32),
                    jnp.sum(e, axis=-1, keepdims=True))

        o1, l1 = unnormalised(jnp.where(lane < DA_HD, q, 0.0))
        o2, l2 = unnormalised(jnp.where(lane < DA_HD, 0.0, q))
        o = o1 / l1 - o2 * (lam / l2)
        ms = jnp.mean(o * o, axis=-1, keepdims=True)
        o_ref[:, sl] = o * lax.rsqrt(ms + EPS) * g_ref[...] * (1.0 - lam_init)


def _attention(q, k, v, cached, lam_vecs, subln_g, *, batch, lq, tq, q_row0, heads_per_step, lam_init):
    nq = lq // tq
    q0 = q_row0 // tq
    bw = heads_per_step * 2 * DA_HD
    seq_block = lambda rows: pl.BlockSpec((rows, bw), lambda b, h, i: (b, h))
    cached = () if cached is None else cached
    return pl.pallas_call(
        functools.partial(_attn_kernel, lam_init=lam_init),
        grid=(batch, DA_HEADS // heads_per_step, nq),
        in_specs=[_const_spec((4, DA_HD)),
                  _const_spec((1, 2 * DA_HD)),
                  pl.BlockSpec((tq, bw), lambda b, h, i: (q0 + b * nq + i, h)),
                  seq_block(lq), seq_block(lq)] + [seq_block(PAST_LEN) for _ in cached],
        out_specs=pl.BlockSpec((tq, bw), lambda b, h, i: (b * nq + i, h)),
        out_shape=jax.ShapeDtypeStruct((batch * lq, DA_V_W), F32),
        compiler_params=_cparams("parallel", "parallel", "parallel"),
        name="diff_attn",
    )(lam_vecs, subln_g, q, k, v, *cached)


def _pool_kernel(x_ref, w_ref, sc_ref, o_ref):
    n = x_ref.shape[0]
    t = lax.broadcasted_iota(jnp.int32, (n, POOL_GC), 0)
    for gi, win in enumerate(POOL_WINDOWS):
        half = win // 2
        sl = slice(gi * POOL_GC, (gi + 1) * POOL_GC)
        xg = x_ref[:, sl]
        acc = xg
        for off in range(-half, half):
            if off == 0:
                continue
            sh = pltpu.roll(xg, (-off) % n, axis=0)
            acc = acc + jnp.where((t + off >= 0) & (t + off < n), sh, 0.0)
        cnt = (jnp.minimum(t + half, n) - jnp.maximum(t - half, 0)).astype(F32)
        pooled = acc / cnt - xg
        y = jnp.dot(pooled.astype(BF16), w_ref[gi], preferred_element_type=F32)
        o_ref[:, sl] = y * sc_ref[:, sl]


def _pool(xpool, w, scale, *, seq, row0, nseq):
    s0 = row0 // seq
    return pl.pallas_call(
        _pool_kernel,
        grid=(nseq,),
        in_specs=[pl.BlockSpec((seq, POOL_W), lambda b: (s0 + b, 0)),
                  _const_spec((POOL_GROUPS, POOL_GC, POOL_GC)),
                  _const_spec((1, POOL_W))],
        out_specs=pl.BlockSpec((seq, POOL_W), lambda b: (b, 0)),
        out_shape=jax.ShapeDtypeStruct((nseq * seq, POOL_W), F32),
        compiler_params=_cparams("parallel"),
        name="pool",
    )(xpool, w, scale)


def _proj_res_kernel(*refs, gate_row, grouped):
    i = pl.program_id(0)
    mod_ref, o_ref = refs[0], refs[-1]
    pos, rows = 1, []
    for is_pair in grouped:
        rows.append(_group_rows(i, TM, refs[pos], refs[pos + 1]) if is_pair else refs[pos][...])
        pos += 2 if is_pair else 1
    acc = None
    for a, w_ref in zip(rows[1:], refs[pos:-1]):
        d = jnp.dot(a.astype(BF16), w_ref[...], preferred_element_type=F32)
        acc = d if acc is None else acc + d
    o_ref[...] = rows[0] + mod_ref[0, gate_row:gate_row + 1] * acc


def _proj_res(x, mods, acts, ws, *, gate_row):
    operands, specs, grouped = [], [], []
    for a in (x, *acts):
        is_pair = isinstance(a, tuple)
        grouped.append(is_pair)
        if is_pair:
            operands += list(a)
            specs += list(_group_specs(TM, a[0].shape[1]))
        else:
            operands.append(a)
            specs.append(pl.BlockSpec((TM, a.shape[1]), lambda i: (i, 0)))
    return pl.pallas_call(
        functools.partial(_proj_res_kernel, gate_row=gate_row, grouped=tuple(grouped)),
        grid=(ROWS // TM,),
        in_specs=[pl.BlockSpec((1, 6, D_MODEL), lambda i: (_mod_index(i, TM), 0, 0))]
                 + specs + [_const_spec(w.shape) for w in ws],
        out_specs=pl.BlockSpec((TM, D_MODEL), lambda i: (i, 0)),
        out_shape=jax.ShapeDtypeStruct((ROWS, D_MODEL), F32),
        compiler_params=_cparams("parallel"),
        name="proj_res",
    )(mods, *operands, *ws)


def _ffn_kernel(x_ref, xp_ref, xn_ref, mod_ref, g_ref, wup_ref, cw_ref, cb_ref, wd_ref, fg_ref,
                *rest, last_layer):
    h_ref, act_ref = rest[-2:]
    i = pl.program_id(0)
    tm = TM_CONV
    modulate = functools.partial(_modulate, g=g_ref[...], shift=mod_ref[0, 3:4], scale=mod_ref[0, 4:5])
    _fill_with_halo(h_ref, x_ref, xp_ref, xn_ref, modulate, i, tm)
    h = h_ref[...]
    for c in range(D_FF // TF):
        gs, vs = slice(c * TF, (c + 1) * TF), slice(D_FF + c * TF, D_FF + (c + 1) * TF)
        ug = jnp.dot(h, wup_ref[:, gs], preferred_element_type=F32)
        uv = jnp.dot(h, wup_ref[:, vs], preferred_element_type=F32)
        cg = _conv3(ug, cw_ref[:, gs], cb_ref[:, gs], tm)
        cv = _conv3(uv, cw_ref[:, vs], cb_ref[:, vs], tm)
        act_ref[:, gs] = (_silu(cg) * cv).astype(BF16)
    y = x_ref[...] + mod_ref[0, 5:6] * jnp.dot(act_ref[...], wd_ref[...], preferred_element_type=F32)
    if not last_layer:
        rest[0][...] = y
        return
    ms = jnp.mean(y * y, axis=-1, keepdims=True)
    y = y * lax.rsqrt(ms + EPS) * fg_ref[...]
    op_ref, os_ref = rest[:2]
    prompt = i < ROWS_P // tm

    @pl.when(prompt)
    def _():
        op_ref[...] = y

    @pl.when(jnp.logical_not(prompt))
    def _():
        os_ref[...] = y


def _conv_ffn(x, mods, g, w_up, conv_w, conv_b, w_down, final_g, *, layer, last_layer):
    tm = TM_CONV
    prev, nxt = _halo_specs(tm, D_MODEL)
    if last_layer:
        out_specs = list(_group_specs(tm, D_MODEL))
        out_shape = [jax.ShapeDtypeStruct((ROWS_P, D_MODEL), F32), jax.ShapeDtypeStruct((ROWS_S, D_MODEL), F32)]
    else:
        out_specs = pl.BlockSpec((tm, D_MODEL), lambda i: (i, 0))
        out_shape = jax.ShapeDtypeStruct((ROWS, D_MODEL), F32)
    return pl.pallas_call(
        functools.partial(_ffn_kernel, last_layer=last_layer),
        grid=(ROWS // tm,),
        in_specs=[pl.BlockSpec((tm, D_MODEL), lambda i: (i, 0)), prev, nxt,
                  pl.BlockSpec((1, 6, D_MODEL), lambda i: (_mod_index(i, tm), 0, 0)),
                  _const_spec((1, D_MODEL)),
                  _layer_spec(w_up.shape, layer), _const_spec(conv_w.shape), _const_spec(conv_b.shape),
                  _layer_spec(w_down.shape, layer), _const_spec((1, D_MODEL))],
        out_specs=out_specs,
        out_shape=out_shape,
        scratch_shapes=[pltpu.VMEM((tm + HALO, D_MODEL), BF16), pltpu.VMEM((tm, D_FF), BF16)],
        compiler_params=_cparams("arbitrary"),
        name="conv_ffn",
    )(x, x, x, mods, g, w_up, conv_w, conv_b, w_down, final_g)


XBC_COLS = 256


def _ssd_in_kernel(x_ref, xp_ref, xn_ref, mod_ref, g_ref, wz_ref, wx_ref, wdt_ref, cw_ref, cb_ref,
                   z_ref, xbc_ref, dt_ref, h_ref):
    i = pl.program_id(0)
    tm = TM_CONV
    modulate = functools.partial(_modulate, g=g_ref[...], shift=mod_ref[0, 0:1], scale=mod_ref[0, 1:2])
    _fill_with_halo(h_ref, x_ref, xp_ref, xn_ref, modulate, i, tm)
    hm = h_ref[0:tm]
    z_ref[...] = jnp.dot(hm, wz_ref[...], preferred_element_type=F32)
    dt_ref[...] = jnp.dot(hm, wdt_ref[...], preferred_element_type=F32)
    h = h_ref[...]
    for c in range(SSD_XBC_W // XBC_COLS):
        sl = slice(c * XBC_COLS, (c + 1) * XBC_COLS)
        u = jnp.dot(h, wx_ref[:, sl], preferred_element_type=F32)
        xbc_ref[:, sl] = _silu(_conv3(u, cw_ref[:, sl], cb_ref[:, sl], tm))


def _ssd_in(x, mods, g, wz, wx, wdt, conv_w, conv_b):
    tm = TM_CONV
    prev, nxt = _halo_specs(tm, D_MODEL)
    row = lambda width: pl.BlockSpec((tm, width), lambda i: (i, 0))
    return pl.pallas_call(
        _ssd_in_kernel,
        grid=(ROWS // tm,),
        in_specs=[row(D_MODEL), prev, nxt,
                  pl.BlockSpec((1, 6, D_MODEL), lambda i: (_mod_index(i, tm), 0, 0)),
                  _const_spec((1, D_MODEL)),
                  _const_spec(wz.shape), _const_spec(wx.shape), _const_spec(wdt.shape),
                  _const_spec(conv_w.shape), _const_spec(conv_b.shape)],
        out_specs=[row(SSD_DI), row(SSD_XBC_W), row(2 * SSD_HEADS)],
        out_shape=[jax.ShapeDtypeStruct((ROWS, SSD_DI), F32),
                   jax.ShapeDtypeStruct((ROWS, SSD_XBC_W), F32),
                   jax.ShapeDtypeStruct((ROWS, 2 * SSD_HEADS), F32)],
        scratch_shapes=[pltpu.VMEM((tm + HALO, D_MODEL), BF16)],
        compiler_params=_cparams("parallel"),
        name="ssd_in",
    )(x, x, x, mods, g, wz, wx, wdt, conv_w, conv_b)


N_CHUNK_P = ROWS_P // SSD_CHUNK
NC_P = SEQ // SSD_CHUNK
NC_S = DEC_SEQ // SSD_CHUNK
N_CHUNK = ROWS // SSD_CHUNK


def _chunk_pos(c):
    nc = jnp.where(c < N_CHUNK_P, NC_P, NC_S)
    pos = jnp.where(c < N_CHUNK_P, c % NC_P, (c - N_CHUNK_P) % NC_S)
    return pos, nc


def _chunk_block(c, reverse):
    if not reverse:
        return c
    pos, nc = _chunk_pos(c)
    return c - pos + (nc - 1 - pos)


def _split3(v):
    hi = v.astype(BF16)
    r1 = v - hi.astype(F32)
    mid = r1.astype(BF16)
    return hi, mid, (r1 - mid.astype(F32)).astype(BF16)


def _ssd_kernel(*refs, reverse, final):
    if final:
        (xbc_ref, dt_ref, dtb_ref, alog_ref, init_ref, wide_ref, yin_ref, z_ref, dexp_ref, ng_ref,
         y_ref, sfin_ref, s_ref, yacc_ref) = refs
    else:
        xbc_ref, dt_ref, dtb_ref, alog_ref, init_ref, wide_ref, y_ref, sfin_ref, s_ref = refs
    c = pl.program_id(0)
    pos, nc = _chunk_pos(c)
    q = SSD_CHUNK
    hh = SSD_HEADS
    gw = SSD_DI // SSD_GROUPS

    @pl.when(pos == 0)
    def _():
        @pl.when(c < N_CHUNK_P)
        def _():
            s_ref[...] = jnp.zeros_like(s_ref)

        @pl.when(c >= N_CHUNK_P)
        def _():
            s_ref[...] = init_ref[0]

    d0 = hh if reverse else 0
    dt_in = dt_ref[:, d0:d0 + hh] + dtb_ref[:, d0:d0 + hh]
    dt = jnp.maximum(dt_in, 0.0) + jnp.log1p(jnp.exp(-jnp.abs(dt_in)))
    a = -jnp.exp(alog_ref[:, d0:d0 + hh])
    row = lax.broadcasted_iota(jnp.int32, (q, q), 0)
    col = lax.broadcasted_iota(jnp.int32, (q, q), 1)
    tri = (row <= col) if reverse else (row >= col)
    acs = jnp.dot(tri.astype(F32), dt * a, precision=HIGHEST, preferred_element_type=F32)
    a_end = acs[0:1] if reverse else acs[q - 1:q]
    factors = jnp.concatenate([jnp.exp(acs), jnp.exp(a_end - acs) * dt,
                               jnp.broadcast_to(jnp.exp(a_end), (SUBLANES, hh))], axis=0)
    wide = jnp.dot(jnp.concatenate(_split3(factors), axis=1), wide_ref[...], preferred_element_type=F32)
    eacs_w, wst_w, dec_w = wide[0:q], wide[q:2 * q], wide[2 * q:2 * q + 1]
    acs2 = acs * LOG2_E
    mt = jnp.concatenate([dt, acs2, jnp.zeros((q, 2 * hh), F32)], axis=1).T
    lane_lo = col < SSD_HEADDIM
    dn_t = (((1,), (1,)), ((), ()))

    for g in range(SSD_GROUPS):
        gs = slice(g * gw, (g + 1) * gw)
        b_g = xbc_ref[:, SSD_DI + g * SSD_STATE:SSD_DI + (g + 1) * SSD_STATE]
        c_g = xbc_ref[:, SSD_DI + SSD_BC_W + g * SSD_STATE:SSD_DI + SSD_BC_W + (g + 1) * SSD_STATE]
        cgb = c_g.astype(BF16)
        cb = lax.dot_general(cgb, b_g.astype(BF16), dn_t, preferred_element_type=F32)
        s_g = s_ref[:, gs]
        y_off = jnp.dot(cgb, s_g.astype(BF16), preferred_element_type=F32) * eacs_w[:, gs]
        x_g = xbc_ref[:, gs]
        s_new = jnp.dot(b_g.T.astype(BF16), (x_g * wst_w[:, gs]).astype(BF16), preferred_element_type=F32)
        s_ref[:, gs] = s_g * dec_w[:, gs] + s_new
        for p in range(gw // LANES):
            ps = slice(p * LANES, (p + 1) * LANES)
            sl = slice(g * gw + p * LANES, g * gw + (p + 1) * LANES)
            xp = x_g[:, ps]
            rhs = jnp.concatenate([jnp.where(lane_lo, xp, 0.0).astype(BF16),
                                   jnp.where(lane_lo, 0.0, xp).astype(BF16)], axis=0)
            w_l = []
            for h in (2 * (g * 4 + p), 2 * (g * 4 + p) + 1):
                seg = jnp.broadcast_to(acs2[:, h:h + 1], (q, q)) - mt[hh + h:hh + h + 1]
                lm = jnp.exp2(jnp.where(tri, seg, -jnp.inf))
                w_l.append((cb * lm * mt[h:h + 1]).astype(BF16))
            yp = jnp.dot(jnp.concatenate(w_l, axis=1), rhs, preferred_element_type=F32) + y_off[:, ps]
            if final:
                yacc_ref[:, sl] = yp
            else:
                y_ref[:, sl] = yp

    if final:
        yt = (yacc_ref[...] + yin_ref[...] + xbc_ref[:, 0:SSD_DI] * dexp_ref[...]) * _silu(z_ref[...])
        ms = jnp.mean(yt * yt, axis=-1, keepdims=True)
        y_ref[...] = (yt * lax.rsqrt(ms + EPS) * ng_ref[...]).astype(y_ref.dtype)

    @pl.when((pos == nc - 1) & (c < N_CHUNK_P))
    def _():
        sfin_ref[0] = s_ref[...].T


def _ssd_scan(xbc, dt, dt_bias, a_log, init, extra, *, reverse):
    final = extra is not None
    chunk = lambda width: pl.BlockSpec((SSD_CHUNK, width), lambda c: (_chunk_block(c, reverse), 0))
    widen = jnp.tile(jnp.repeat(jnp.eye(SSD_HEADS, dtype=BF16), SSD_HEADDIM, axis=1), (3, 1))
    in_specs = [chunk(SSD_XBC_W), chunk(2 * SSD_HEADS),
                _const_spec((1, 2 * SSD_HEADS)), _const_spec((1, 2 * SSD_HEADS)),
                pl.BlockSpec((1, SSD_STATE, SSD_DI),
                             lambda c: (jnp.clip((c - N_CHUNK_P) // NC_S, 0, DEC_BATCH - 1), 0, 0)),
                _const_spec(widen.shape)]
    args = [xbc, dt, dt_bias, a_log, init, widen]
    scratch = [pltpu.VMEM((SSD_STATE, SSD_DI), F32)]
    if final:
        in_specs += [chunk(SSD_DI), chunk(SSD_DI), _const_spec((1, SSD_DI)), _const_spec((1, SSD_DI))]
        args += list(extra)
        scratch.append(pltpu.VMEM((SSD_CHUNK, SSD_DI), F32))
    return pl.pallas_call(
        functools.partial(_ssd_kernel, reverse=reverse, final=final),
        grid=(N_CHUNK,),
        in_specs=in_specs,
        out_specs=[chunk(SSD_DI),
                   pl.BlockSpec((1, SSD_DI, SSD_STATE), lambda c: (jnp.minimum(c // NC_P, BATCH - 1), 0, 0))],
        out_shape=[jax.ShapeDtypeStruct((ROWS, SSD_DI), BF16 if final else F32),
                   jax.ShapeDtypeStruct((BATCH, SSD_DI, SSD_STATE), F32)],
        scratch_shapes=scratch,
        compiler_params=_cparams("arbitrary"),
        name="ssd_scan_bwd" if reverse else "ssd_scan_fwd",
    )(*args)


def kernel(x_prompt, x_sample, cache_k, cache_v, state_ssm_fwd, state_ssm_bwd, c, c_ctx, ada_w, ada_b, norm_mix_g, norm_ffn_g, att_w_in, att_lambda, att_subln_g, pool_w, pool_scale, att_w_out, ssd_w_in, ssd_conv_w, ssd_conv_b, ssd_dt_bias, ssd_a_log, ssd_d, ssd_norm_g, ssd_w_out, ffn_w_up, ffn_conv_w, ffn_conv_b, ffn_w_down, final_norm_g):
    x = (x_prompt.reshape(ROWS_P, D_MODEL), x_sample.reshape(ROWS_S, D_MODEL))
    cvec = jnp.concatenate([c_ctx[None], c, jnp.zeros((MOD_PAD - N_MOD, D_MODEL), F32)], axis=0)
    mods_all = _adaln(cvec, ada_w, ada_b).reshape(DEPTH, MOD_PAD, 6, D_MODEL)
    cos, sin = _rope_tables()
    ffn_up, ffn_down = ffn_w_up.astype(BF16), ffn_w_down.astype(BF16)
    outs = {}
    for l in range(DEPTH):
        i = l // 2
        mods = mods_all[l, :N_MOD]
        if l % 2 == 0:
            lam_init = 0.8 - 0.6 * math.exp(-0.3 * l)
            if not isinstance(x, tuple):
                x = (x[:ROWS_P], x[ROWS_P:])
            q, k_p, k_s, v_p, v_s, xpool = _att_in(*x, mods, norm_mix_g[l][None], att_w_in[i].astype(BF16), cos, sin)
            cached = (cache_k[:, i].reshape(DEC_BATCH * PAST_LEN, DA_QK_W),
                      cache_v[:, i].reshape(DEC_BATCH * PAST_LEN, DA_V_W))
            att = functools.partial(_attention, lam_vecs=att_lambda[i], subln_g=att_subln_g[i][None],
                                    lam_init=lam_init)
            o_p = att(q, k_p, v_p, None, batch=BATCH, lq=SEQ, tq=SEQ, q_row0=0, heads_per_step=DA_HEADS)
            o_s = att(q, k_s, v_s, cached, batch=DEC_BATCH, lq=DEC_SEQ, tq=TQ, q_row0=ROWS_P, heads_per_step=1)
            pw, ps = pool_w[i].astype(BF16), pool_scale[i][None]
            pool_p = _pool(xpool, pw, ps, seq=SEQ, row0=0, nseq=BATCH)
            pool_s = _pool(xpool, pw, ps, seq=DEC_SEQ, row0=ROWS_P, nseq=DEC_BATCH)
            w_out = att_w_out[i].astype(BF16)
            x = _proj_res(x, mods, [(o_p, o_s), (pool_p, pool_s)], [w_out[:DA_V_W], w_out[DA_V_W:]], gate_row=2)
            outs.setdefault("k", []).append(k_p.reshape(BATCH, SEQ, DA_HEADS, 2, DA_HD))
            outs.setdefault("v", []).append(v_p.reshape(BATCH, SEQ, DA_HEADS, 2 * DA_HD))
        else:
            w_in = ssd_w_in[i]
            z, xbc, dt = _ssd_in(x, mods, norm_mix_g[l][None], w_in[:, :SSD_DI].astype(BF16),
                                 w_in[:, SSD_DI:SSD_DI + SSD_XBC_W].astype(BF16),
                                 w_in[:, SSD_DI + SSD_XBC_W:].astype(BF16),
                                 ssd_conv_w[i], ssd_conv_b[i][None])
            to_scan_layout = lambda s: s.transpose(0, 3, 1, 2).reshape(DEC_BATCH, SSD_STATE, SSD_DI)
            dtb, alog = ssd_dt_bias[i].reshape(1, -1), ssd_a_log[i].reshape(1, -1)
            y_f, s_f = _ssd_scan(xbc, dt, dtb, alog, to_scan_layout(state_ssm_fwd[:, i]), None, reverse=False)
            d_exp = jnp.repeat(ssd_d[i], SSD_HEADDIM)[None]
            y, s_b = _ssd_scan(xbc, dt, dtb, alog, to_scan_layout(state_ssm_bwd[:, i]),
                               (y_f, z, d_exp, ssd_norm_g[i][None]), reverse=True)
            x = _proj_res(x, mods, [y], [ssd_w_out[i].astype(BF16)], gate_row=2)
            outs.setdefault("sf", []).append(s_f.reshape(BATCH, SSD_HEADS, SSD_HEADDIM, SSD_STATE))
            outs.setdefault("sb", []).append(s_b.reshape(BATCH, SSD_HEADS, SSD_HEADDIM, SSD_STATE))
        x = _conv_ffn(x, mods, norm_ffn_g[l][None], ffn_up, ffn_conv_w[l], ffn_conv_b[l][None],
                      ffn_down, final_norm_g[None], layer=l, last_layer=(l == DEPTH - 1))
    y_prompt = x[0].reshape(BATCH, SEQ, D_MODEL)
    y_sample = x[1].reshape(DEC_BATCH, DEC_SEQ, D_MODEL)
    return (y_prompt, y_sample, jnp.stack(outs["k"], axis=1), jnp.stack(outs["v"], axis=1),
            jnp.stack(outs["sf"], axis=1), jnp.stack(outs["sb"], axis=1))
```

```python
import functools
import math

import jax
import jax.numpy as jnp
from jax import lax
from jax.experimental import pallas as pl
from jax.experimental.pallas import tpu as pltpu

F32 = jnp.float32
BF16 = jnp.bfloat16
HIGHEST = lax.Precision.HIGHEST

D_MODEL = 1024
BATCH = 32
SEQ = 256
DEPTH = 2
DEC_BATCH = 2
DEC_SEQ = 2048
PAST_LEN = 512
GRID_W = 64
EPS = 1e-6
DA_HEADS = 4
DA_HD = 64
DA_QK_W = DA_HEADS * 2 * DA_HD
DA_V_W = DA_HEADS * 2 * DA_HD
POOL_W = D_MODEL - DA_V_W
POOL_WINDOWS = (2, 4, 8, 16)
POOL_GROUPS = 4
POOL_GC = POOL_W // POOL_GROUPS
ATT_IN_W = 2 * DA_QK_W + DA_V_W + POOL_W
LOG2_E = math.log2(math.e)
ROPE_THETA = 10000.0
ROPE_NF = DA_HD // 4
SSD_DI = 2 * D_MODEL
SSD_HEADDIM = 64
SSD_HEADS = SSD_DI // SSD_HEADDIM
SSD_GROUPS = 4
SSD_STATE = 128
SSD_CHUNK = 128
SSD_BC_W = SSD_GROUPS * SSD_STATE
SSD_XBC_W = SSD_DI + 2 * SSD_BC_W
D_FF = 2816

ROWS_P = BATCH * SEQ
ROWS_S = DEC_BATCH * DEC_SEQ
ROWS = ROWS_P + ROWS_S
N_MOD = 1 + DEC_BATCH
MOD_PAD = 8

LANES = 128
SUBLANES = 8
HALO = 2 * SUBLANES
VMEM_LIMIT = 56 * 1024 * 1024

TM = 512
TM_CONV = 256
TF = 256
TQ = 256


def _cparams(*sem):
    return pltpu.CompilerParams(dimension_semantics=sem, vmem_limit_bytes=VMEM_LIMIT)


def _mod_index(i, tm):
    n_p, per_seq = ROWS_P // tm, DEC_SEQ // tm
    return jnp.where(i < n_p, 0, 1 + (i - n_p) // per_seq)


def _silu(x):
    return x / (1.0 + jnp.exp(-x))


def _modulate(x, g, shift, scale):
    ms = jnp.mean(x * x, axis=-1, keepdims=True)
    y = x * lax.rsqrt(ms + EPS) * g
    return y * (1.0 + scale) + shift


def _const_spec(shape):
    nd = len(shape)
    return pl.BlockSpec(shape, lambda *_: (0,) * nd, pipeline_mode=pl.Buffered(1))


def _layer_spec(stacked_shape, layer):
    nd = len(stacked_shape) - 1
    return pl.BlockSpec((None, *stacked_shape[1:]), lambda *_: (layer,) + (0,) * nd, pipeline_mode=pl.Buffered(1))


def _group_specs(tm, width):
    n_p = ROWS_P // tm
    return (pl.BlockSpec((tm, width), lambda i: (jnp.minimum(i, n_p - 1), 0)),
            pl.BlockSpec((tm, width), lambda i: (jnp.maximum(i - n_p, 0), 0)))


def _group_rows(i, tm, p_ref, s_ref):
    return jnp.where(i < ROWS_P // tm, p_ref[...], s_ref[...])


def _halo_specs(tm, width):
    per = tm // SUBLANES
    last = ROWS // SUBLANES - 1
    prev = pl.BlockSpec((SUBLANES, width), lambda i: (jnp.maximum(i * per - 1, 0), 0))
    nxt = pl.BlockSpec((SUBLANES, width), lambda i: (jnp.minimum((i + 1) * per, last), 0))
    return prev, nxt


def _fill_with_halo(h_ref, x_ref, xp_ref, xn_ref, modulate, i, tm):
    m = jnp.where(i < ROWS_P // tm, SEQ - 1, DEC_SEQ - 1)
    starts = ((i * tm) & m) == 0
    ends = (((i + 1) * tm) & m) == 0
    h_ref[0:tm] = modulate(x_ref[...]).astype(BF16)
    halo = jnp.concatenate([jnp.where(ends, 0.0, modulate(xn_ref[...])),
                            jnp.where(starts, 0.0, modulate(xp_ref[...]))], axis=0)
    h_ref[tm:tm + HALO] = halo.astype(BF16)


def _conv3(u, cw, cb, tm):
    n = u.shape[0]
    up = pltpu.roll(u, 1, axis=0)[0:tm]
    un = pltpu.roll(u, n - 1, axis=0)[0:tm]
    return cw[1:2] * u[0:tm] + cb + cw[0:1] * up + cw[2:3] * un


def _adaln_kernel(c_ref, w_ref, b_ref, o_ref):
    s = _silu(c_ref[...])
    o_ref[0] = jnp.dot(s, w_ref[0], precision=HIGHEST, preferred_element_type=F32) + b_ref[0]


def _adaln(cvec, ada_w, ada_b):
    tn = 1536
    n = 6 * D_MODEL
    return pl.pallas_call(
        _adaln_kernel,
        grid=(DEPTH, n // tn),
        in_specs=[_const_spec((MOD_PAD, D_MODEL)),
                  pl.BlockSpec((1, D_MODEL, tn), lambda l, j: (l, 0, j)),
                  pl.BlockSpec((1, 1, tn), lambda l, j: (l, 0, j))],
        out_specs=pl.BlockSpec((1, MOD_PAD, tn), lambda l, j: (l, 0, j)),
        out_shape=jax.ShapeDtypeStruct((DEPTH, MOD_PAD, n), F32),
        compiler_params=_cparams("parallel", "parallel"),
        name="adaln",
    )(cvec, ada_w, ada_b.reshape(DEPTH, 1, n))


def _rope(x, cos, sin_signed):
    lane = lax.broadcasted_iota(jnp.int32, cos.shape, 1)
    lower = (lane & 31) < ROPE_NF
    out = []
    for s in range(x.shape[1] // LANES):
        xs = x[:, s * LANES:(s + 1) * LANES]
        partner = jnp.where(lower, pltpu.roll(xs, LANES - ROPE_NF, axis=1), pltpu.roll(xs, ROPE_NF, axis=1))
        out.append(xs * cos + partner * sin_signed)
    return jnp.concatenate(out, axis=1)


def _att_in_kernel(xp_ref, xs_ref, mod_ref, g_ref, w_ref, cos_ref, sin_ref,
                   q_ref, kp_ref, ks_ref, vp_ref, vs_ref, p_ref):
    i = pl.program_id(0)
    x = _group_rows(i, TM, xp_ref, xs_ref)
    h = _modulate(x, g_ref[...], mod_ref[0, 0:1], mod_ref[0, 1:2]).astype(BF16)
    proj = jnp.dot(h, w_ref[...], preferred_element_type=F32)
    q = proj[:, :DA_QK_W]
    k = proj[:, DA_QK_W:2 * DA_QK_W]
    v = proj[:, 2 * DA_QK_W:2 * DA_QK_W + DA_V_W]
    p_ref[...] = proj[:, 2 * DA_QK_W + DA_V_W:]
    latent = i >= ROWS_P // TM

    @pl.when(latent)
    def _():
        q_ref[...] = _rope(q, cos_ref[...], sin_ref[...])
        ks_ref[...] = _rope(k, cos_ref[...], sin_ref[...])
        vs_ref[...] = v

    @pl.when(jnp.logical_not(latent))
    def _():
        q_ref[...] = q
        kp_ref[...] = k
        vp_ref[...] = v


def _rope_tables():
    t = jnp.arange(DEC_SEQ, dtype=F32)
    r, col = jnp.floor(t / GRID_W), t % GRID_W
    inv = ROPE_THETA ** (-jnp.arange(ROPE_NF, dtype=F32) / ROPE_NF)
    ar, ac = r[:, None] * inv, col[:, None] * inv
    cos = jnp.concatenate([jnp.cos(ar), jnp.cos(ar), jnp.cos(ac), jnp.cos(ac)], axis=1)
    sin = jnp.concatenate([-jnp.sin(ar), jnp.sin(ar), -jnp.sin(ac), jnp.sin(ac)], axis=1)
    return jnp.tile(cos, (1, 2)), jnp.tile(sin, (1, 2))


def _att_in(xp, xs, mods, g, w, cos, sin):
    n_p, per_seq = ROWS_P // TM, DEC_SEQ // TM
    tab = pl.BlockSpec((TM, LANES), lambda i: (jnp.maximum(i - n_p, 0) % per_seq, 0))
    out = pl.BlockSpec((TM, DA_QK_W), lambda i: (i, 0))
    out_p, out_s = _group_specs(TM, DA_QK_W)
    full, grp_p, grp_s = (jax.ShapeDtypeStruct((r, DA_QK_W), F32) for r in (ROWS, ROWS_P, ROWS_S))
    return pl.pallas_call(
        _att_in_kernel,
        grid=(ROWS // TM,),
        in_specs=[*_group_specs(TM, D_MODEL),
                  pl.BlockSpec((1, 6, D_MODEL), lambda i: (_mod_index(i, TM), 0, 0)),
                  _const_spec((1, D_MODEL)),
                  _const_spec((D_MODEL, ATT_IN_W)),
                  tab, tab],
        out_specs=[out, out_p, out_s, out_p, out_s, out],
        out_shape=[full, grp_p, grp_s, grp_p, grp_s, full],
        compiler_params=_cparams("arbitrary"),
        name="att_in",
    )(xp, xs, mods, g, w, cos, sin)


def _attn_kernel(lam_ref, g_ref, q_ref, k_ref, v_ref, *rest, lam_init):
    o_ref = rest[-1]
    lv = lam_ref[...]
    lam = (jnp.exp(jnp.sum(lv[0:1] * lv[1:2], keepdims=True))
           - jnp.exp(jnp.sum(lv[2:3] * lv[3:4], keepdims=True)) + lam_init)
    dn = (((1,), (1,)), ((), ()))
    hw = 2 * DA_HD
    lane = lax.broadcasted_iota(jnp.int32, (q_ref.shape[0], hw), 1)
    for h in range(q_ref.shape[1] // hw):
        sl = slice(h * hw, (h + 1) * hw)
        q = q_ref[:, sl] * (DA_HD ** -0.5 * LOG2_E)
        kb = k_ref[:, sl].astype(BF16)
        vb = v_ref[:, sl].astype(BF16)
        if len(rest) == 3:
            kb = jnp.concatenate([kb, rest[0][:, sl].astype(BF16)], axis=0)
            vb = jnp.concatenate([vb, rest[1][:, sl].astype(BF16)], axis=0)

        def unnormalised(qc):
            s = lax.dot_general(qc.astype(BF16), kb, dn, preferred_element_type=F32)
            e = jnp.exp2(s - jnp.max(s, axis=-1, keepdims=True))
            return (jnp.dot(e.astype(BF16), vb, preferred_element_type=F32),
                    jnp.sum(e, axis=-1, keepdims=True))

        o1, l1 = unnormalised(jnp.where(lane < DA_HD, q, 0.0))
        o2, l2 = unnormalised(jnp.where(lane < DA_HD, 0.0, q))
        o = o1 / l1 - o2 * (lam / l2)
        ms = jnp.mean(o * o, axis=-1, keepdims=True)
        o_ref[:, sl] = (o * lax.rsqrt(ms + EPS) * g_ref[...] * (1.0 - lam_init)).astype(o_ref.dtype)


def _attention(q, k, v, cached, lam_vecs, subln_g, *, batch, lq, tq, q_row0, heads_per_step, lam_init):
    nq = lq // tq
    q0 = q_row0 // tq
    bw = heads_per_step * 2 * DA_HD
    seq_block = lambda rows: pl.BlockSpec((rows, bw), lambda b, h, i: (b, h))
    cached = () if cached is None else cached
    return pl.pallas_call(
        functools.partial(_attn_kernel, lam_init=lam_init),
        grid=(batch, DA_HEADS // heads_per_step, nq),
        in_specs=[_const_spec((4, DA_HD)),
                  _const_spec((1, 2 * DA_HD)),
                  pl.BlockSpec((tq, bw), lambda b, h, i: (q0 + b * nq + i, h)),
                  seq_block(lq), seq_block(lq)] + [seq_block(PAST_LEN) for _ in cached],
        out_specs=pl.BlockSpec((tq, bw), lambda b, h, i: (b * nq + i, h)),
        out_shape=jax.ShapeDtypeStruct((batch * lq, DA_V_W), BF16),
        compiler_params=_cparams("parallel", "parallel", "parallel"),
        name="diff_attn",
    )(lam_vecs, subln_g, q, k, v, *cached)


def _pool_kernel(x_ref, w_ref, sc_ref, o_ref):
    n = x_ref.shape[0]
    t = lax.broadcasted_iota(jnp.int32, (n, POOL_GC), 0)
    for gi, win in enumerate(POOL_WINDOWS):
        half = win // 2
        sl = slice(gi * POOL_GC, (gi + 1) * POOL_GC)
        xg = x_ref[:, sl]
        acc = xg
        for off in range(-half, half):
            if off == 0:
                continue
            sh = pltpu.roll(xg, (-off) % n, axis=0)
            acc = acc + jnp.where((t + off >= 0) & (t + off < n), sh, 0.0)
        cnt = (jnp.minimum(t + half, n) - jnp.maximum(t - half, 0)).astype(F32)
        pooled = acc / cnt - xg
        y = jnp.dot(pooled.astype(BF16), w_ref[gi], preferred_element_type=F32)
        o_ref[:, sl] = (y * sc_ref[:, sl]).astype(o_ref.dtype)


def _pool(xpool, w, scale, *, seq, row0, nseq):
    s0 = row0 // seq
    return pl.pallas_call(
        _pool_kernel,
        grid=(nseq,),
        in_specs=[pl.BlockSpec((seq, POOL_W), lambda b: (s0 + b, 0)),
                  _const_spec((POOL_GROUPS, POOL_GC, POOL_GC)),
                  _const_spec((1, POOL_W))],
        out_specs=pl.BlockSpec((seq, POOL_W), lambda b: (b, 0)),
        out_shape=jax.ShapeDtypeStruct((nseq * seq, POOL_W), BF16),
        compiler_params=_cparams("parallel"),
        name="pool",
    )(xpool, w, scale)


def _proj_res_kernel(*refs, gate_row, grouped):
    i = pl.program_id(0)
    mod_ref, o_ref = refs[0], refs[-1]
    pos, rows = 1, []
    for is_pair in grouped:
        rows.append(_group_rows(i, TM, refs[pos], refs[pos + 1]) if is_pair else refs[pos][...])
        pos += 2 if is_pair else 1
    acc = None
    for a, w_ref in zip(rows[1:], refs[pos:-1]):
        d = jnp.dot(a.astype(BF16), w_ref[...], preferred_element_type=F32)
        acc = d if acc is None else acc + d
    o_ref[...] = rows[0] + mod_ref[0, gate_row:gate_row + 1] * acc


def _proj_res(x, mods, acts, ws, *, gate_row):
    operands, specs, grouped = [], [], []
    for a in (x, *acts):
        is_pair = isinstance(a, tuple)
        grouped.append(is_pair)
        if is_pair:
            operands += list(a)
            specs += list(_group_specs(TM, a[0].shape[1]))
        else:
            operands.append(a)
            specs.append(pl.BlockSpec((TM, a.shape[1]), lambda i: (i, 0)))
    return pl.pallas_call(
        functools.partial(_proj_res_kernel, gate_row=gate_row, grouped=tuple(grouped)),
        grid=(ROWS // TM,),
        in_specs=[pl.BlockSpec((1, 6, D_MODEL), lambda i: (_mod_index(i, TM), 0, 0))]
                 + specs + [_const_spec(w.shape) for w in ws],
        out_specs=pl.BlockSpec((TM, D_MODEL), lambda i: (i, 0)),
        out_shape=jax.ShapeDtypeStruct((ROWS, D_MODEL), F32),
        compiler_params=_cparams("parallel"),
        name="proj_res",
    )(mods, *operands, *ws)


def _ffn_kernel(x_ref, xp_ref, xn_ref, mod_ref, g_ref, wup_ref, cw_ref, cb_ref, wd_ref, fg_ref,
                *rest, last_layer):
    h_ref, act_ref = rest[-2:]
    i = pl.program_id(0)
    tm = TM_CONV
    modulate = functools.partial(_modulate, g=g_ref[...], shift=mod_ref[0, 3:4], scale=mod_ref[0, 4:5])
    _fill_with_halo(h_ref, x_ref, xp_ref, xn_ref, modulate, i, tm)
    h = h_ref[...]
    for c in range(D_FF // TF):
        gs, vs = slice(c * TF, (c + 1) * TF), slice(D_FF + c * TF, D_FF + (c + 1) * TF)
        ug = jnp.dot(h, wup_ref[:, gs], preferred_element_type=F32)
        uv = jnp.dot(h, wup_ref[:, vs], preferred_element_type=F32)
        cg = _conv3(ug, cw_ref[:, gs], cb_ref[:, gs], tm)
        cv = _conv3(uv, cw_ref[:, vs], cb_ref[:, vs], tm)
        act_ref[:, gs] = (_silu(cg) * cv).astype(BF16)
    y = x_ref[...] + mod_ref[0, 5:6] * jnp.dot(act_ref[...], wd_ref[...], preferred_element_type=F32)
    if not last_layer:
        rest[0][...] = y
        return
    ms = jnp.mean(y * y, axis=-1, keepdims=True)
    y = y * lax.rsqrt(ms + EPS) * fg_ref[...]
    op_ref, os_ref = rest[:2]
    prompt = i < ROWS_P // tm

    @pl.when(prompt)
    def _():
        op_ref[...] = y

    @pl.when(jnp.logical_not(prompt))
    def _():
        os_ref[...] = y


def _conv_ffn(x, mods, g, w_up, conv_w, conv_b, w_down, final_g, *, layer, last_layer):
    tm = TM_CONV
    prev, nxt = _halo_specs(tm, D_MODEL)
    if last_layer:
        out_specs = list(_group_specs(tm, D_MODEL))
        out_shape = [jax.ShapeDtypeStruct((ROWS_P, D_MODEL), F32), jax.ShapeDtypeStruct((ROWS_S, D_MODEL), F32)]
    else:
        out_specs = pl.BlockSpec((tm, D_MODEL), lambda i: (i, 0))
        out_shape = jax.ShapeDtypeStruct((ROWS, D_MODEL), F32)
    return pl.pallas_call(
        functools.partial(_ffn_kernel, last_layer=last_layer),
        grid=(ROWS // tm,),
        in_specs=[pl.BlockSpec((tm, D_MODEL), lambda i: (i, 0)), prev, nxt,
                  pl.BlockSpec((1, 6, D_MODEL), lambda i: (_mod_index(i, tm), 0, 0)),
                  _const_spec((1, D_MODEL)),
                  _layer_spec(w_up.shape, layer), _const_spec(conv_w.shape), _const_spec(conv_b.shape),
                  _layer_spec(w_down.shape, layer), _const_spec((1, D_MODEL))],
        out_specs=out_specs,
        out_shape=out_shape,
        scratch_shapes=[pltpu.VMEM((tm + HALO, D_MODEL), BF16), pltpu.VMEM((tm, D_FF), BF16)],
        compiler_params=_cparams("arbitrary"),
        name="conv_ffn",
    )(x, x, x, mods, g, w_up, conv_w, conv_b, w_down, final_g)


XBC_COLS = 256


def _ssd_in_kernel(x_ref, xp_ref, xn_ref, mod_ref, g_ref, wz_ref, wx_ref, wdt_ref, cw_ref, cb_ref,
                   zp_ref, zs_ref, xbcp_ref, xbcs_ref, dtp_ref, dts_ref, h_ref):
    i = pl.program_id(0)
    tm = TM_CONV
    modulate = functools.partial(_modulate, g=g_ref[...], shift=mod_ref[0, 0:1], scale=mod_ref[0, 1:2])
    _fill_with_halo(h_ref, x_ref, xp_ref, xn_ref, modulate, i, tm)

    def project(z_ref, xbc_ref, dt_ref):
        hm = h_ref[0:tm]
        z_ref[...] = jnp.dot(hm, wz_ref[...], preferred_element_type=F32)
        dt_ref[...] = jnp.dot(hm, wdt_ref[...], preferred_element_type=F32)
        h = h_ref[...]
        for c in range(SSD_XBC_W // XBC_COLS):
            sl = slice(c * XBC_COLS, (c + 1) * XBC_COLS)
            u = jnp.dot(h, wx_ref[:, sl], preferred_element_type=F32)
            xbc_ref[:, sl] = _silu(_conv3(u, cw_ref[:, sl], cb_ref[:, sl], tm))

    prompt = i < ROWS_P // tm
    pl.when(prompt)(lambda: project(zp_ref, xbcp_ref, dtp_ref))
    pl.when(jnp.logical_not(prompt))(lambda: project(zs_ref, xbcs_ref, dts_ref))


def _ssd_in(x, mods, g, wz, wx, wdt, conv_w, conv_b):
    tm = TM_CONV
    prev, nxt = _halo_specs(tm, D_MODEL)
    widths = (SSD_DI, SSD_XBC_W, 2 * SSD_HEADS)
    outs = pl.pallas_call(
        _ssd_in_kernel,
        grid=(ROWS // tm,),
        in_specs=[pl.BlockSpec((tm, D_MODEL), lambda i: (i, 0)), prev, nxt,
                  pl.BlockSpec((1, 6, D_MODEL), lambda i: (_mod_index(i, tm), 0, 0)),
                  _const_spec((1, D_MODEL)),
                  _const_spec(wz.shape), _const_spec(wx.shape), _const_spec(wdt.shape),
                  _const_spec(conv_w.shape), _const_spec(conv_b.shape)],
        out_specs=[spec for w in widths for spec in _group_specs(tm, w)],
        out_shape=[jax.ShapeDtypeStruct((r, w), F32) for w in widths for r in (ROWS_P, ROWS_S)],
        scratch_shapes=[pltpu.VMEM((tm + HALO, D_MODEL), BF16)],
        compiler_params=_cparams("arbitrary"),
        name="ssd_in",
    )(x, x, x, mods, g, wz, wx, wdt, conv_w, conv_b)
    return tuple(zip(outs[0::2], outs[1::2]))


STREAMS = 2


def _split3(v):
    hi = v.astype(BF16)
    r1 = v - hi.astype(F32)
    mid = r1.astype(BF16)
    return hi, mid, (r1 - mid.astype(F32)).astype(BF16)


def _ssd_kernel(*refs, reverse, final, has_init, emit_state):
    refs = list(refs)
    xbc_ref, dt_ref, dtb_ref, alog_ref = refs[:4]
    del refs[:4]
    init_ref = refs.pop(0) if has_init else None
    wide_ref = refs.pop(0)
    if final:
        yin_ref, z_ref, dexp_ref, ng_ref = refs[:4]
        del refs[:4]
    y_ref = refs.pop(0)
    sfin_ref = refs.pop(0) if emit_state else None
    s_ref = refs.pop(0)
    yacc_ref = refs.pop(0) if final else None
    pos = pl.program_id(1)
    q = SSD_CHUNK
    hh = SSD_HEADS
    gw = SSD_DI // SSD_GROUPS

    @pl.when(pos == 0)
    def _():
        s_ref[...] = init_ref[...] if has_init else jnp.zeros_like(s_ref)

    d0 = hh if reverse else 0
    a = -jnp.exp(alog_ref[:, d0:d0 + hh])
    row = lax.broadcasted_iota(jnp.int32, (q, q), 0)
    col = lax.broadcasted_iota(jnp.int32, (q, q), 1)
    tri = (row <= col) if reverse else (row >= col)
    lane_lo = col < SSD_HEADDIM
    dn_t = (((1,), (1,)), ((), ()))

    def chunk_factors(k):
        dt_in = dt_ref[k, :, d0:d0 + hh] + dtb_ref[:, d0:d0 + hh]
        dt = jnp.maximum(dt_in, 0.0) + jnp.log1p(jnp.exp(-jnp.abs(dt_in)))
        acs = jnp.dot(tri.astype(F32), dt * a, precision=HIGHEST, preferred_element_type=F32)
        a_end = acs[0:1] if reverse else acs[q - 1:q]
        factors = jnp.concatenate([jnp.exp(acs), jnp.exp(a_end - acs) * dt,
                                   jnp.broadcast_to(jnp.exp(a_end), (SUBLANES, hh))], axis=0)
        wide = jnp.dot(jnp.concatenate(_split3(factors), axis=1), wide_ref[...], preferred_element_type=F32)
        acs2 = acs * LOG2_E
        mt = jnp.concatenate([dt, acs2, jnp.zeros((q, 2 * hh), F32)], axis=1).T
        return wide[0:q], wide[q:2 * q], wide[2 * q:2 * q + 1], acs2, mt

    def group(k, g, eacs_w, wst_w, dec_w, acs2, mt):
        gs = slice(g * gw, (g + 1) * gw)
        b_g = xbc_ref[k, :, SSD_DI + g * SSD_STATE:SSD_DI + (g + 1) * SSD_STATE]
        c_g = xbc_ref[k, :, SSD_DI + SSD_BC_W + g * SSD_STATE:SSD_DI + SSD_BC_W + (g + 1) * SSD_STATE]
        cgb = c_g.astype(BF16)
        cb = lax.dot_general(cgb, b_g.astype(BF16), dn_t, preferred_element_type=F32)
        s_g = s_ref[k, :, gs]
        y_off = jnp.dot(cgb, s_g.astype(BF16), preferred_element_type=F32) * eacs_w[:, gs]
        x_g = xbc_ref[k, :, gs]
        s_new = jnp.dot(b_g.T.astype(BF16), (x_g * wst_w[:, gs]).astype(BF16), preferred_element_type=F32)
        s_ref[k, :, gs] = s_g * dec_w[:, gs] + s_new
        for p in range(gw // LANES):
            ps = slice(p * LANES, (p + 1) * LANES)
            sl = slice(g * gw + p * LANES, g * gw + (p + 1) * LANES)
            xp = x_g[:, ps]
            rhs = jnp.concatenate([jnp.where(lane_lo, xp, 0.0).astype(BF16),
                                   jnp.where(lane_lo, 0.0, xp).astype(BF16)], axis=0)
            w_l = []
            for h in (2 * (g * 4 + p), 2 * (g * 4 + p) + 1):
                seg = jnp.broadcast_to(acs2[:, h:h + 1], (q, q)) - mt[hh + h:hh + h + 1]
                lm = jnp.exp2(jnp.where(tri, seg, -jnp.inf))
                w_l.append((cb * lm * mt[h:h + 1]).astype(BF16))
            yp = jnp.dot(jnp.concatenate(w_l, axis=1), rhs, preferred_element_type=F32) + y_off[:, ps]
            if final:
                yacc_ref[k, :, sl] = yp
            else:
                y_ref[k, :, sl] = yp

    per_stream = [chunk_factors(k) for k in range(STREAMS)]
    for g in range(SSD_GROUPS):
        for k in range(STREAMS):
            group(k, g, *per_stream[k])

    if final:
        for k in range(STREAMS):
            yt = (yacc_ref[k] + yin_ref[k] + xbc_ref[k, :, 0:SSD_DI] * dexp_ref[...]) * _silu(z_ref[k])
            ms = jnp.mean(yt * yt, axis=-1, keepdims=True)
            y_ref[k] = (yt * lax.rsqrt(ms + EPS) * ng_ref[...]).astype(y_ref.dtype)

    if emit_state:
        @pl.when(pos == pl.num_programs(1) - 1)
        def _():
            for k in range(STREAMS):
                sfin_ref[k] = s_ref[k].T


def _ssd_scan(xbc, dt, dt_bias, a_log, init, extra, *, reverse, nseq, emit_state):
    final = extra is not None
    has_init = init is not None
    assert nseq % STREAMS == 0 and (not has_init or nseq == STREAMS)
    nc = xbc.shape[0] // (nseq * SSD_CHUNK)
    npairs = nseq // STREAMS
    view = lambda a: a.reshape(npairs, STREAMS, nc, SSD_CHUNK, a.shape[-1])
    chunk = lambda width: pl.BlockSpec((None, STREAMS, None, SSD_CHUNK, width),
                                       lambda p, t: (p, 0, (nc - 1 - t) if reverse else t, 0, 0))
    widen = jnp.tile(jnp.repeat(jnp.eye(SSD_HEADS, dtype=BF16), SSD_HEADDIM, axis=1), (3, 1))
    in_specs = [chunk(SSD_XBC_W), chunk(2 * SSD_HEADS),
                _const_spec((1, 2 * SSD_HEADS)), _const_spec((1, 2 * SSD_HEADS))]
    args = [view(xbc), view(dt), dt_bias, a_log]
    if has_init:
        in_specs.append(_const_spec(init.shape))
        args.append(init)
    in_specs.append(_const_spec(widen.shape))
    args.append(widen)
    scratch = [pltpu.VMEM((STREAMS, SSD_STATE, SSD_DI), F32)]
    if final:
        y_other, z, d_exp, norm_g = extra
        in_specs += [chunk(SSD_DI), chunk(SSD_DI), _const_spec((1, SSD_DI)), _const_spec((1, SSD_DI))]
        args += [view(y_other), view(z), d_exp, norm_g]
        scratch.append(pltpu.VMEM((STREAMS, SSD_CHUNK, SSD_DI), F32))
    out_specs = [chunk(SSD_DI)]
    out_shape = [jax.ShapeDtypeStruct((npairs, STREAMS, nc, SSD_CHUNK, SSD_DI), BF16 if final else F32)]
    if emit_state:
        out_specs.append(pl.BlockSpec((None, STREAMS, SSD_DI, SSD_STATE), lambda p, t: (p, 0, 0, 0)))
        out_shape.append(jax.ShapeDtypeStruct((npairs, STREAMS, SSD_DI, SSD_STATE), F32))
    outs = pl.pallas_call(
        functools.partial(_ssd_kernel, reverse=reverse, final=final, has_init=has_init, emit_state=emit_state),
        grid=(npairs, nc),
        in_specs=in_specs,
        out_specs=out_specs,
        out_shape=out_shape,
        scratch_shapes=scratch,
        compiler_params=_cparams("arbitrary", "arbitrary"),
        name="ssd_scan_bwd" if reverse else "ssd_scan_fwd",
    )(*args)
    y = outs[0].reshape(xbc.shape[0], SSD_DI)
    return (y, outs[1].reshape(nseq, SSD_DI, SSD_STATE)) if emit_state else y


def kernel(x_prompt, x_sample, cache_k, cache_v, state_ssm_fwd, state_ssm_bwd, c, c_ctx, ada_w, ada_b, norm_mix_g, norm_ffn_g, att_w_in, att_lambda, att_subln_g, pool_w, pool_scale, att_w_out, ssd_w_in, ssd_conv_w, ssd_conv_b, ssd_dt_bias, ssd_a_log, ssd_d, ssd_norm_g, ssd_w_out, ffn_w_up, ffn_conv_w, ffn_conv_b, ffn_w_down, final_norm_g):
    x = (x_prompt.reshape(ROWS_P, D_MODEL), x_sample.reshape(ROWS_S, D_MODEL))
    cvec = jnp.concatenate([c_ctx[None], c, jnp.zeros((MOD_PAD - N_MOD, D_MODEL), F32)], axis=0)
    mods_all = _adaln(cvec, ada_w, ada_b).reshape(DEPTH, MOD_PAD, 6, D_MODEL)
    cos, sin = _rope_tables()
    ffn_up, ffn_down = ffn_w_up.astype(BF16), ffn_w_down.astype(BF16)
    outs = {}
    for l in range(DEPTH):
        i = l // 2
        mods = mods_all[l, :N_MOD]
        if l % 2 == 0:
            lam_init = 0.8 - 0.6 * math.exp(-0.3 * l)
            if not isinstance(x, tuple):
                x = (x[:ROWS_P], x[ROWS_P:])
            q, k_p, k_s, v_p, v_s, xpool = _att_in(*x, mods, norm_mix_g[l][None], att_w_in[i].astype(BF16), cos, sin)
            cached = (cache_k[:, i].reshape(DEC_BATCH * PAST_LEN, DA_QK_W),
                      cache_v[:, i].reshape(DEC_BATCH * PAST_LEN, DA_V_W))
            att = functools.partial(_attention, lam_vecs=att_lambda[i], subln_g=att_subln_g[i][None],
                                    lam_init=lam_init)
            o_p = att(q, k_p, v_p, None, batch=BATCH, lq=SEQ, tq=SEQ, q_row0=0, heads_per_step=DA_HEADS)
            o_s = att(q, k_s, v_s, cached, batch=DEC_BATCH, lq=DEC_SEQ, tq=TQ, q_row0=ROWS_P, heads_per_step=1)
            pw, ps = pool_w[i].astype(BF16), pool_scale[i][None]
            pool_p = _pool(xpool, pw, ps, seq=SEQ, row0=0, nseq=BATCH)
            pool_s = _pool(xpool, pw, ps, seq=DEC_SEQ, row0=ROWS_P, nseq=DEC_BATCH)
            w_out = att_w_out[i].astype(BF16)
            x = _proj_res(x, mods, [(o_p, o_s), (pool_p, pool_s)], [w_out[:DA_V_W], w_out[DA_V_W:]], gate_row=2)
            outs.setdefault("k", []).append(k_p.reshape(BATCH, SEQ, DA_HEADS, 2, DA_HD))
            outs.setdefault("v", []).append(v_p.reshape(BATCH, SEQ, DA_HEADS, 2 * DA_HD))
        else:
            w_in = ssd_w_in[i]
            z, xbc, dt = _ssd_in(x, mods, norm_mix_g[l][None], w_in[:, :SSD_DI].astype(BF16),
                                 w_in[:, SSD_DI:SSD_DI + SSD_XBC_W].astype(BF16),
                                 w_in[:, SSD_DI + SSD_XBC_W:].astype(BF16),
                                 ssd_conv_w[i], ssd_conv_b[i][None])
            to_scan_layout = lambda s: s.transpose(0, 3, 1, 2).reshape(DEC_BATCH, SSD_STATE, SSD_DI)
            dtb, alog = ssd_dt_bias[i].reshape(1, -1), ssd_a_log[i].reshape(1, -1)
            d_exp, norm_g = jnp.repeat(ssd_d[i], SSD_HEADDIM)[None], ssd_norm_g[i][None]
            scan_p = functools.partial(_ssd_scan, xbc[0], dt[0], dtb, alog, None, nseq=BATCH, emit_state=True)
            y_f, s_f = scan_p(None, reverse=False)
            y_p, s_b = scan_p((y_f, z[0], d_exp, norm_g), reverse=True)
            scan_s = functools.partial(_ssd_scan, xbc[1], dt[1], dtb, alog, nseq=DEC_BATCH, emit_state=False)
            y_f = scan_s(to_scan_layout(state_ssm_fwd[:, i]), None, reverse=False)
            y_s = scan_s(to_scan_layout(state_ssm_bwd[:, i]), (y_f, z[1], d_exp, norm_g), reverse=True)
            x = _proj_res(x, mods, [(y_p, y_s)], [ssd_w_out[i].astype(BF16)], gate_row=2)
            outs.setdefault("sf", []).append(s_f.reshape(BATCH, SSD_HEADS, SSD_HEADDIM, SSD_STATE))
            outs.setdefault("sb", []).append(s_b.reshape(BATCH, SSD_HEADS, SSD_HEADDIM, SSD_STATE))
        x = _conv_ffn(x, mods, norm_ffn_g[l][None], ffn_up, ffn_conv_w[l], ffn_conv_b[l][None],
                      ffn_down, final_norm_g[None], layer=l, last_layer=(l == DEPTH - 1))
    y_prompt = x[0].reshape(BATCH, SEQ, D_MODEL)
    y_sample = x[1].reshape(DEC_BATCH, DEC_SEQ, D_MODEL)
    return (y_prompt, y_sample, jnp.stack(outs["k"], axis=1), jnp.stack(outs["v"], axis=1),
            jnp.stack(outs["sf"], axis=1), jnp.stack(outs["sb"], axis=1))
```

```python
import functools
import math

import jax
import jax.numpy as jnp
from jax import lax
from jax.experimental import pallas as pl
from jax.experimental.pallas import tpu as pltpu

F32 = jnp.float32
BF16 = jnp.bfloat16
HIGHEST = lax.Precision.HIGHEST

D_MODEL = 1024
BATCH = 32
SEQ = 256
DEPTH = 2
DEC_BATCH = 2
DEC_SEQ = 2048
PAST_LEN = 512
GRID_W = 64
EPS = 1e-6
DA_HEADS = 4
DA_HD = 64
DA_QK_W = DA_HEADS * 2 * DA_HD
DA_V_W = DA_HEADS * 2 * DA_HD
POOL_W = D_MODEL - DA_V_W
POOL_WINDOWS = (2, 4, 8, 16)
POOL_GROUPS = 4
POOL_GC = POOL_W // POOL_GROUPS
ATT_IN_W = 2 * DA_QK_W + DA_V_W + POOL_W
LOG2_E = math.log2(math.e)
ROPE_THETA = 10000.0
ROPE_NF = DA_HD // 4
SSD_DI = 2 * D_MODEL
SSD_HEADDIM = 64
SSD_HEADS = SSD_DI // SSD_HEADDIM
SSD_GROUPS = 4
SSD_STATE = 128
SSD_CHUNK = 128
SSD_BC_W = SSD_GROUPS * SSD_STATE
SSD_XBC_W = SSD_DI + 2 * SSD_BC_W
D_FF = 2816

ROWS_P = BATCH * SEQ
ROWS_S = DEC_BATCH * DEC_SEQ
ROWS = ROWS_P + ROWS_S
N_MOD = 1 + DEC_BATCH
MOD_PAD = 8

LANES = 128
SUBLANES = 8
HALO = 2 * SUBLANES
VMEM_LIMIT = 56 * 1024 * 1024

TM = 512
TM_CONV = 512
TF = 256
TQ = 256


def _cparams(*sem):
    return pltpu.CompilerParams(dimension_semantics=sem, vmem_limit_bytes=VMEM_LIMIT)


def _mod_index(i, tm):
    n_p, per_seq = ROWS_P // tm, DEC_SEQ // tm
    return jnp.where(i < n_p, 0, 1 + (i - n_p) // per_seq)


def _silu(x):
    return x / (1.0 + jnp.exp(-x))


def _modulate(x, g, shift, scale):
    ms = jnp.mean(x * x, axis=-1, keepdims=True)
    y = x * lax.rsqrt(ms + EPS) * g
    return y * (1.0 + scale) + shift


def _const_spec(shape):
    nd = len(shape)
    return pl.BlockSpec(shape, lambda *_: (0,) * nd, pipeline_mode=pl.Buffered(1))


def _layer_spec(stacked_shape, layer):
    nd = len(stacked_shape) - 1
    return pl.BlockSpec((None, *stacked_shape[1:]), lambda *_: (layer,) + (0,) * nd, pipeline_mode=pl.Buffered(1))


def _group_specs(tm, width):
    n_p = ROWS_P // tm
    return (pl.BlockSpec((tm, width), lambda i: (jnp.minimum(i, n_p - 1), 0)),
            pl.BlockSpec((tm, width), lambda i: (jnp.maximum(i - n_p, 0), 0)))


def _group_rows(i, tm, p_ref, s_ref):
    return jnp.where(i < ROWS_P // tm, p_ref[...], s_ref[...])


def _halo_specs(tm, width):
    per = tm // SUBLANES
    last = ROWS // SUBLANES - 1
    prev = pl.BlockSpec((SUBLANES, width), lambda i: (jnp.maximum(i * per - 1, 0), 0))
    nxt = pl.BlockSpec((SUBLANES, width), lambda i: (jnp.minimum((i + 1) * per, last), 0))
    return prev, nxt


def _fill_with_halo(h_ref, x_ref, xp_ref, xn_ref, modulate, i, tm):
    m = jnp.where(i < ROWS_P // tm, SEQ - 1, DEC_SEQ - 1)
    starts = ((i * tm) & m) == 0
    ends = (((i + 1) * tm) & m) == 0
    h_ref[0:tm] = modulate(x_ref[...]).astype(BF16)
    halo = jnp.concatenate([jnp.where(ends, 0.0, modulate(xn_ref[...])),
                            jnp.where(starts, 0.0, modulate(xp_ref[...]))], axis=0)
    h_ref[tm:tm + HALO] = halo.astype(BF16)


def _conv3(u, cw, cb, tm, prompt):
    n = u.shape[0]
    up = pltpu.roll(u, 1, axis=0)[0:tm]
    un = pltpu.roll(u, n - 1, axis=0)[0:tm]
    if tm > SEQ and prompt is not False:
        s = SUBLANES
        sub = lax.broadcasted_iota(jnp.int32, (s, u.shape[1]), 0)
        at_edge = (lambda m: m) if prompt is True else (lambda m: prompt & m)
        ups, uns, lo = [], [], 0
        for edge in range(SEQ, tm, SEQ):
            ups += [up[lo:edge], jnp.where(at_edge(sub == 0), 0.0, up[edge:edge + s])]
            uns += [un[lo - s if lo else 0:edge - s], jnp.where(at_edge(sub == s - 1), 0.0, un[edge - s:edge])]
            lo = edge + s
        up = jnp.concatenate(ups + [up[lo:tm]], axis=0)
        un = jnp.concatenate(uns + [un[lo - s:tm]], axis=0)
    return cw[1:2] * u[0:tm] + cb + cw[0:1] * up + cw[2:3] * un


def _adaln_kernel(c_ref, w_ref, b_ref, o_ref):
    s = _silu(c_ref[...])
    o_ref[0] = jnp.dot(s, w_ref[0], precision=HIGHEST, preferred_element_type=F32) + b_ref[0]


def _adaln(cvec, ada_w, ada_b):
    tn = 1536
    n = 6 * D_MODEL
    return pl.pallas_call(
        _adaln_kernel,
        grid=(DEPTH, n // tn),
        in_specs=[_const_spec((MOD_PAD, D_MODEL)),
                  pl.BlockSpec((1, D_MODEL, tn), lambda l, j: (l, 0, j)),
                  pl.BlockSpec((1, 1, tn), lambda l, j: (l, 0, j))],
        out_specs=pl.BlockSpec((1, MOD_PAD, tn), lambda l, j: (l, 0, j)),
        out_shape=jax.ShapeDtypeStruct((DEPTH, MOD_PAD, n), F32),
        compiler_params=_cparams("parallel", "parallel"),
        name="adaln",
    )(cvec, ada_w, ada_b.reshape(DEPTH, 1, n))


def _rope(x, cos, sin_signed):
    lane = lax.broadcasted_iota(jnp.int32, cos.shape, 1)
    lower = (lane & 31) < ROPE_NF
    out = []
    for s in range(x.shape[1] // LANES):
        xs = x[:, s * LANES:(s + 1) * LANES]
        partner = jnp.where(lower, pltpu.roll(xs, LANES - ROPE_NF, axis=1), pltpu.roll(xs, ROPE_NF, axis=1))
        out.append(xs * cos + partner * sin_signed)
    return jnp.concatenate(out, axis=1)


def _att_in_kernel(xp_ref, xs_ref, mod_ref, g_ref, w_ref, cos_ref, sin_ref,
                   q_ref, kp_ref, ks_ref, vp_ref, vs_ref, p_ref):
    i = pl.program_id(0)
    x = _group_rows(i, TM, xp_ref, xs_ref)
    h = _modulate(x, g_ref[...], mod_ref[0, 0:1], mod_ref[0, 1:2]).astype(BF16)
    proj = jnp.dot(h, w_ref[...], preferred_element_type=F32)
    q = proj[:, :DA_QK_W]
    k = proj[:, DA_QK_W:2 * DA_QK_W]
    v = proj[:, 2 * DA_QK_W:2 * DA_QK_W + DA_V_W]
    p_ref[...] = proj[:, 2 * DA_QK_W + DA_V_W:]
    latent = i >= ROWS_P // TM

    @pl.when(latent)
    def _():
        q_ref[...] = _rope(q, cos_ref[...], sin_ref[...])
        ks_ref[...] = _rope(k, cos_ref[...], sin_ref[...])
        vs_ref[...] = v

    @pl.when(jnp.logical_not(latent))
    def _():
        q_ref[...] = q
        kp_ref[...] = k
        vp_ref[...] = v


def _rope_tables():
    t = jnp.arange(DEC_SEQ, dtype=F32)
    r, col = jnp.floor(t / GRID_W), t % GRID_W
    inv = ROPE_THETA ** (-jnp.arange(ROPE_NF, dtype=F32) / ROPE_NF)
    ar, ac = r[:, None] * inv, col[:, None] * inv
    cos = jnp.concatenate([jnp.cos(ar), jnp.cos(ar), jnp.cos(ac), jnp.cos(ac)], axis=1)
    sin = jnp.concatenate([-jnp.sin(ar), jnp.sin(ar), -jnp.sin(ac), jnp.sin(ac)], axis=1)
    return jnp.tile(cos, (1, 2)), jnp.tile(sin, (1, 2))


def _att_in(xp, xs, mods, g, w, cos, sin):
    n_p, per_seq = ROWS_P // TM, DEC_SEQ // TM
    tab = pl.BlockSpec((TM, LANES), lambda i: (jnp.maximum(i - n_p, 0) % per_seq, 0))
    out = pl.BlockSpec((TM, DA_QK_W), lambda i: (i, 0))
    out_p, out_s = _group_specs(TM, DA_QK_W)
    full, grp_p, grp_s = (jax.ShapeDtypeStruct((r, DA_QK_W), F32) for r in (ROWS, ROWS_P, ROWS_S))
    return pl.pallas_call(
        _att_in_kernel,
        grid=(ROWS // TM,),
        in_specs=[*_group_specs(TM, D_MODEL),
                  pl.BlockSpec((1, 6, D_MODEL), lambda i: (_mod_index(i, TM), 0, 0)),
                  _const_spec((1, D_MODEL)),
                  _const_spec((D_MODEL, ATT_IN_W)),
                  tab, tab],
        out_specs=[out, out_p, out_s, out_p, out_s, out],
        out_shape=[full, grp_p, grp_s, grp_p, grp_s, full],
        compiler_params=_cparams("arbitrary"),
        name="att_in",
    )(xp, xs, mods, g, w, cos, sin)


def _attn_kernel(lam_ref, g_ref, q_ref, k_ref, v_ref, *rest, lam_init):
    o_ref = rest[-1]
    lv = lam_ref[...]
    lam = (jnp.exp(jnp.sum(lv[0:1] * lv[1:2], keepdims=True))
           - jnp.exp(jnp.sum(lv[2:3] * lv[3:4], keepdims=True)) + lam_init)
    dn = (((1,), (1,)), ((), ()))
    hw = 2 * DA_HD
    lane = lax.broadcasted_iota(jnp.int32, (q_ref.shape[0], hw), 1)
    for h in range(q_ref.shape[1] // hw):
        sl = slice(h * hw, (h + 1) * hw)
        q = q_ref[:, sl] * (DA_HD ** -0.5 * LOG2_E)
        kb = k_ref[:, sl].astype(BF16)
        vb = v_ref[:, sl].astype(BF16)
        if len(rest) == 3:
            kb = jnp.concatenate([kb, rest[0][:, sl].astype(BF16)], axis=0)
            vb = jnp.concatenate([vb, rest[1][:, sl].astype(BF16)], axis=0)

        def unnormalised(qc):
            s = lax.dot_general(qc.astype(BF16), kb, dn, preferred_element_type=F32)
            e = jnp.exp2(s - jnp.max(s, axis=-1, keepdims=True))
            return (jnp.dot(e.astype(BF16), vb, preferred_element_type=F32),
                    jnp.sum(e, axis=-1, keepdims=True))

        o1, l1 = unnormalised(jnp.where(lane < DA_HD, q, 0.0))
        o2, l2 = unnormalised(jnp.where(lane < DA_HD, 0.0, q))
        o = o1 / l1 - o2 * (lam / l2)
        ms = jnp.mean(o * o, axis=-1, keepdims=True)
        o_ref[:, sl] = (o * lax.rsqrt(ms + EPS) * g_ref[...] * (1.0 - lam_init)).astype(o_ref.dtype)


def _attention(q, k, v, cached, lam_vecs, subln_g, *, batch, lq, tq, q_row0, heads_per_step, lam_init):
    nq = lq // tq
    q0 = q_row0 // tq
    bw = heads_per_step * 2 * DA_HD
    seq_block = lambda rows: pl.BlockSpec((rows, bw), lambda b, h, i: (b, h))
    cached = () if cached is None else cached
    return pl.pallas_call(
        functools.partial(_attn_kernel, lam_init=lam_init),
        grid=(batch, DA_HEADS // heads_per_step, nq),
        in_specs=[_const_spec((4, DA_HD)),
                  _const_spec((1, 2 * DA_HD)),
                  pl.BlockSpec((tq, bw), lambda b, h, i: (q0 + b * nq + i, h)),
                  seq_block(lq), seq_block(lq)] + [seq_block(PAST_LEN) for _ in cached],
        out_specs=pl.BlockSpec((tq, bw), lambda b, h, i: (b * nq + i, h)),
        out_shape=jax.ShapeDtypeStruct((batch * lq, DA_V_W), BF16),
        compiler_params=_cparams("parallel", "parallel", "parallel"),
        name="diff_attn",
    )(lam_vecs, subln_g, q, k, v, *cached)


def _pool_kernel(x_ref, w_ref, sc_ref, o_ref):
    n = x_ref.shape[0]
    t = lax.broadcasted_iota(jnp.int32, (n, POOL_GC), 0)
    for gi, win in enumerate(POOL_WINDOWS):
        half = win // 2
        sl = slice(gi * POOL_GC, (gi + 1) * POOL_GC)
        xg = x_ref[:, sl]
        acc = xg
        for off in range(-half, half):
            if off == 0:
                continue
            sh = pltpu.roll(xg, (-off) % n, axis=0)
            acc = acc + jnp.where((t + off >= 0) & (t + off < n), sh, 0.0)
        cnt = (jnp.minimum(t + half, n) - jnp.maximum(t - half, 0)).astype(F32)
        pooled = acc / cnt - xg
        y = jnp.dot(pooled.astype(BF16), w_ref[gi], preferred_element_type=F32)
        o_ref[:, sl] = (y * sc_ref[:, sl]).astype(o_ref.dtype)


def _pool(xpool, w, scale, *, seq, row0, nseq):
    s0 = row0 // seq
    return pl.pallas_call(
        _pool_kernel,
        grid=(nseq,),
        in_specs=[pl.BlockSpec((seq, POOL_W), lambda b: (s0 + b, 0)),
                  _const_spec((POOL_GROUPS, POOL_GC, POOL_GC)),
                  _const_spec((1, POOL_W))],
        out_specs=pl.BlockSpec((seq, POOL_W), lambda b: (b, 0)),
        out_shape=jax.ShapeDtypeStruct((nseq * seq, POOL_W), BF16),
        compiler_params=_cparams("parallel"),
        name="pool",
    )(xpool, w, scale)


def _proj_res_kernel(*refs, gate_row, grouped):
    i = pl.program_id(0)
    mod_ref, o_ref = refs[0], refs[-1]
    pos, rows = 1, []
    for is_pair in grouped:
        rows.append(_group_rows(i, TM, refs[pos], refs[pos + 1]) if is_pair else refs[pos][...])
        pos += 2 if is_pair else 1
    acc = None
    for a, w_ref in zip(rows[1:], refs[pos:-1]):
        d = jnp.dot(a.astype(BF16), w_ref[...], preferred_element_type=F32)
        acc = d if acc is None else acc + d
    o_ref[...] = rows[0] + mod_ref[0, gate_row:gate_row + 1] * acc


def _proj_res(x, mods, acts, ws, *, gate_row):
    operands, specs, grouped = [], [], []
    for a in (x, *acts):
        is_pair = isinstance(a, tuple)
        grouped.append(is_pair)
        if is_pair:
            operands += list(a)
            specs += list(_group_specs(TM, a[0].shape[1]))
        else:
            operands.append(a)
            specs.append(pl.BlockSpec((TM, a.shape[1]), lambda i: (i, 0)))
    return pl.pallas_call(
        functools.partial(_proj_res_kernel, gate_row=gate_row, grouped=tuple(grouped)),
        grid=(ROWS // TM,),
        in_specs=[pl.BlockSpec((1, 6, D_MODEL), lambda i: (_mod_index(i, TM), 0, 0))]
                 + specs + [_const_spec(w.shape) for w in ws],
        out_specs=pl.BlockSpec((TM, D_MODEL), lambda i: (i, 0)),
        out_shape=jax.ShapeDtypeStruct((ROWS, D_MODEL), F32),
        compiler_params=_cparams("parallel"),
        name="proj_res",
    )(mods, *operands, *ws)


def _ffn_kernel(x_ref, xp_ref, xn_ref, mod_ref, g_ref, wup_ref, cw_ref, cb_ref, wd_ref, fg_ref,
                *rest, last_layer):
    h_ref, act_ref = rest[-2:]
    i = pl.program_id(0)
    tm = TM_CONV
    modulate = functools.partial(_modulate, g=g_ref[...], shift=mod_ref[0, 3:4], scale=mod_ref[0, 4:5])
    _fill_with_halo(h_ref, x_ref, xp_ref, xn_ref, modulate, i, tm)
    h = h_ref[...]
    prompt = i < ROWS_P // tm
    for c in range(D_FF // TF):
        gs, vs = slice(c * TF, (c + 1) * TF), slice(D_FF + c * TF, D_FF + (c + 1) * TF)
        ug = jnp.dot(h, wup_ref[:, gs], preferred_element_type=F32)
        uv = jnp.dot(h, wup_ref[:, vs], preferred_element_type=F32)
        cg = _conv3(ug, cw_ref[:, gs], cb_ref[:, gs], tm, prompt)
        cv = _conv3(uv, cw_ref[:, vs], cb_ref[:, vs], tm, prompt)
        act_ref[:, gs] = (_silu(cg) * cv).astype(BF16)
    y = x_ref[...] + mod_ref[0, 5:6] * jnp.dot(act_ref[...], wd_ref[...], preferred_element_type=F32)
    if not last_layer:
        rest[0][...] = y
        return
    ms = jnp.mean(y * y, axis=-1, keepdims=True)
    y = y * lax.rsqrt(ms + EPS) * fg_ref[...]
    op_ref, os_ref = rest[:2]

    @pl.when(prompt)
    def _():
        op_ref[...] = y

    @pl.when(jnp.logical_not(prompt))
    def _():
        os_ref[...] = y


def _conv_ffn(x, mods, g, w_up, conv_w, conv_b, w_down, final_g, *, layer, last_layer):
    tm = TM_CONV
    prev, nxt = _halo_specs(tm, D_MODEL)
    if last_layer:
        out_specs = list(_group_specs(tm, D_MODEL))
        out_shape = [jax.ShapeDtypeStruct((ROWS_P, D_MODEL), F32), jax.ShapeDtypeStruct((ROWS_S, D_MODEL), F32)]
    else:
        out_specs = pl.BlockSpec((tm, D_MODEL), lambda i: (i, 0))
        out_shape = jax.ShapeDtypeStruct((ROWS, D_MODEL), F32)
    return pl.pallas_call(
        functools.partial(_ffn_kernel, last_layer=last_layer),
        grid=(ROWS // tm,),
        in_specs=[pl.BlockSpec((tm, D_MODEL), lambda i: (i, 0)), prev, nxt,
                  pl.BlockSpec((1, 6, D_MODEL), lambda i: (_mod_index(i, tm), 0, 0)),
                  _const_spec((1, D_MODEL)),
                  _layer_spec(w_up.shape, layer), _const_spec(conv_w.shape), _const_spec(conv_b.shape),
                  _layer_spec(w_down.shape, layer), _const_spec((1, D_MODEL))],
        out_specs=out_specs,
        out_shape=out_shape,
        scratch_shapes=[pltpu.VMEM((tm + HALO, D_MODEL), BF16), pltpu.VMEM((tm, D_FF), BF16)],
        compiler_params=_cparams("arbitrary"),
        name="conv_ffn",
    )(x, x, x, mods, g, w_up, conv_w, conv_b, w_down, final_g)


XBC_COLS = 256


def _ssd_xbc_kernel(x_ref, xp_ref, xn_ref, mod_ref, g_ref, wx_ref, cw_ref, cb_ref, xbcp_ref, xbcs_ref, h_ref):
    i = pl.program_id(0)
    tm = TM_CONV
    modulate = functools.partial(_modulate, g=g_ref[...], shift=mod_ref[0, 0:1], scale=mod_ref[0, 1:2])
    _fill_with_halo(h_ref, x_ref, xp_ref, xn_ref, modulate, i, tm)

    def project(xbc_ref, is_prompt):
        h = h_ref[...]
        for c in range(SSD_XBC_W // XBC_COLS):
            sl = slice(c * XBC_COLS, (c + 1) * XBC_COLS)
            u = jnp.dot(h, wx_ref[:, sl], preferred_element_type=F32)
            xbc_ref[:, sl] = _silu(_conv3(u, cw_ref[:, sl], cb_ref[:, sl], tm, is_prompt))

    prompt = i < ROWS_P // tm
    pl.when(prompt)(lambda: project(xbcp_ref, True))
    pl.when(jnp.logical_not(prompt))(lambda: project(xbcs_ref, False))


def _ssd_zdt_kernel(x_ref, mod_ref, g_ref, wz_ref, wdt_ref, zp_ref, zs_ref, dtp_ref, dts_ref):
    i = pl.program_id(0)
    h = _modulate(x_ref[...], g_ref[...], mod_ref[0, 0:1], mod_ref[0, 1:2]).astype(BF16)

    def project(z_ref, dt_ref):
        z_ref[...] = jnp.dot(h, wz_ref[...], preferred_element_type=F32)
        dt_ref[...] = jnp.dot(h, wdt_ref[...], preferred_element_type=F32)

    prompt = i < ROWS_P // TM
    pl.when(prompt)(lambda: project(zp_ref, dtp_ref))
    pl.when(jnp.logical_not(prompt))(lambda: project(zs_ref, dts_ref))


def _ssd_in(x, mods, g, wz, wx, wdt, conv_w, conv_b):
    tm = TM_CONV
    prev, nxt = _halo_specs(tm, D_MODEL)
    mod = lambda t: pl.BlockSpec((1, 6, D_MODEL), lambda i: (_mod_index(i, t), 0, 0))
    group_shapes = lambda w: [jax.ShapeDtypeStruct((r, w), F32) for r in (ROWS_P, ROWS_S)]
    xbc = pl.pallas_call(
        _ssd_xbc_kernel,
        grid=(ROWS // tm,),
        in_specs=[pl.BlockSpec((tm, D_MODEL), lambda i: (i, 0)), prev, nxt, mod(tm), _const_spec((1, D_MODEL)),
                  _const_spec(wx.shape), _const_spec(conv_w.shape), _const_spec(conv_b.shape)],
        out_specs=list(_group_specs(tm, SSD_XBC_W)),
        out_shape=group_shapes(SSD_XBC_W),
        scratch_shapes=[pltpu.VMEM((tm + HALO, D_MODEL), BF16)],
        compiler_params=_cparams("arbitrary"),
        name="ssd_xbc",
    )(x, x, x, mods, g, wx, conv_w, conv_b)
    zp, zs, dtp, dts = pl.pallas_call(
        _ssd_zdt_kernel,
        grid=(ROWS // TM,),
        in_specs=[pl.BlockSpec((TM, D_MODEL), lambda i: (i, 0)), mod(TM), _const_spec((1, D_MODEL)),
                  _const_spec(wz.shape), _const_spec(wdt.shape)],
        out_specs=[*_group_specs(TM, SSD_DI), *_group_specs(TM, 2 * SSD_HEADS)],
        out_shape=group_shapes(SSD_DI) + group_shapes(2 * SSD_HEADS),
        compiler_params=_cparams("arbitrary"),
        name="ssd_zdt",
    )(x, mods, g, wz, wdt)
    return (zp, zs), tuple(xbc), (dtp, dts)


STREAMS = 2


def _split3(v):
    hi = v.astype(BF16)
    r1 = v - hi.astype(F32)
    mid = r1.astype(BF16)
    return hi, mid, (r1 - mid.astype(F32)).astype(BF16)


def _ssd_kernel(*refs, reverse, final, has_init, emit_state):
    refs = list(refs)
    xbc_ref, dt_ref, dtb_ref, alog_ref = refs[:4]
    del refs[:4]
    init_ref = refs.pop(0) if has_init else None
    wide_ref = refs.pop(0)
    if final:
        yin_ref, z_ref, dexp_ref, ng_ref = refs[:4]
        del refs[:4]
    y_ref = refs.pop(0)
    sfin_ref = refs.pop(0) if emit_state else None
    s_ref = refs.pop(0)
    yacc_ref = refs.pop(0) if final else None
    pos = pl.program_id(1)
    q = SSD_CHUNK
    hh = SSD_HEADS
    gw = SSD_DI // SSD_GROUPS

    @pl.when(pos == 0)
    def _():
        s_ref[...] = init_ref[...] if has_init else jnp.zeros_like(s_ref)

    d0 = hh if reverse else 0
    a = -jnp.exp(alog_ref[:, d0:d0 + hh])
    row = lax.broadcasted_iota(jnp.int32, (q, q), 0)
    col = lax.broadcasted_iota(jnp.int32, (q, q), 1)
    tri = (row <= col) if reverse else (row >= col)
    lane_lo = col < SSD_HEADDIM
    dn_t = (((1,), (1,)), ((), ()))

    def chunk_factors(k):
        dt_in = dt_ref[k, :, d0:d0 + hh] + dtb_ref[:, d0:d0 + hh]
        dt = jnp.maximum(dt_in, 0.0) + jnp.log1p(jnp.exp(-jnp.abs(dt_in)))
        acs = jnp.dot(tri.astype(F32), dt * a, precision=HIGHEST, preferred_element_type=F32)
        a_end = acs[0:1] if reverse else acs[q - 1:q]
        factors = jnp.concatenate([jnp.exp(acs), jnp.exp(a_end - acs) * dt,
                                   jnp.broadcast_to(jnp.exp(a_end), (SUBLANES, hh))], axis=0)
        wide = jnp.dot(jnp.concatenate(_split3(factors), axis=1), wide_ref[...], preferred_element_type=F32)
        acs2 = acs * LOG2_E
        mt = jnp.concatenate([dt, acs2, jnp.zeros((q, 2 * hh), F32)], axis=1).T
        return wide[0:q], wide[q:2 * q], wide[2 * q:2 * q + 1], acs2, mt

    def group(k, g, eacs_w, wst_w, dec_w, acs2, mt):
        gs = slice(g * gw, (g + 1) * gw)
        b_g = xbc_ref[k, :, SSD_DI + g * SSD_STATE:SSD_DI + (g + 1) * SSD_STATE]
        c_g = xbc_ref[k, :, SSD_DI + SSD_BC_W + g * SSD_STATE:SSD_DI + SSD_BC_W + (g + 1) * SSD_STATE]
        cgb = c_g.astype(BF16)
        cb = lax.dot_general(cgb, b_g.astype(BF16), dn_t, preferred_element_type=F32)
        s_g = s_ref[k, :, gs]
        y_off = jnp.dot(cgb, s_g.astype(BF16), preferred_element_type=F32) * eacs_w[:, gs]
        x_g = xbc_ref[k, :, gs]
        s_new = jnp.dot(b_g.T.astype(BF16), (x_g * wst_w[:, gs]).astype(BF16), preferred_element_type=F32)
        s_ref[k, :, gs] = s_g * dec_w[:, gs] + s_new
        for p in range(gw // LANES):
            ps = slice(p * LANES, (p + 1) * LANES)
            sl = slice(g * gw + p * LANES, g * gw + (p + 1) * LANES)
            xp = x_g[:, ps]
            rhs = jnp.concatenate([jnp.where(lane_lo, xp, 0.0).astype(BF16),
                                   jnp.where(lane_lo, 0.0, xp).astype(BF16)], axis=0)
            w_l = []
            for h in (2 * (g * 4 + p), 2 * (g * 4 + p) + 1):
                seg = jnp.broadcast_to(acs2[:, h:h + 1], (q, q)) - mt[hh + h:hh + h + 1]
                lm = jnp.exp2(jnp.where(tri, seg, -jnp.inf))
                w_l.append((cb * lm * mt[h:h + 1]).astype(BF16))
            yp = jnp.dot(jnp.concatenate(w_l, axis=1), rhs, preferred_element_type=F32) + y_off[:, ps]
            if final:
                yacc_ref[k, :, sl] = yp
            else:
                y_ref[k, :, sl] = yp

    per_stream = [chunk_factors(k) for k in range(STREAMS)]
    for g in range(SSD_GROUPS):
        for k in range(STREAMS):
            group(k, g, *per_stream[k])

    if final:
        for k in range(STREAMS):
            yt = (yacc_ref[k] + yin_ref[k] + xbc_ref[k, :, 0:SSD_DI] * dexp_ref[...]) * _silu(z_ref[k])
            ms = jnp.mean(yt * yt, axis=-1, keepdims=True)
            y_ref[k] = (yt * lax.rsqrt(ms + EPS) * ng_ref[...]).astype(y_ref.dtype)

    if emit_state:
        @pl.when(pos == pl.num_programs(1) - 1)
        def _():
            for k in range(STREAMS):
                sfin_ref[k] = s_ref[k].T


def _ssd_scan(xbc, dt, dt_bias, a_log, init, extra, *, reverse, nseq, emit_state):
    final = extra is not None
    has_init = init is not None
    assert nseq % STREAMS == 0 and (not has_init or nseq == STREAMS)
    nc = xbc.shape[0] // (nseq * SSD_CHUNK)
    npairs = nseq // STREAMS
    view = lambda a: a.reshape(npairs, STREAMS, nc, SSD_CHUNK, a.shape[-1])
    chunk = lambda width: pl.BlockSpec((None, STREAMS, None, SSD_CHUNK, width),
                                       lambda p, t: (p, 0, (nc - 1 - t) if reverse else t, 0, 0))
    widen = jnp.tile(jnp.repeat(jnp.eye(SSD_HEADS, dtype=BF16), SSD_HEADDIM, axis=1), (3, 1))
    in_specs = [chunk(SSD_XBC_W), chunk(2 * SSD_HEADS),
                _const_spec((1, 2 * SSD_HEADS)), _const_spec((1, 2 * SSD_HEADS))]
    args = [view(xbc), view(dt), dt_bias, a_log]
    if has_init:
        in_specs.append(_const_spec(init.shape))
        args.append(init)
    in_specs.append(_const_spec(widen.shape))
    args.append(widen)
    scratch = [pltpu.VMEM((STREAMS, SSD_STATE, SSD_DI), F32)]
    if final:
        y_other, z, d_exp, norm_g = extra
        in_specs += [chunk(SSD_DI), chunk(SSD_DI), _const_spec((1, SSD_DI)), _const_spec((1, SSD_DI))]
        args += [view(y_other), view(z), d_exp, norm_g]
        scratch.append(pltpu.VMEM((STREAMS, SSD_CHUNK, SSD_DI), F32))
    out_specs = [chunk(SSD_DI)]
    out_shape = [jax.ShapeDtypeStruct((npairs, STREAMS, nc, SSD_CHUNK, SSD_DI), BF16 if final else F32)]
    if emit_state:
        out_specs.append(pl.BlockSpec((None, STREAMS, SSD_DI, SSD_STATE), lambda p, t: (p, 0, 0, 0)))
        out_shape.append(jax.ShapeDtypeStruct((npairs, STREAMS, SSD_DI, SSD_STATE), F32))
    outs = pl.pallas_call(
        functools.partial(_ssd_kernel, reverse=reverse, final=final, has_init=has_init, emit_state=emit_state),
        grid=(npairs, nc),
        in_specs=in_specs,
        out_specs=out_specs,
        out_shape=out_shape,
        scratch_shapes=scratch,
        compiler_params=_cparams("arbitrary", "arbitrary"),
        name="ssd_scan_bwd" if reverse else "ssd_scan_fwd",
    )(*args)
    y = outs[0].reshape(xbc.shape[0], SSD_DI)
    return (y, outs[1].reshape(nseq, SSD_DI, SSD_STATE)) if emit_state else y


def kernel(x_prompt, x_sample, cache_k, cache_v, state_ssm_fwd, state_ssm_bwd, c, c_ctx, ada_w, ada_b, norm_mix_g, norm_ffn_g, att_w_in, att_lambda, att_subln_g, pool_w, pool_scale, att_w_out, ssd_w_in, ssd_conv_w, ssd_conv_b, ssd_dt_bias, ssd_a_log, ssd_d, ssd_norm_g, ssd_w_out, ffn_w_up, ffn_conv_w, ffn_conv_b, ffn_w_down, final_norm_g):
    x = (x_prompt.reshape(ROWS_P, D_MODEL), x_sample.reshape(ROWS_S, D_MODEL))
    cvec = jnp.concatenate([c_ctx[None], c, jnp.zeros((MOD_PAD - N_MOD, D_MODEL), F32)], axis=0)
    mods_all = _adaln(cvec, ada_w, ada_b).reshape(DEPTH, MOD_PAD, 6, D_MODEL)
    cos, sin = _rope_tables()
    ffn_up, ffn_down = ffn_w_up.astype(BF16), ffn_w_down.astype(BF16)
    outs = {}
    for l in range(DEPTH):
        i = l // 2
        mods = mods_all[l, :N_MOD]
        if l % 2 == 0:
            lam_init = 0.8 - 0.6 * math.exp(-0.3 * l)
            if not isinstance(x, tuple):
                x = (x[:ROWS_P], x[ROWS_P:])
            q, k_p, k_s, v_p, v_s, xpool = _att_in(*x, mods, norm_mix_g[l][None], att_w_in[i].astype(BF16), cos, sin)
            cached = (cache_k[:, i].reshape(DEC_BATCH * PAST_LEN, DA_QK_W),
                      cache_v[:, i].reshape(DEC_BATCH * PAST_LEN, DA_V_W))
            att = functools.partial(_attention, lam_vecs=att_lambda[i], subln_g=att_subln_g[i][None],
                                    lam_init=lam_init)
            o_p = att(q, k_p, v_p, None, batch=BATCH, lq=SEQ, tq=SEQ, q_row0=0, heads_per_step=DA_HEADS)
            o_s = att(q, k_s, v_s, cached, batch=DEC_BATCH, lq=DEC_SEQ, tq=TQ, q_row0=ROWS_P, heads_per_step=1)
            pw, ps = pool_w[i].astype(BF16), pool_scale[i][None]
            pool_p = _pool(xpool, pw, ps, seq=SEQ, row0=0, nseq=BATCH)
            pool_s = _pool(xpool, pw, ps, seq=DEC_SEQ, row0=ROWS_P, nseq=DEC_BATCH)
            w_out = att_w_out[i].astype(BF16)
            x = _proj_res(x, mods, [(o_p, o_s), (pool_p, pool_s)], [w_out[:DA_V_W], w_out[DA_V_W:]], gate_row=2)
            outs.setdefault("k", []).append(k_p.reshape(BATCH, SEQ, DA_HEADS, 2, DA_HD))
            outs.setdefault("v", []).append(v_p.reshape(BATCH, SEQ, DA_HEADS, 2 * DA_HD))
        else:
            w_in = ssd_w_in[i]
            z, xbc, dt = _ssd_in(x, mods, norm_mix_g[l][None], w_in[:, :SSD_DI].astype(BF16),
                                 w_in[:, SSD_DI:SSD_DI + SSD_XBC_W].astype(BF16),
                                 w_in[:, SSD_DI + SSD_XBC_W:].astype(BF16),
                                 ssd_conv_w[i], ssd_conv_b[i][None])
            to_scan_layout = lambda s: s.transpose(0, 3, 1, 2).reshape(DEC_BATCH, SSD_STATE, SSD_DI)
            dtb, alog = ssd_dt_bias[i].reshape(1, -1), ssd_a_log[i].reshape(1, -1)
            d_exp, norm_g = jnp.repeat(ssd_d[i], SSD_HEADDIM)[None], ssd_norm_g[i][None]
            scan_p = functools.partial(_ssd_scan, xbc[0], dt[0], dtb, alog, None, nseq=BATCH, emit_state=True)
            y_f, s_f = scan_p(None, reverse=False)
            y_p, s_b = scan_p((y_f, z[0], d_exp, norm_g), reverse=True)
            scan_s = functools.partial(_ssd_scan, xbc[1], dt[1], dtb, alog, nseq=DEC_BATCH, emit_state=False)
            y_f = scan_s(to_scan_layout(state_ssm_fwd[:, i]), None, reverse=False)
            y_s = scan_s(to_scan_layout(state_ssm_bwd[:, i]), (y_f, z[1], d_exp, norm_g), reverse=True)
            x = _proj_res(x, mods, [(y_p, y_s)], [ssd_w_out[i].astype(BF16)], gate_row=2)
            outs.setdefault("sf", []).append(s_f.reshape(BATCH, SSD_HEADS, SSD_HEADDIM, SSD_STATE))
            outs.setdefault("sb", []).append(s_b.reshape(BATCH, SSD_HEADS, SSD_HEADDIM, SSD_STATE))
        x = _conv_ffn(x, mods, norm_ffn_g[l][None], ffn_up, ffn_conv_w[l], ffn_conv_b[l][None],
                      ffn_down, final_norm_g[None], layer=l, last_layer=(l == DEPTH - 1))
    y_prompt = x[0].reshape(BATCH, SEQ, D_MODEL)
    y_sample = x[1].reshape(DEC_BATCH, DEC_SEQ, D_MODEL)
    return (y_prompt, y_sample, jnp.stack(outs["k"], axis=1), jnp.stack(outs["v"], axis=1),
            jnp.stack(outs["sf"], axis=1), jnp.stack(outs["sb"], axis=1))
```

```python
import functools
import math

import jax
import jax.numpy as jnp
from jax import lax
from jax.experimental import pallas as pl
from jax.experimental.pallas import tpu as pltpu

F32 = jnp.float32
BF16 = jnp.bfloat16
HIGHEST = lax.Precision.HIGHEST

D_MODEL = 1024
BATCH = 32
SEQ = 256
DEPTH = 2
DEC_BATCH = 2
DEC_SEQ = 2048
PAST_LEN = 512
GRID_W = 64
EPS = 1e-6
DA_HEADS = 4
DA_HD = 64
DA_QK_W = DA_HEADS * 2 * DA_HD
DA_V_W = DA_HEADS * 2 * DA_HD
POOL_W = D_MODEL - DA_V_W
POOL_WINDOWS = (2, 4, 8, 16)
POOL_GROUPS = 4
POOL_GC = POOL_W // POOL_GROUPS
ATT_IN_W = 2 * DA_QK_W + DA_V_W + POOL_W
LOG2_E = math.log2(math.e)
ROPE_THETA = 10000.0
ROPE_NF = DA_HD // 4
SSD_DI = 2 * D_MODEL
SSD_HEADDIM = 64
SSD_HEADS = SSD_DI // SSD_HEADDIM
SSD_GROUPS = 4
SSD_STATE = 128
SSD_CHUNK = 128
SSD_BC_W = SSD_GROUPS * SSD_STATE
SSD_XBC_W = SSD_DI + 2 * SSD_BC_W
D_FF = 2816

ROWS_P = BATCH * SEQ
ROWS_S = DEC_BATCH * DEC_SEQ
ROWS = ROWS_P + ROWS_S
N_MOD = 1 + DEC_BATCH
MOD_PAD = 8

LANES = 128
SUBLANES = 8
HALO = 2 * SUBLANES
VMEM_LIMIT = 56 * 1024 * 1024

TM = 512
TM_CONV = 512
TF = 256
TQ = 256


def _cparams(*sem):
    return pltpu.CompilerParams(dimension_semantics=sem, vmem_limit_bytes=VMEM_LIMIT)


def _mod_index(i, tm):
    n_p, per_seq = ROWS_P // tm, DEC_SEQ // tm
    return jnp.where(i < n_p, 0, 1 + (i - n_p) // per_seq)


def _silu(x):
    return x / (1.0 + jnp.exp2(x * -LOG2_E))


def _modulate(x, g, shift, scale):
    ms = jnp.mean(x * x, axis=-1, keepdims=True)
    y = x * lax.rsqrt(ms + EPS) * g
    return y * (1.0 + scale) + shift


def _const_spec(shape):
    nd = len(shape)
    return pl.BlockSpec(shape, lambda *_: (0,) * nd, pipeline_mode=pl.Buffered(1))


def _layer_spec(stacked_shape, layer):
    nd = len(stacked_shape) - 1
    return pl.BlockSpec((None, *stacked_shape[1:]), lambda *_: (layer,) + (0,) * nd, pipeline_mode=pl.Buffered(1))


def _group_specs(tm, width):
    n_p = ROWS_P // tm
    return (pl.BlockSpec((tm, width), lambda i: (jnp.minimum(i, n_p - 1), 0)),
            pl.BlockSpec((tm, width), lambda i: (jnp.maximum(i - n_p, 0), 0)))


def _group_rows(i, tm, p_ref, s_ref):
    return jnp.where(i < ROWS_P // tm, p_ref[...], s_ref[...])


def _halo_specs(tm, width):
    per = tm // SUBLANES
    last = ROWS // SUBLANES - 1
    prev = pl.BlockSpec((SUBLANES, width), lambda i: (jnp.maximum(i * per - 1, 0), 0))
    nxt = pl.BlockSpec((SUBLANES, width), lambda i: (jnp.minimum((i + 1) * per, last), 0))
    return prev, nxt


def _fill_with_halo(h_ref, x_ref, xp_ref, xn_ref, modulate, i, tm):
    m = jnp.where(i < ROWS_P // tm, SEQ - 1, DEC_SEQ - 1)
    starts = ((i * tm) & m) == 0
    ends = (((i + 1) * tm) & m) == 0
    h_ref[0:tm] = modulate(x_ref[...]).astype(BF16)
    halo = jnp.concatenate([jnp.where(ends, 0.0, modulate(xn_ref[...])),
                            jnp.where(starts, 0.0, modulate(xp_ref[...]))], axis=0)
    h_ref[tm:tm + HALO] = halo.astype(BF16)


def _conv3(u, cw, cb, tm, prompt):
    n = u.shape[0]
    up = pltpu.roll(u, 1, axis=0)[0:tm]
    un = pltpu.roll(u, n - 1, axis=0)[0:tm]
    if tm > SEQ and prompt is not False:
        s = SUBLANES
        sub = lax.broadcasted_iota(jnp.int32, (s, u.shape[1]), 0)
        at_edge = (lambda m: m) if prompt is True else (lambda m: prompt & m)
        ups, uns, lo = [], [], 0
        for edge in range(SEQ, tm, SEQ):
            ups += [up[lo:edge], jnp.where(at_edge(sub == 0), 0.0, up[edge:edge + s])]
            uns += [un[lo - s if lo else 0:edge - s], jnp.where(at_edge(sub == s - 1), 0.0, un[edge - s:edge])]
            lo = edge + s
        up = jnp.concatenate(ups + [up[lo:tm]], axis=0)
        un = jnp.concatenate(uns + [un[lo - s:tm]], axis=0)
    return cw[1:2] * u[0:tm] + cb + cw[0:1] * up + cw[2:3] * un


def _adaln_kernel(c_ref, w_ref, b_ref, o_ref):
    s = _silu(c_ref[...])
    o_ref[0] = jnp.dot(s, w_ref[0], precision=HIGHEST, preferred_element_type=F32) + b_ref[0]


def _adaln(cvec, ada_w, ada_b):
    tn = 1536
    n = 6 * D_MODEL
    return pl.pallas_call(
        _adaln_kernel,
        grid=(DEPTH, n // tn),
        in_specs=[_const_spec((MOD_PAD, D_MODEL)),
                  pl.BlockSpec((1, D_MODEL, tn), lambda l, j: (l, 0, j)),
                  pl.BlockSpec((1, 1, tn), lambda l, j: (l, 0, j))],
        out_specs=pl.BlockSpec((1, MOD_PAD, tn), lambda l, j: (l, 0, j)),
        out_shape=jax.ShapeDtypeStruct((DEPTH, MOD_PAD, n), F32),
        compiler_params=_cparams("parallel", "parallel"),
        name="adaln",
    )(cvec, ada_w, ada_b.reshape(DEPTH, 1, n))


def _rope(x, cos, sin_signed):
    lane = lax.broadcasted_iota(jnp.int32, cos.shape, 1)
    lower = (lane & 31) < ROPE_NF
    out = []
    for s in range(x.shape[1] // LANES):
        xs = x[:, s * LANES:(s + 1) * LANES]
        partner = jnp.where(lower, pltpu.roll(xs, LANES - ROPE_NF, axis=1), pltpu.roll(xs, ROPE_NF, axis=1))
        out.append(xs * cos + partner * sin_signed)
    return jnp.concatenate(out, axis=1)


def _att_in_kernel(xp_ref, xs_ref, mod_ref, g_ref, w_ref, cos_ref, sin_ref,
                   q_ref, kp_ref, ks_ref, vp_ref, vs_ref, p_ref):
    i = pl.program_id(0)
    x = _group_rows(i, TM, xp_ref, xs_ref)
    h = _modulate(x, g_ref[...], mod_ref[0, 0:1], mod_ref[0, 1:2]).astype(BF16)
    proj = jnp.dot(h, w_ref[...], preferred_element_type=F32)
    q = proj[:, :DA_QK_W]
    k = proj[:, DA_QK_W:2 * DA_QK_W]
    v = proj[:, 2 * DA_QK_W:2 * DA_QK_W + DA_V_W]
    p_ref[...] = proj[:, 2 * DA_QK_W + DA_V_W:]
    latent = i >= ROWS_P // TM

    @pl.when(latent)
    def _():
        q_ref[...] = _rope(q, cos_ref[...], sin_ref[...])
        ks_ref[...] = _rope(k, cos_ref[...], sin_ref[...])
        vs_ref[...] = v

    @pl.when(jnp.logical_not(latent))
    def _():
        q_ref[...] = q
        kp_ref[...] = k
        vp_ref[...] = v


def _rope_tables():
    t = jnp.arange(DEC_SEQ, dtype=F32)
    r, col = jnp.floor(t / GRID_W), t % GRID_W
    inv = ROPE_THETA ** (-jnp.arange(ROPE_NF, dtype=F32) / ROPE_NF)
    ar, ac = r[:, None] * inv, col[:, None] * inv
    cos = jnp.concatenate([jnp.cos(ar), jnp.cos(ar), jnp.cos(ac), jnp.cos(ac)], axis=1)
    sin = jnp.concatenate([-jnp.sin(ar), jnp.sin(ar), -jnp.sin(ac), jnp.sin(ac)], axis=1)
    return jnp.tile(cos, (1, 2)), jnp.tile(sin, (1, 2))


def _att_in(xp, xs, mods, g, w, cos, sin):
    n_p, per_seq = ROWS_P // TM, DEC_SEQ // TM
    tab = pl.BlockSpec((TM, LANES), lambda i: (jnp.maximum(i - n_p, 0) % per_seq, 0))
    out = pl.BlockSpec((TM, DA_QK_W), lambda i: (i, 0))
    out_p, out_s = _group_specs(TM, DA_QK_W)
    full, grp_p, grp_s = (jax.ShapeDtypeStruct((r, DA_QK_W), F32) for r in (ROWS, ROWS_P, ROWS_S))
    return pl.pallas_call(
        _att_in_kernel,
        grid=(ROWS // TM,),
        in_specs=[*_group_specs(TM, D_MODEL),
                  pl.BlockSpec((1, 6, D_MODEL), lambda i: (_mod_index(i, TM), 0, 0)),
                  _const_spec((1, D_MODEL)),
                  _const_spec((D_MODEL, ATT_IN_W)),
                  tab, tab],
        out_specs=[out, out_p, out_s, out_p, out_s, out],
        out_shape=[full, grp_p, grp_s, grp_p, grp_s, full],
        compiler_params=_cparams("arbitrary"),
        name="att_in",
    )(xp, xs, mods, g, w, cos, sin)


def _attn_kernel(lam_ref, g_ref, q_ref, k_ref, v_ref, *rest, lam_init):
    o_ref = rest[-1]
    lv = lam_ref[...]
    lam = (jnp.exp(jnp.sum(lv[0:1] * lv[1:2], keepdims=True))
           - jnp.exp(jnp.sum(lv[2:3] * lv[3:4], keepdims=True)) + lam_init)
    dn = (((1,), (1,)), ((), ()))
    hw = 2 * DA_HD
    lane = lax.broadcasted_iota(jnp.int32, (q_ref.shape[0], hw), 1)
    for h in range(q_ref.shape[1] // hw):
        sl = slice(h * hw, (h + 1) * hw)
        q = q_ref[:, sl] * (DA_HD ** -0.5 * LOG2_E)
        kb = k_ref[:, sl].astype(BF16)
        vb = v_ref[:, sl].astype(BF16)
        if len(rest) == 3:
            kb = jnp.concatenate([kb, rest[0][:, sl].astype(BF16)], axis=0)
            vb = jnp.concatenate([vb, rest[1][:, sl].astype(BF16)], axis=0)

        def unnormalised(qc):
            s = lax.dot_general(qc.astype(BF16), kb, dn, preferred_element_type=F32)
            e = jnp.exp2(s - jnp.max(s, axis=-1, keepdims=True))
            return (jnp.dot(e.astype(BF16), vb, preferred_element_type=F32),
                    jnp.sum(e, axis=-1, keepdims=True))

        o1, l1 = unnormalised(jnp.where(lane < DA_HD, q, 0.0))
        o2, l2 = unnormalised(jnp.where(lane < DA_HD, 0.0, q))
        o = o1 / l1 - o2 * (lam / l2)
        ms = jnp.mean(o * o, axis=-1, keepdims=True)
        o_ref[:, sl] = (o * lax.rsqrt(ms + EPS) * g_ref[...] * (1.0 - lam_init)).astype(o_ref.dtype)


def _attention(q, k, v, cached, lam_vecs, subln_g, *, batch, lq, tq, q_row0, heads_per_step, lam_init):
    nq = lq // tq
    q0 = q_row0 // tq
    bw = heads_per_step * 2 * DA_HD
    seq_block = lambda rows: pl.BlockSpec((rows, bw), lambda b, h, i: (b, h))
    cached = () if cached is None else cached
    return pl.pallas_call(
        functools.partial(_attn_kernel, lam_init=lam_init),
        grid=(batch, DA_HEADS // heads_per_step, nq),
        in_specs=[_const_spec((4, DA_HD)),
                  _const_spec((1, 2 * DA_HD)),
                  pl.BlockSpec((tq, bw), lambda b, h, i: (q0 + b * nq + i, h)),
                  seq_block(lq), seq_block(lq)] + [seq_block(PAST_LEN) for _ in cached],
        out_specs=pl.BlockSpec((tq, bw), lambda b, h, i: (b * nq + i, h)),
        out_shape=jax.ShapeDtypeStruct((batch * lq, DA_V_W), BF16),
        compiler_params=_cparams("parallel", "parallel", "parallel"),
        name="diff_attn",
    )(lam_vecs, subln_g, q, k, v, *cached)


assert max(POOL_WINDOWS) // 2 <= SUBLANES


def _pooled(ext, pos_ext, seq_len, w_ref, sc_ref, tm):
    n = ext.shape[0]
    masks = {}

    def shifted(a, k):
        if k not in masks:
            masks[k] = (pos_ext + k < seq_len) if k > 0 else (pos_ext + k >= 0)
        return jnp.where(masks[k], pltpu.roll(a, (-k) % n, axis=0), 0.0)

    pos = pos_ext[0:tm]
    out = []
    for gi, win in enumerate(POOL_WINDOWS):
        half = win // 2
        sl = slice(gi * POOL_GC, (gi + 1) * POOL_GC)
        xg = ext[:, sl]
        right, left, step = xg, xg, 1
        while step < half:
            right, left, step = right + shifted(right, step), left + shifted(left, -step), 2 * step
        acc = (right + shifted(left, -1))[0:tm]
        cnt = (jnp.minimum(pos + half, seq_len) - jnp.maximum(pos - half, 0)).astype(F32)
        pooled = acc / cnt - xg[0:tm]
        y = jnp.dot(pooled.astype(BF16), w_ref[gi], preferred_element_type=F32)
        out.append((y * sc_ref[:, sl]).astype(BF16))
    return jnp.concatenate(out, axis=1)


def _mix_out_kernel(mod_ref, xp_ref, xs_ref, op_ref, os_ref, pool_ref, poolp_ref, pooln_ref,
                    pw_ref, ps_ref, wa_ref, wb_ref, o_ref):
    i = pl.program_id(0)
    prompt = i < ROWS_P // TM
    seq_len = jnp.where(prompt, SEQ, DEC_SEQ)
    ext = jnp.concatenate([pool_ref[...], pooln_ref[...], poolp_ref[...]], axis=0)
    n = ext.shape[0]
    r = lax.broadcasted_iota(jnp.int32, (n, POOL_GC), 0)
    r = jnp.where(r < TM + SUBLANES, r, r - n)
    pooled = _pooled(ext, (i * TM + r) & (seq_len - 1), seq_len, pw_ref, ps_ref, TM)
    att = _group_rows(i, TM, op_ref, os_ref)
    acc = (jnp.dot(att, wa_ref[...], preferred_element_type=F32)
           + jnp.dot(pooled, wb_ref[...], preferred_element_type=F32))
    o_ref[...] = _group_rows(i, TM, xp_ref, xs_ref) + mod_ref[0, 2:3] * acc


def _mix_out(x, mods, att, xpool, pool_w, pool_scale, w_att, w_pool):
    prev, nxt = _halo_specs(TM, POOL_W)
    return pl.pallas_call(
        _mix_out_kernel,
        grid=(ROWS // TM,),
        in_specs=[pl.BlockSpec((1, 6, D_MODEL), lambda i: (_mod_index(i, TM), 0, 0)),
                  *_group_specs(TM, D_MODEL), *_group_specs(TM, DA_V_W),
                  pl.BlockSpec((TM, POOL_W), lambda i: (i, 0)), prev, nxt,
                  _const_spec(pool_w.shape), _const_spec(pool_scale.shape),
                  _const_spec(w_att.shape), _const_spec(w_pool.shape)],
        out_specs=pl.BlockSpec((TM, D_MODEL), lambda i: (i, 0)),
        out_shape=jax.ShapeDtypeStruct((ROWS, D_MODEL), F32),
        compiler_params=_cparams("parallel"),
        name="mix_out",
    )(mods, *x, *att, xpool, xpool, xpool, pool_w, pool_scale, w_att, w_pool)


def _proj_res_kernel(*refs, gate_row, grouped):
    i = pl.program_id(0)
    mod_ref, o_ref = refs[0], refs[-1]
    pos, rows = 1, []
    for is_pair in grouped:
        rows.append(_group_rows(i, TM, refs[pos], refs[pos + 1]) if is_pair else refs[pos][...])
        pos += 2 if is_pair else 1
    acc = None
    for a, w_ref in zip(rows[1:], refs[pos:-1]):
        d = jnp.dot(a.astype(BF16), w_ref[...], preferred_element_type=F32)
        acc = d if acc is None else acc + d
    o_ref[...] = rows[0] + mod_ref[0, gate_row:gate_row + 1] * acc


def _proj_res(x, mods, acts, ws, *, gate_row):
    operands, specs, grouped = [], [], []
    for a in (x, *acts):
        is_pair = isinstance(a, tuple)
        grouped.append(is_pair)
        if is_pair:
            operands += list(a)
            specs += list(_group_specs(TM, a[0].shape[1]))
        else:
            operands.append(a)
            specs.append(pl.BlockSpec((TM, a.shape[1]), lambda i: (i, 0)))
    return pl.pallas_call(
        functools.partial(_proj_res_kernel, gate_row=gate_row, grouped=tuple(grouped)),
        grid=(ROWS // TM,),
        in_specs=[pl.BlockSpec((1, 6, D_MODEL), lambda i: (_mod_index(i, TM), 0, 0))]
                 + specs + [_const_spec(w.shape) for w in ws],
        out_specs=pl.BlockSpec((TM, D_MODEL), lambda i: (i, 0)),
        out_shape=jax.ShapeDtypeStruct((ROWS, D_MODEL), F32),
        compiler_params=_cparams("parallel"),
        name="proj_res",
    )(mods, *operands, *ws)


def _ffn_kernel(x_ref, xp_ref, xn_ref, mod_ref, g_ref, wup_ref, cw_ref, cb_ref, wd_ref, fg_ref,
                *rest, last_layer):
    h_ref, act_ref = rest[-2:]
    i = pl.program_id(0)
    tm = TM_CONV
    modulate = functools.partial(_modulate, g=g_ref[...], shift=mod_ref[0, 3:4], scale=mod_ref[0, 4:5])
    _fill_with_halo(h_ref, x_ref, xp_ref, xn_ref, modulate, i, tm)
    h = h_ref[...]
    prompt = i < ROWS_P // tm
    for c in range(D_FF // TF):
        gs, vs = slice(c * TF, (c + 1) * TF), slice(D_FF + c * TF, D_FF + (c + 1) * TF)
        ug = jnp.dot(h, wup_ref[:, gs], preferred_element_type=F32)
        uv = jnp.dot(h, wup_ref[:, vs], preferred_element_type=F32)
        cg = _conv3(ug, cw_ref[:, gs], cb_ref[:, gs], tm, prompt)
        cv = _conv3(uv, cw_ref[:, vs], cb_ref[:, vs], tm, prompt)
        act_ref[:, gs] = (_silu(cg) * cv).astype(BF16)
    y = x_ref[...] + mod_ref[0, 5:6] * jnp.dot(act_ref[...], wd_ref[...], preferred_element_type=F32)
    if not last_layer:
        rest[0][...] = y
        return
    ms = jnp.mean(y * y, axis=-1, keepdims=True)
    y = y * lax.rsqrt(ms + EPS) * fg_ref[...]
    op_ref, os_ref = rest[:2]

    @pl.when(prompt)
    def _():
        op_ref[...] = y

    @pl.when(jnp.logical_not(prompt))
    def _():
        os_ref[...] = y


def _conv_ffn(x, mods, g, w_up, conv_w, conv_b, w_down, final_g, *, layer, last_layer):
    tm = TM_CONV
    prev, nxt = _halo_specs(tm, D_MODEL)
    if last_layer:
        out_specs = list(_group_specs(tm, D_MODEL))
        out_shape = [jax.ShapeDtypeStruct((ROWS_P, D_MODEL), F32), jax.ShapeDtypeStruct((ROWS_S, D_MODEL), F32)]
    else:
        out_specs = pl.BlockSpec((tm, D_MODEL), lambda i: (i, 0))
        out_shape = jax.ShapeDtypeStruct((ROWS, D_MODEL), F32)
    return pl.pallas_call(
        functools.partial(_ffn_kernel, last_layer=last_layer),
        grid=(ROWS // tm,),
        in_specs=[pl.BlockSpec((tm, D_MODEL), lambda i: (i, 0)), prev, nxt,
                  pl.BlockSpec((1, 6, D_MODEL), lambda i: (_mod_index(i, tm), 0, 0)),
                  _const_spec((1, D_MODEL)),
                  _layer_spec(w_up.shape, layer), _const_spec(conv_w.shape), _const_spec(conv_b.shape),
                  _layer_spec(w_down.shape, layer), _const_spec((1, D_MODEL))],
        out_specs=out_specs,
        out_shape=out_shape,
        scratch_shapes=[pltpu.VMEM((tm + HALO, D_MODEL), BF16), pltpu.VMEM((tm, D_FF), BF16)],
        compiler_params=_cparams("arbitrary"),
        name="conv_ffn",
    )(x, x, x, mods, g, w_up, conv_w, conv_b, w_down, final_g)


XBC_COLS = 256


def _ssd_xbc_kernel(x_ref, xp_ref, xn_ref, mod_ref, g_ref, wx_ref, cw_ref, cb_ref, xbcp_ref, xbcs_ref, h_ref):
    i = pl.program_id(0)
    tm = TM_CONV
    modulate = functools.partial(_modulate, g=g_ref[...], shift=mod_ref[0, 0:1], scale=mod_ref[0, 1:2])
    _fill_with_halo(h_ref, x_ref, xp_ref, xn_ref, modulate, i, tm)

    def project(xbc_ref, is_prompt):
        h = h_ref[...]
        for c in range(SSD_XBC_W // XBC_COLS):
            sl = slice(c * XBC_COLS, (c + 1) * XBC_COLS)
            u = jnp.dot(h, wx_ref[:, sl], preferred_element_type=F32)
            xbc_ref[:, sl] = _silu(_conv3(u, cw_ref[:, sl], cb_ref[:, sl], tm, is_prompt))

    prompt = i < ROWS_P // tm
    pl.when(prompt)(lambda: project(xbcp_ref, True))
    pl.when(jnp.logical_not(prompt))(lambda: project(xbcs_ref, False))


def _ssd_zdt_kernel(x_ref, mod_ref, g_ref, wz_ref, wdt_ref, zp_ref, zs_ref, dtp_ref, dts_ref):
    i = pl.program_id(0)
    h = _modulate(x_ref[...], g_ref[...], mod_ref[0, 0:1], mod_ref[0, 1:2]).astype(BF16)

    def project(z_ref, dt_ref):
        z_ref[...] = jnp.dot(h, wz_ref[...], preferred_element_type=F32)
        dt_ref[...] = jnp.dot(h, wdt_ref[...], preferred_element_type=F32)

    prompt = i < ROWS_P // TM
    pl.when(prompt)(lambda: project(zp_ref, dtp_ref))
    pl.when(jnp.logical_not(prompt))(lambda: project(zs_ref, dts_ref))


def _ssd_in(x, mods, g, wz, wx, wdt, conv_w, conv_b):
    tm = TM_CONV
    prev, nxt = _halo_specs(tm, D_MODEL)
    mod = lambda t: pl.BlockSpec((1, 6, D_MODEL), lambda i: (_mod_index(i, t), 0, 0))
    group_shapes = lambda w: [jax.ShapeDtypeStruct((r, w), F32) for r in (ROWS_P, ROWS_S)]
    xbc = pl.pallas_call(
        _ssd_xbc_kernel,
        grid=(ROWS // tm,),
        in_specs=[pl.BlockSpec((tm, D_MODEL), lambda i: (i, 0)), prev, nxt, mod(tm), _const_spec((1, D_MODEL)),
                  _const_spec(wx.shape), _const_spec(conv_w.shape), _const_spec(conv_b.shape)],
        out_specs=list(_group_specs(tm, SSD_XBC_W)),
        out_shape=group_shapes(SSD_XBC_W),
        scratch_shapes=[pltpu.VMEM((tm + HALO, D_MODEL), BF16)],
        compiler_params=_cparams("arbitrary"),
        name="ssd_xbc",
    )(x, x, x, mods, g, wx, conv_w, conv_b)
    zp, zs, dtp, dts = pl.pallas_call(
        _ssd_zdt_kernel,
        grid=(ROWS // TM,),
        in_specs=[pl.BlockSpec((TM, D_MODEL), lambda i: (i, 0)), mod(TM), _const_spec((1, D_MODEL)),
                  _const_spec(wz.shape), _const_spec(wdt.shape)],
        out_specs=[*_group_specs(TM, SSD_DI), *_group_specs(TM, 2 * SSD_HEADS)],
        out_shape=group_shapes(SSD_DI) + group_shapes(2 * SSD_HEADS),
        compiler_params=_cparams("arbitrary"),
        name="ssd_zdt",
    )(x, mods, g, wz, wdt)
    return (zp, zs), tuple(xbc), (dtp, dts)


STREAMS = 2


def _split3(v):
    hi = v.astype(BF16)
    r1 = v - hi.astype(F32)
    mid = r1.astype(BF16)
    return hi, mid, (r1 - mid.astype(F32)).astype(BF16)


def _ssd_kernel(*refs, reverse, final, has_init, emit_state):
    refs = list(refs)
    xbc_ref, dt_ref, dtb_ref, alog_ref = refs[:4]
    del refs[:4]
    init_ref = refs.pop(0) if has_init else None
    wide_ref = refs.pop(0)
    if final:
        yin_ref, z_ref, dexp_ref, ng_ref = refs[:4]
        del refs[:4]
    y_ref = refs.pop(0)
    sfin_ref = refs.pop(0) if emit_state else None
    s_ref = refs.pop(0)
    yacc_ref = refs.pop(0) if final else None
    pos = pl.program_id(1)
    q = SSD_CHUNK
    hh = SSD_HEADS
    gw = SSD_DI // SSD_GROUPS

    @pl.when(pos == 0)
    def _():
        s_ref[...] = init_ref[...] if has_init else jnp.zeros_like(s_ref)

    d0 = hh if reverse else 0
    a = -jnp.exp(alog_ref[:, d0:d0 + hh])
    row = lax.broadcasted_iota(jnp.int32, (q, q), 0)
    col = lax.broadcasted_iota(jnp.int32, (q, q), 1)
    tri = (row <= col) if reverse else (row >= col)
    lane_lo = col < SSD_HEADDIM
    dn_t = (((1,), (1,)), ((), ()))

    def chunk_factors(k):
        dt_in = dt_ref[k, :, d0:d0 + hh] + dtb_ref[:, d0:d0 + hh]
        dt = jnp.maximum(dt_in, 0.0) + jnp.log1p(jnp.exp(-jnp.abs(dt_in)))
        acs = jnp.dot(tri.astype(F32), dt * a, precision=HIGHEST, preferred_element_type=F32)
        a_end = acs[0:1] if reverse else acs[q - 1:q]
        factors = jnp.concatenate([jnp.exp(acs), jnp.exp(a_end - acs) * dt,
                                   jnp.broadcast_to(jnp.exp(a_end), (SUBLANES, hh))], axis=0)
        wide = jnp.dot(jnp.concatenate(_split3(factors), axis=1), wide_ref[...], preferred_element_type=F32)
        acs2 = acs * LOG2_E
        mt = jnp.concatenate([dt, acs2, jnp.zeros((q, 2 * hh), F32)], axis=1).T
        return wide[0:q], wide[q:2 * q], wide[2 * q:2 * q + 1], acs2, mt

    def group(k, g, eacs_w, wst_w, dec_w, acs2, mt):
        gs = slice(g * gw, (g + 1) * gw)
        b_g = xbc_ref[k, :, SSD_DI + g * SSD_STATE:SSD_DI + (g + 1) * SSD_STATE]
        c_g = xbc_ref[k, :, SSD_DI + SSD_BC_W + g * SSD_STATE:SSD_DI + SSD_BC_W + (g + 1) * SSD_STATE]
        cgb = c_g.astype(BF16)
        cb = lax.dot_general(cgb, b_g.astype(BF16), dn_t, preferred_element_type=F32)
        s_g = s_ref[k, :, gs]
        y_off = jnp.dot(cgb, s_g.astype(BF16), preferred_element_type=F32) * eacs_w[:, gs]
        x_g = xbc_ref[k, :, gs]
        s_new = jnp.dot(b_g.T.astype(BF16), (x_g * wst_w[:, gs]).astype(BF16), preferred_element_type=F32)
        s_ref[k, :, gs] = s_g * dec_w[:, gs] + s_new
        for p in range(gw // LANES):
            ps = slice(p * LANES, (p + 1) * LANES)
            sl = slice(g * gw + p * LANES, g * gw + (p + 1) * LANES)
            xp = x_g[:, ps]
            rhs = jnp.concatenate([jnp.where(lane_lo, xp, 0.0).astype(BF16),
                                   jnp.where(lane_lo, 0.0, xp).astype(BF16)], axis=0)
            w_l = []
            for h in (2 * (g * 4 + p), 2 * (g * 4 + p) + 1):
                seg = jnp.broadcast_to(acs2[:, h:h + 1], (q, q)) - mt[hh + h:hh + h + 1]
                lm = jnp.exp2(jnp.where(tri, seg, -jnp.inf))
                w_l.append((cb * lm * mt[h:h + 1]).astype(BF16))
            yp = jnp.dot(jnp.concatenate(w_l, axis=1), rhs, preferred_element_type=F32) + y_off[:, ps]
            if final:
                yacc_ref[k, :, sl] = yp
            else:
                y_ref[k, :, sl] = yp

    per_stream = [chunk_factors(k) for k in range(STREAMS)]
    for g in range(SSD_GROUPS):
        for k in range(STREAMS):
            group(k, g, *per_stream[k])

    if final:
        for k in range(STREAMS):
            yt = (yacc_ref[k] + yin_ref[k] + xbc_ref[k, :, 0:SSD_DI] * dexp_ref[...]) * _silu(z_ref[k])
            ms = jnp.mean(yt * yt, axis=-1, keepdims=True)
            y_ref[k] = (yt * lax.rsqrt(ms + EPS) * ng_ref[...]).astype(y_ref.dtype)

    if emit_state:
        @pl.when(pos == pl.num_programs(1) - 1)
        def _():
            for k in range(STREAMS):
                sfin_ref[k] = s_ref[k].T


def _ssd_scan(xbc, dt, dt_bias, a_log, init, extra, *, reverse, nseq, emit_state):
    final = extra is not None
    has_init = init is not None
    assert nseq % STREAMS == 0 and (not has_init or nseq == STREAMS)
    nc = xbc.shape[0] // (nseq * SSD_CHUNK)
    npairs = nseq // STREAMS
    view = lambda a: a.reshape(npairs, STREAMS, nc, SSD_CHUNK, a.shape[-1])
    chunk = lambda width: pl.BlockSpec((None, STREAMS, None, SSD_CHUNK, width),
                                       lambda p, t: (p, 0, (nc - 1 - t) if reverse else t, 0, 0))
    widen = jnp.tile(jnp.repeat(jnp.eye(SSD_HEADS, dtype=BF16), SSD_HEADDIM, axis=1), (3, 1))
    in_specs = [chunk(SSD_XBC_W), chunk(2 * SSD_HEADS),
                _const_spec((1, 2 * SSD_HEADS)), _const_spec((1, 2 * SSD_HEADS))]
    args = [view(xbc), view(dt), dt_bias, a_log]
    if has_init:
        in_specs.append(_const_spec(init.shape))
        args.append(init)
    in_specs.append(_const_spec(widen.shape))
    args.append(widen)
    scratch = [pltpu.VMEM((STREAMS, SSD_STATE, SSD_DI), F32)]
    if final:
        y_other, z, d_exp, norm_g = extra
        in_specs += [chunk(SSD_DI), chunk(SSD_DI), _const_spec((1, SSD_DI)), _const_spec((1, SSD_DI))]
        args += [view(y_other), view(z), d_exp, norm_g]
        scratch.append(pltpu.VMEM((STREAMS, SSD_CHUNK, SSD_DI), F32))
    out_specs = [chunk(SSD_DI)]
    out_shape = [jax.ShapeDtypeStruct((npairs, STREAMS, nc, SSD_CHUNK, SSD_DI), BF16 if final else F32)]
    if emit_state:
        out_specs.append(pl.BlockSpec((None, STREAMS, SSD_DI, SSD_STATE), lambda p, t: (p, 0, 0, 0)))
        out_shape.append(jax.ShapeDtypeStruct((npairs, STREAMS, SSD_DI, SSD_STATE), F32))
    outs = pl.pallas_call(
        functools.partial(_ssd_kernel, reverse=reverse, final=final, has_init=has_init, emit_state=emit_state),
        grid=(npairs, nc),
        in_specs=in_specs,
        out_specs=out_specs,
        out_shape=out_shape,
        scratch_shapes=scratch,
        compiler_params=_cparams("arbitrary", "arbitrary"),
        name="ssd_scan_bwd" if reverse else "ssd_scan_fwd",
    )(*args)
    y = outs[0].reshape(xbc.shape[0], SSD_DI)
    return (y, outs[1].reshape(nseq, SSD_DI, SSD_STATE)) if emit_state else y


def kernel(x_prompt, x_sample, cache_k, cache_v, state_ssm_fwd, state_ssm_bwd, c, c_ctx, ada_w, ada_b, norm_mix_g, norm_ffn_g, att_w_in, att_lambda, att_subln_g, pool_w, pool_scale, att_w_out, ssd_w_in, ssd_conv_w, ssd_conv_b, ssd_dt_bias, ssd_a_log, ssd_d, ssd_norm_g, ssd_w_out, ffn_w_up, ffn_conv_w, ffn_conv_b, ffn_w_down, final_norm_g):
    x = (x_prompt.reshape(ROWS_P, D_MODEL), x_sample.reshape(ROWS_S, D_MODEL))
    cvec = jnp.concatenate([c_ctx[None], c, jnp.zeros((MOD_PAD - N_MOD, D_MODEL), F32)], axis=0)
    mods_all = _adaln(cvec, ada_w, ada_b).reshape(DEPTH, MOD_PAD, 6, D_MODEL)
    cos, sin = _rope_tables()
    ffn_up, ffn_down = ffn_w_up.astype(BF16), ffn_w_down.astype(BF16)
    outs = {}
    for l in range(DEPTH):
        i = l // 2
        mods = mods_all[l, :N_MOD]
        if l % 2 == 0:
            lam_init = 0.8 - 0.6 * math.exp(-0.3 * l)
            if not isinstance(x, tuple):
                x = (x[:ROWS_P], x[ROWS_P:])
            q, k_p, k_s, v_p, v_s, xpool = _att_in(*x, mods, norm_mix_g[l][None], att_w_in[i].astype(BF16), cos, sin)
            cached = (cache_k[:, i].reshape(DEC_BATCH * PAST_LEN, DA_QK_W),
                      cache_v[:, i].reshape(DEC_BATCH * PAST_LEN, DA_V_W))
            att = functools.partial(_attention, lam_vecs=att_lambda[i], subln_g=att_subln_g[i][None],
                                    lam_init=lam_init)
            o_p = att(q, k_p, v_p, None, batch=BATCH, lq=SEQ, tq=SEQ, q_row0=0, heads_per_step=DA_HEADS)
            o_s = att(q, k_s, v_s, cached, batch=DEC_BATCH, lq=DEC_SEQ, tq=TQ, q_row0=ROWS_P, heads_per_step=1)
            w_out = att_w_out[i].astype(BF16)
            x = _mix_out(x, mods, (o_p, o_s), xpool, pool_w[i].astype(BF16), pool_scale[i][None],
                         w_out[:DA_V_W], w_out[DA_V_W:])
            outs.setdefault("k", []).append(k_p.reshape(BATCH, SEQ, DA_HEADS, 2, DA_HD))
            outs.setdefault("v", []).append(v_p.reshape(BATCH, SEQ, DA_HEADS, 2 * DA_HD))
        else:
            w_in = ssd_w_in[i]
            z, xbc, dt = _ssd_in(x, mods, norm_mix_g[l][None], w_in[:, :SSD_DI].astype(BF16),
                                 w_in[:, SSD_DI:SSD_DI + SSD_XBC_W].astype(BF16),
                                 w_in[:, SSD_DI + SSD_XBC_W:].astype(BF16),
                                 ssd_conv_w[i], ssd_conv_b[i][None])
            to_scan_layout = lambda s: s.transpose(0, 3, 1, 2).reshape(DEC_BATCH, SSD_STATE, SSD_DI)
            dtb, alog = ssd_dt_bias[i].reshape(1, -1), ssd_a_log[i].reshape(1, -1)
            d_exp, norm_g = jnp.repeat(ssd_d[i], SSD_HEADDIM)[None], ssd_norm_g[i][None]
            scan_p = functools.partial(_ssd_scan, xbc[0], dt[0], dtb, alog, None, nseq=BATCH, emit_state=True)
            y_f, s_f = scan_p(None, reverse=False)
            y_p, s_b = scan_p((y_f, z[0], d_exp, norm_g), reverse=True)
            scan_s = functools.partial(_ssd_scan, xbc[1], dt[1], dtb, alog, nseq=DEC_BATCH, emit_state=False)
            y_f = scan_s(to_scan_layout(state_ssm_fwd[:, i]), None, reverse=False)
            y_s = scan_s(to_scan_layout(state_ssm_bwd[:, i]), (y_f, z[1], d_exp, norm_g), reverse=True)
            x = _proj_res(x, mods, [(y_p, y_s)], [ssd_w_out[i].astype(BF16)], gate_row=2)
            outs.setdefault("sf", []).append(s_f.reshape(BATCH, SSD_HEADS, SSD_HEADDIM, SSD_STATE))
            outs.setdefault("sb", []).append(s_b.reshape(BATCH, SSD_HEADS, SSD_HEADDIM, SSD_STATE))
        x = _conv_ffn(x, mods, norm_ffn_g[l][None], ffn_up, ffn_conv_w[l], ffn_conv_b[l][None],
                      ffn_down, final_norm_g[None], layer=l, last_layer=(l == DEPTH - 1))
    y_prompt = x[0].reshape(BATCH, SEQ, D_MODEL)
    y_sample = x[1].reshape(DEC_BATCH, DEC_SEQ, D_MODEL)
    return (y_prompt, y_sample, jnp.stack(outs["k"], axis=1), jnp.stack(outs["v"], axis=1),
            jnp.stack(outs["sf"], axis=1), jnp.stack(outs["sb"], axis=1))
```

```python
import functools
import math

import jax
import jax.numpy as jnp
from jax import lax
from jax.experimental import pallas as pl
from jax.experimental.pallas import tpu as pltpu

F32 = jnp.float32
BF16 = jnp.bfloat16
HIGHEST = lax.Precision.HIGHEST

D_MODEL = 1024
BATCH = 32
SEQ = 256
DEPTH = 2
DEC_BATCH = 2
DEC_SEQ = 2048
PAST_LEN = 512
GRID_W = 64
EPS = 1e-6
DA_HEADS = 4
DA_HD = 64
DA_QK_W = DA_HEADS * 2 * DA_HD
DA_V_W = DA_HEADS * 2 * DA_HD
POOL_W = D_MODEL - DA_V_W
POOL_WINDOWS = (2, 4, 8, 16)
POOL_GROUPS = 4
POOL_GC = POOL_W // POOL_GROUPS
ATT_IN_W = 2 * DA_QK_W + DA_V_W + POOL_W
LOG2_E = math.log2(math.e)
ROPE_THETA = 10000.0
ROPE_NF = DA_HD // 4
SSD_DI = 2 * D_MODEL
SSD_HEADDIM = 64
SSD_HEADS = SSD_DI // SSD_HEADDIM
SSD_GROUPS = 4
SSD_STATE = 128
SSD_CHUNK = 128
SSD_BC_W = SSD_GROUPS * SSD_STATE
SSD_XBC_W = SSD_DI + 2 * SSD_BC_W
D_FF = 2816

ROWS_P = BATCH * SEQ
ROWS_S = DEC_BATCH * DEC_SEQ
ROWS = ROWS_P + ROWS_S
N_MOD = 1 + DEC_BATCH
MOD_PAD = 8

LANES = 128
SUBLANES = 8
HALO = 2 * SUBLANES
VMEM_LIMIT = 56 * 1024 * 1024

TM = 512
TM_CONV = 512
TF = 256
TQ = 256


def _cparams(*sem):
    return pltpu.CompilerParams(dimension_semantics=sem, vmem_limit_bytes=VMEM_LIMIT)


def _mod_index(i, tm):
    n_p, per_seq = ROWS_P // tm, DEC_SEQ // tm
    return jnp.where(i < n_p, 0, 1 + (i - n_p) // per_seq)


def _silu(x):
    return x / (1.0 + jnp.exp2(x * -LOG2_E))


def _modulate(x, g, shift, scale):
    ms = jnp.mean(x * x, axis=-1, keepdims=True)
    y = x * lax.rsqrt(ms + EPS) * g
    return y * (1.0 + scale) + shift


def _const_spec(shape):
    nd = len(shape)
    return pl.BlockSpec(shape, lambda *_: (0,) * nd, pipeline_mode=pl.Buffered(1))


def _layer_spec(stacked_shape, layer):
    nd = len(stacked_shape) - 1
    return pl.BlockSpec((None, *stacked_shape[1:]), lambda *_: (layer,) + (0,) * nd, pipeline_mode=pl.Buffered(1))


def _group_specs(tm, width):
    n_p = ROWS_P // tm
    return (pl.BlockSpec((tm, width), lambda i: (jnp.minimum(i, n_p - 1), 0)),
            pl.BlockSpec((tm, width), lambda i: (jnp.maximum(i - n_p, 0), 0)))


def _group_rows(i, tm, p_ref, s_ref):
    return jnp.where(i < ROWS_P // tm, p_ref[...], s_ref[...])


def _halo_specs(tm, width):
    per = tm // SUBLANES
    last = ROWS // SUBLANES - 1
    prev = pl.BlockSpec((SUBLANES, width), lambda i: (jnp.maximum(i * per - 1, 0), 0))
    nxt = pl.BlockSpec((SUBLANES, width), lambda i: (jnp.minimum((i + 1) * per, last), 0))
    return prev, nxt


def _fill_with_halo(h_ref, x_ref, xp_ref, xn_ref, modulate, i, tm):
    m = jnp.where(i < ROWS_P // tm, SEQ - 1, DEC_SEQ - 1)
    starts = ((i * tm) & m) == 0
    ends = (((i + 1) * tm) & m) == 0
    h_ref[0:tm] = modulate(x_ref[...]).astype(BF16)
    halo = jnp.concatenate([jnp.where(ends, 0.0, modulate(xn_ref[...])),
                            jnp.where(starts, 0.0, modulate(xp_ref[...]))], axis=0)
    h_ref[tm:tm + HALO] = halo.astype(BF16)


def _conv3(u, cw, cb, tm, prompt):
    n = u.shape[0]
    up = pltpu.roll(u, 1, axis=0)[0:tm]
    un = pltpu.roll(u, n - 1, axis=0)[0:tm]
    if tm > SEQ and prompt is not False:
        s = SUBLANES
        sub = lax.broadcasted_iota(jnp.int32, (s, u.shape[1]), 0)
        at_edge = (lambda m: m) if prompt is True else (lambda m: prompt & m)
        ups, uns, lo = [], [], 0
        for edge in range(SEQ, tm, SEQ):
            ups += [up[lo:edge], jnp.where(at_edge(sub == 0), 0.0, up[edge:edge + s])]
            uns += [un[lo - s if lo else 0:edge - s], jnp.where(at_edge(sub == s - 1), 0.0, un[edge - s:edge])]
            lo = edge + s
        up = jnp.concatenate(ups + [up[lo:tm]], axis=0)
        un = jnp.concatenate(uns + [un[lo - s:tm]], axis=0)
    return cw[1:2] * u[0:tm] + cb + cw[0:1] * up + cw[2:3] * un


def _adaln_kernel(c_ref, w_ref, b_ref, o_ref):
    s = _silu(c_ref[...])
    o_ref[0] = jnp.dot(s, w_ref[0], precision=HIGHEST, preferred_element_type=F32) + b_ref[0]


def _adaln(cvec, ada_w, ada_b):
    tn = 1536
    n = 6 * D_MODEL
    return pl.pallas_call(
        _adaln_kernel,
        grid=(DEPTH, n // tn),
        in_specs=[_const_spec((MOD_PAD, D_MODEL)),
                  pl.BlockSpec((1, D_MODEL, tn), lambda l, j: (l, 0, j)),
                  pl.BlockSpec((1, 1, tn), lambda l, j: (l, 0, j))],
        out_specs=pl.BlockSpec((1, MOD_PAD, tn), lambda l, j: (l, 0, j)),
        out_shape=jax.ShapeDtypeStruct((DEPTH, MOD_PAD, n), F32),
        compiler_params=_cparams("parallel", "parallel"),
        name="adaln",
    )(cvec, ada_w, ada_b.reshape(DEPTH, 1, n))


def _rope(x, cos, sin_signed):
    lane = lax.broadcasted_iota(jnp.int32, cos.shape, 1)
    lower = (lane & 31) < ROPE_NF
    out = []
    for s in range(x.shape[1] // LANES):
        xs = x[:, s * LANES:(s + 1) * LANES]
        partner = jnp.where(lower, pltpu.roll(xs, LANES - ROPE_NF, axis=1), pltpu.roll(xs, ROPE_NF, axis=1))
        out.append(xs * cos + partner * sin_signed)
    return jnp.concatenate(out, axis=1)


def _att_in_kernel(xp_ref, xs_ref, mod_ref, g_ref, w_ref, cos_ref, sin_ref,
                   q_ref, kpt_ref, ks_ref, vp_ref, vs_ref, p_ref):
    i = pl.program_id(0)
    x = _group_rows(i, TM, xp_ref, xs_ref)
    h = _modulate(x, g_ref[...], mod_ref[0, 0:1], mod_ref[0, 1:2]).astype(BF16)
    proj = jnp.dot(h, w_ref[...], preferred_element_type=F32)
    q = proj[:, :DA_QK_W]
    k = proj[:, DA_QK_W:2 * DA_QK_W]
    v = proj[:, 2 * DA_QK_W:2 * DA_QK_W + DA_V_W]
    p_ref[...] = proj[:, 2 * DA_QK_W + DA_V_W:]
    latent = i >= ROWS_P // TM

    @pl.when(latent)
    def _():
        q_ref[...] = _rope(q, cos_ref[...], sin_ref[...])
        ks_ref[...] = _rope(k, cos_ref[...], sin_ref[...])
        vs_ref[...] = v

    @pl.when(jnp.logical_not(latent))
    def _():
        q_ref[...] = q
        vp_ref[...] = v
        hw = 2 * DA_HD
        for s in range(TM // SEQ):
            for hd in range(DA_HEADS):
                kpt_ref[s, hd] = k[s * SEQ:(s + 1) * SEQ, hd * hw:(hd + 1) * hw].T


def _rope_tables():
    t = jnp.arange(DEC_SEQ, dtype=F32)
    r, col = jnp.floor(t / GRID_W), t % GRID_W
    inv = ROPE_THETA ** (-jnp.arange(ROPE_NF, dtype=F32) / ROPE_NF)
    ar, ac = r[:, None] * inv, col[:, None] * inv
    cos = jnp.concatenate([jnp.cos(ar), jnp.cos(ar), jnp.cos(ac), jnp.cos(ac)], axis=1)
    sin = jnp.concatenate([-jnp.sin(ar), jnp.sin(ar), -jnp.sin(ac), jnp.sin(ac)], axis=1)
    return jnp.tile(cos, (1, 2)), jnp.tile(sin, (1, 2))


def _att_in(xp, xs, mods, g, w, cos, sin):
    n_p, per_seq = ROWS_P // TM, DEC_SEQ // TM
    tab = pl.BlockSpec((TM, LANES), lambda i: (jnp.maximum(i - n_p, 0) % per_seq, 0))
    out = pl.BlockSpec((TM, DA_QK_W), lambda i: (i, 0))
    out_p, out_s = _group_specs(TM, DA_QK_W)
    full, grp_p, grp_s = (jax.ShapeDtypeStruct((r, DA_QK_W), F32) for r in (ROWS, ROWS_P, ROWS_S))
    seqs = TM // SEQ
    out_kt = pl.BlockSpec((seqs, DA_HEADS, 2 * DA_HD, SEQ), lambda i: (jnp.minimum(i, n_p - 1), 0, 0, 0))
    kt = jax.ShapeDtypeStruct((BATCH, DA_HEADS, 2 * DA_HD, SEQ), F32)
    return pl.pallas_call(
        _att_in_kernel,
        grid=(ROWS // TM,),
        in_specs=[*_group_specs(TM, D_MODEL),
                  pl.BlockSpec((1, 6, D_MODEL), lambda i: (_mod_index(i, TM), 0, 0)),
                  _const_spec((1, D_MODEL)),
                  _const_spec((D_MODEL, ATT_IN_W)),
                  tab, tab],
        out_specs=[out, out_kt, out_s, out_p, out_s, out],
        out_shape=[full, kt, grp_s, grp_p, grp_s, full],
        compiler_params=_cparams("arbitrary"),
        name="att_in",
    )(xp, xs, mods, g, w, cos, sin)


def _attn_kernel(lam_ref, g_ref, q_ref, *rest, n_row_keys, n_values, lam_init):
    k_refs, kt_ref = rest[:n_row_keys], rest[n_row_keys]
    v_refs, o_ref = rest[n_row_keys + 1:n_row_keys + 1 + n_values], rest[-1]
    lv = lam_ref[...]
    lam = (jnp.exp(jnp.sum(lv[0:1] * lv[1:2], keepdims=True))
           - jnp.exp(jnp.sum(lv[2:3] * lv[3:4], keepdims=True)) + lam_init)
    dn = (((1,), (1,)), ((), ()))
    hw = 2 * DA_HD
    lane = lax.broadcasted_iota(jnp.int32, (q_ref.shape[0], hw), 1)
    for h in range(q_ref.shape[1] // hw):
        sl = slice(h * hw, (h + 1) * hw)
        q = q_ref[:, sl] * (DA_HD ** -0.5 * LOG2_E)
        kbs = [k_ref[:, sl].astype(BF16) for k_ref in k_refs]
        ktb = kt_ref[h].astype(BF16)
        vb = jnp.concatenate([v_ref[:, sl].astype(BF16) for v_ref in v_refs], axis=0)

        def unnormalised(qc):
            qb = qc.astype(BF16)
            s = jnp.concatenate([lax.dot_general(qb, kb, dn, preferred_element_type=F32) for kb in kbs]
                                + [jnp.dot(qb, ktb, preferred_element_type=F32)], axis=1)
            e = jnp.exp2(s - jnp.max(s, axis=-1, keepdims=True))
            return (jnp.dot(e.astype(BF16), vb, preferred_element_type=F32),
                    jnp.sum(e, axis=-1, keepdims=True))

        o1, l1 = unnormalised(jnp.where(lane < DA_HD, q, 0.0))
        o2, l2 = unnormalised(jnp.where(lane < DA_HD, 0.0, q))
        o = o1 / l1 - o2 * (lam / l2)
        ms = jnp.mean(o * o, axis=-1, keepdims=True)
        o_ref[:, sl] = (o * lax.rsqrt(ms + EPS) * g_ref[...] * (1.0 - lam_init)).astype(o_ref.dtype)


def _attention(q, row_keys, keys_t, values, lam_vecs, subln_g, *, batch, lq, tq, q_row0, heads_per_step, lam_init):
    nq = lq // tq
    q0 = q_row0 // tq
    hw = 2 * DA_HD
    bw = heads_per_step * hw
    seq_block = lambda a: pl.BlockSpec((a.shape[0] // batch, bw), lambda b, h, i: (b, h))
    return pl.pallas_call(
        functools.partial(_attn_kernel, n_row_keys=len(row_keys), n_values=len(values), lam_init=lam_init),
        grid=(batch, DA_HEADS // heads_per_step, nq),
        in_specs=[_const_spec((4, DA_HD)),
                  _const_spec((1, hw)),
                  pl.BlockSpec((tq, bw), lambda b, h, i: (q0 + b * nq + i, h))]
                 + [seq_block(a) for a in row_keys]
                 + [pl.BlockSpec((None, heads_per_step, hw, keys_t.shape[-1]), lambda b, h, i: (b, h, 0, 0))]
                 + [seq_block(a) for a in values],
        out_specs=pl.BlockSpec((tq, bw), lambda b, h, i: (b * nq + i, h)),
        out_shape=jax.ShapeDtypeStruct((batch * lq, DA_V_W), BF16),
        compiler_params=_cparams("parallel", "parallel", "parallel"),
        name="diff_attn",
    )(lam_vecs, subln_g, q, *row_keys, keys_t, *values)


assert max(POOL_WINDOWS) // 2 <= SUBLANES


def _pooled(ext, pos_ext, seq_len, w_ref, sc_ref, tm):
    n = ext.shape[0]
    masks = {}

    def shifted(a, k):
        if k not in masks:
            masks[k] = (pos_ext + k < seq_len) if k > 0 else (pos_ext + k >= 0)
        return jnp.where(masks[k], pltpu.roll(a, (-k) % n, axis=0), 0.0)

    pos = pos_ext[0:tm]
    out = []
    for gi, win in enumerate(POOL_WINDOWS):
        half = win // 2
        sl = slice(gi * POOL_GC, (gi + 1) * POOL_GC)
        xg = ext[:, sl]
        right, left, step = xg, xg, 1
        while step < half:
            right, left, step = right + shifted(right, step), left + shifted(left, -step), 2 * step
        acc = (right + shifted(left, -1))[0:tm]
        cnt = (jnp.minimum(pos + half, seq_len) - jnp.maximum(pos - half, 0)).astype(F32)
        pooled = acc / cnt - xg[0:tm]
        y = jnp.dot(pooled.astype(BF16), w_ref[gi], preferred_element_type=F32)
        out.append((y * sc_ref[:, sl]).astype(BF16))
    return jnp.concatenate(out, axis=1)


def _mix_out_kernel(mod_ref, xp_ref, xs_ref, op_ref, os_ref, pool_ref, poolp_ref, pooln_ref,
                    pw_ref, ps_ref, wa_ref, wb_ref, o_ref):
    i = pl.program_id(0)
    prompt = i < ROWS_P // TM
    seq_len = jnp.where(prompt, SEQ, DEC_SEQ)
    ext = jnp.concatenate([pool_ref[...], pooln_ref[...], poolp_ref[...]], axis=0)
    n = ext.shape[0]
    r = lax.broadcasted_iota(jnp.int32, (n, POOL_GC), 0)
    r = jnp.where(r < TM + SUBLANES, r, r - n)
    pooled = _pooled(ext, (i * TM + r) & (seq_len - 1), seq_len, pw_ref, ps_ref, TM)
    att = _group_rows(i, TM, op_ref, os_ref)
    acc = (jnp.dot(att, wa_ref[...], preferred_element_type=F32)
           + jnp.dot(pooled, wb_ref[...], preferred_element_type=F32))
    o_ref[...] = _group_rows(i, TM, xp_ref, xs_ref) + mod_ref[0, 2:3] * acc


def _mix_out(x, mods, att, xpool, pool_w, pool_scale, w_att, w_pool):
    prev, nxt = _halo_specs(TM, POOL_W)
    return pl.pallas_call(
        _mix_out_kernel,
        grid=(ROWS // TM,),
        in_specs=[pl.BlockSpec((1, 6, D_MODEL), lambda i: (_mod_index(i, TM), 0, 0)),
                  *_group_specs(TM, D_MODEL), *_group_specs(TM, DA_V_W),
                  pl.BlockSpec((TM, POOL_W), lambda i: (i, 0)), prev, nxt,
                  _const_spec(pool_w.shape), _const_spec(pool_scale.shape),
                  _const_spec(w_att.shape), _const_spec(w_pool.shape)],
        out_specs=pl.BlockSpec((TM, D_MODEL), lambda i: (i, 0)),
        out_shape=jax.ShapeDtypeStruct((ROWS, D_MODEL), F32),
        compiler_params=_cparams("parallel"),
        name="mix_out",
    )(mods, *x, *att, xpool, xpool, xpool, pool_w, pool_scale, w_att, w_pool)


def _proj_res_kernel(*refs, gate_row, grouped):
    i = pl.program_id(0)
    mod_ref, o_ref = refs[0], refs[-1]
    pos, rows = 1, []
    for is_pair in grouped:
        rows.append(_group_rows(i, TM, refs[pos], refs[pos + 1]) if is_pair else refs[pos][...])
        pos += 2 if is_pair else 1
    acc = None
    for a, w_ref in zip(rows[1:], refs[pos:-1]):
        d = jnp.dot(a.astype(BF16), w_ref[...], preferred_element_type=F32)
        acc = d if acc is None else acc + d
    o_ref[...] = rows[0] + mod_ref[0, gate_row:gate_row + 1] * acc


def _proj_res(x, mods, acts, ws, *, gate_row):
    operands, specs, grouped = [], [], []
    for a in (x, *acts):
        is_pair = isinstance(a, tuple)
        grouped.append(is_pair)
        if is_pair:
            operands += list(a)
            specs += list(_group_specs(TM, a[0].shape[1]))
        else:
            operands.append(a)
            specs.append(pl.BlockSpec((TM, a.shape[1]), lambda i: (i, 0)))
    return pl.pallas_call(
        functools.partial(_proj_res_kernel, gate_row=gate_row, grouped=tuple(grouped)),
        grid=(ROWS // TM,),
        in_specs=[pl.BlockSpec((1, 6, D_MODEL), lambda i: (_mod_index(i, TM), 0, 0))]
                 + specs + [_const_spec(w.shape) for w in ws],
        out_specs=pl.BlockSpec((TM, D_MODEL), lambda i: (i, 0)),
        out_shape=jax.ShapeDtypeStruct((ROWS, D_MODEL), F32),
        compiler_params=_cparams("parallel"),
        name="proj_res",
    )(mods, *operands, *ws)


def _ffn_kernel(x_ref, xp_ref, xn_ref, mod_ref, g_ref, wup_ref, cw_ref, cb_ref, wd_ref, fg_ref,
                *rest, last_layer):
    h_ref, act_ref = rest[-2:]
    i = pl.program_id(0)
    tm = TM_CONV
    modulate = functools.partial(_modulate, g=g_ref[...], shift=mod_ref[0, 3:4], scale=mod_ref[0, 4:5])
    _fill_with_halo(h_ref, x_ref, xp_ref, xn_ref, modulate, i, tm)
    h = h_ref[...]
    prompt = i < ROWS_P // tm
    for c in range(D_FF // TF):
        gs, vs = slice(c * TF, (c + 1) * TF), slice(D_FF + c * TF, D_FF + (c + 1) * TF)
        ug = jnp.dot(h, wup_ref[:, gs], preferred_element_type=F32)
        uv = jnp.dot(h, wup_ref[:, vs], preferred_element_type=F32)
        cg = _conv3(ug, cw_ref[:, gs], cb_ref[:, gs], tm, prompt)
        cv = _conv3(uv, cw_ref[:, vs], cb_ref[:, vs], tm, prompt)
        act_ref[:, gs] = (_silu(cg) * cv).astype(BF16)
    y = x_ref[...] + mod_ref[0, 5:6] * jnp.dot(act_ref[...], wd_ref[...], preferred_element_type=F32)
    if not last_layer:
        rest[0][...] = y
        return
    ms = jnp.mean(y * y, axis=-1, keepdims=True)
    y = y * lax.rsqrt(ms + EPS) * fg_ref[...]
    op_ref, os_ref = rest[:2]

    @pl.when(prompt)
    def _():
        op_ref[...] = y

    @pl.when(jnp.logical_not(prompt))
    def _():
        os_ref[...] = y


def _conv_ffn(x, mods, g, w_up, conv_w, conv_b, w_down, final_g, *, layer, last_layer):
    tm = TM_CONV
    prev, nxt = _halo_specs(tm, D_MODEL)
    if last_layer:
        out_specs = list(_group_specs(tm, D_MODEL))
        out_shape = [jax.ShapeDtypeStruct((ROWS_P, D_MODEL), F32), jax.ShapeDtypeStruct((ROWS_S, D_MODEL), F32)]
    else:
        out_specs = pl.BlockSpec((tm, D_MODEL), lambda i: (i, 0))
        out_shape = jax.ShapeDtypeStruct((ROWS, D_MODEL), F32)
    return pl.pallas_call(
        functools.partial(_ffn_kernel, last_layer=last_layer),
        grid=(ROWS // tm,),
        in_specs=[pl.BlockSpec((tm, D_MODEL), lambda i: (i, 0)), prev, nxt,
                  pl.BlockSpec((1, 6, D_MODEL), lambda i: (_mod_index(i, tm), 0, 0)),
                  _const_spec((1, D_MODEL)),
                  _layer_spec(w_up.shape, layer), _const_spec(conv_w.shape), _const_spec(conv_b.shape),
                  _layer_spec(w_down.shape, layer), _const_spec((1, D_MODEL))],
        out_specs=out_specs,
        out_shape=out_shape,
        scratch_shapes=[pltpu.VMEM((tm + HALO, D_MODEL), BF16), pltpu.VMEM((tm, D_FF), BF16)],
        compiler_params=_cparams("arbitrary"),
        name="conv_ffn",
    )(x, x, x, mods, g, w_up, conv_w, conv_b, w_down, final_g)


XBC_COLS = 256


def _ssd_xbc_kernel(x_ref, xp_ref, xn_ref, mod_ref, g_ref, wx_ref, cw_ref, cb_ref, xbcp_ref, xbcs_ref, h_ref):
    i = pl.program_id(0)
    tm = TM_CONV
    modulate = functools.partial(_modulate, g=g_ref[...], shift=mod_ref[0, 0:1], scale=mod_ref[0, 1:2])
    _fill_with_halo(h_ref, x_ref, xp_ref, xn_ref, modulate, i, tm)

    def project(xbc_ref, is_prompt):
        h = h_ref[...]
        for c in range(SSD_XBC_W // XBC_COLS):
            sl = slice(c * XBC_COLS, (c + 1) * XBC_COLS)
            u = jnp.dot(h, wx_ref[:, sl], preferred_element_type=F32)
            xbc_ref[:, sl] = _silu(_conv3(u, cw_ref[:, sl], cb_ref[:, sl], tm, is_prompt))

    prompt = i < ROWS_P // tm
    pl.when(prompt)(lambda: project(xbcp_ref, True))
    pl.when(jnp.logical_not(prompt))(lambda: project(xbcs_ref, False))


def _ssd_zdt_kernel(x_ref, mod_ref, g_ref, wz_ref, wdt_ref, zp_ref, zs_ref, dtp_ref, dts_ref):
    i = pl.program_id(0)
    h = _modulate(x_ref[...], g_ref[...], mod_ref[0, 0:1], mod_ref[0, 1:2]).astype(BF16)

    def project(z_ref, dt_ref):
        z_ref[...] = jnp.dot(h, wz_ref[...], preferred_element_type=F32)
        dt_ref[...] = jnp.dot(h, wdt_ref[...], preferred_element_type=F32)

    prompt = i < ROWS_P // TM
    pl.when(prompt)(lambda: project(zp_ref, dtp_ref))
    pl.when(jnp.logical_not(prompt))(lambda: project(zs_ref, dts_ref))


def _ssd_in(x, mods, g, wz, wx, wdt, conv_w, conv_b):
    tm = TM_CONV
    prev, nxt = _halo_specs(tm, D_MODEL)
    mod = lambda t: pl.BlockSpec((1, 6, D_MODEL), lambda i: (_mod_index(i, t), 0, 0))
    group_shapes = lambda w: [jax.ShapeDtypeStruct((r, w), F32) for r in (ROWS_P, ROWS_S)]
    xbc = pl.pallas_call(
        _ssd_xbc_kernel,
        grid=(ROWS // tm,),
        in_specs=[pl.BlockSpec((tm, D_MODEL), lambda i: (i, 0)), prev, nxt, mod(tm), _const_spec((1, D_MODEL)),
                  _const_spec(wx.shape), _const_spec(conv_w.shape), _const_spec(conv_b.shape)],
        out_specs=list(_group_specs(tm, SSD_XBC_W)),
        out_shape=group_shapes(SSD_XBC_W),
        scratch_shapes=[pltpu.VMEM((tm + HALO, D_MODEL), BF16)],
        compiler_params=_cparams("arbitrary"),
        name="ssd_xbc",
    )(x, x, x, mods, g, wx, conv_w, conv_b)
    zp, zs, dtp, dts = pl.pallas_call(
        _ssd_zdt_kernel,
        grid=(ROWS // TM,),
        in_specs=[pl.BlockSpec((TM, D_MODEL), lambda i: (i, 0)), mod(TM), _const_spec((1, D_MODEL)),
                  _const_spec(wz.shape), _const_spec(wdt.shape)],
        out_specs=[*_group_specs(TM, SSD_DI), *_group_specs(TM, 2 * SSD_HEADS)],
        out_shape=group_shapes(SSD_DI) + group_shapes(2 * SSD_HEADS),
        compiler_params=_cparams("arbitrary"),
        name="ssd_zdt",
    )(x, mods, g, wz, wdt)
    return (zp, zs), tuple(xbc), (dtp, dts)


STREAMS = 2


def _split3(v):
    hi = v.astype(BF16)
    r1 = v - hi.astype(F32)
    mid = r1.astype(BF16)
    return hi, mid, (r1 - mid.astype(F32)).astype(BF16)


def _ssd_kernel(*refs, reverse, final, has_init, emit_state):
    refs = list(refs)
    xbc_ref, dt_ref, dtb_ref, alog_ref = refs[:4]
    del refs[:4]
    init_ref = refs.pop(0) if has_init else None
    wide_ref = refs.pop(0)
    if final:
        yin_ref, z_ref, dexp_ref, ng_ref = refs[:4]
        del refs[:4]
    y_ref = refs.pop(0)
    sfin_ref = refs.pop(0) if emit_state else None
    s_ref = refs.pop(0)
    yacc_ref = refs.pop(0) if final else None
    pos = pl.program_id(1)
    q = SSD_CHUNK
    hh = SSD_HEADS
    gw = SSD_DI // SSD_GROUPS

    @pl.when(pos == 0)
    def _():
        s_ref[...] = init_ref[...] if has_init else jnp.zeros_like(s_ref)

    d0 = hh if reverse else 0
    a = -jnp.exp(alog_ref[:, d0:d0 + hh])
    row = lax.broadcasted_iota(jnp.int32, (q, q), 0)
    col = lax.broadcasted_iota(jnp.int32, (q, q), 1)
    tri = (row <= col) if reverse else (row >= col)
    lane_lo = col < SSD_HEADDIM
    dn_t = (((1,), (1,)), ((), ()))

    def chunk_factors(k):
        dt_in = dt_ref[k, :, d0:d0 + hh] + dtb_ref[:, d0:d0 + hh]
        dt = jnp.maximum(dt_in, 0.0) + jnp.log1p(jnp.exp(-jnp.abs(dt_in)))
        acs = jnp.dot(tri.astype(F32), dt * a, precision=HIGHEST, preferred_element_type=F32)
        a_end = acs[0:1] if reverse else acs[q - 1:q]
        factors = jnp.concatenate([jnp.exp(acs), jnp.exp(a_end - acs) * dt,
                                   jnp.broadcast_to(jnp.exp(a_end), (SUBLANES, hh))], axis=0)
        wide = jnp.dot(jnp.concatenate(_split3(factors), axis=1), wide_ref[...], preferred_element_type=F32)
        acs2 = acs * LOG2_E
        mt = jnp.concatenate([dt, acs2, jnp.zeros((q, 2 * hh), F32)], axis=1).T
        return wide[0:q], wide[q:2 * q], wide[2 * q:2 * q + 1], acs2, mt

    def group(k, g, eacs_w, wst_w, dec_w, acs2, mt):
        gs = slice(g * gw, (g + 1) * gw)
        b_g = xbc_ref[k, :, SSD_DI + g * SSD_STATE:SSD_DI + (g + 1) * SSD_STATE]
        c_g = xbc_ref[k, :, SSD_DI + SSD_BC_W + g * SSD_STATE:SSD_DI + SSD_BC_W + (g + 1) * SSD_STATE]
        cgb = c_g.astype(BF16)
        cb = lax.dot_general(cgb, b_g.astype(BF16), dn_t, preferred_element_type=F32)
        s_g = s_ref[k, :, gs]
        y_off = jnp.dot(cgb, s_g.astype(BF16), preferred_element_type=F32) * eacs_w[:, gs]
        x_g = xbc_ref[k, :, gs]
        s_new = jnp.dot(b_g.T.astype(BF16), (x_g * wst_w[:, gs]).astype(BF16), preferred_element_type=F32)
        s_ref[k, :, gs] = s_g * dec_w[:, gs] + s_new
        for p in range(gw // LANES):
            ps = slice(p * LANES, (p + 1) * LANES)
            sl = slice(g * gw + p * LANES, g * gw + (p + 1) * LANES)
            xp = x_g[:, ps]
            rhs = jnp.concatenate([jnp.where(lane_lo, xp, 0.0).astype(BF16),
                                   jnp.where(lane_lo, 0.0, xp).astype(BF16)], axis=0)
            w_l = []
            for h in (2 * (g * 4 + p), 2 * (g * 4 + p) + 1):
                seg = jnp.broadcast_to(acs2[:, h:h + 1], (q, q)) - mt[hh + h:hh + h + 1]
                lm = jnp.exp2(jnp.where(tri, seg, -jnp.inf))
                w_l.append((cb * lm * mt[h:h + 1]).astype(BF16))
            yp = jnp.dot(jnp.concatenate(w_l, axis=1), rhs, preferred_element_type=F32) + y_off[:, ps]
            if final:
                yacc_ref[k, :, sl] = yp
            else:
                y_ref[k, :, sl] = yp

    per_stream = [chunk_factors(k) for k in range(STREAMS)]
    for g in range(SSD_GROUPS):
        for k in range(STREAMS):
            group(k, g, *per_stream[k])

    if final:
        for k in range(STREAMS):
            yt = (yacc_ref[k] + yin_ref[k] + xbc_ref[k, :, 0:SSD_DI] * dexp_ref[...]) * _silu(z_ref[k])
            ms = jnp.mean(yt * yt, axis=-1, keepdims=True)
            y_ref[k] = (yt * lax.rsqrt(ms + EPS) * ng_ref[...]).astype(y_ref.dtype)

    if emit_state:
        @pl.when(pos == pl.num_programs(1) - 1)
        def _():
            for k in range(STREAMS):
                sfin_ref[k] = s_ref[k].T


def _ssd_scan(xbc, dt, dt_bias, a_log, init, extra, *, reverse, nseq, emit_state):
    final = extra is not None
    has_init = init is not None
    assert nseq % STREAMS == 0 and (not has_init or nseq == STREAMS)
    nc = xbc.shape[0] // (nseq * SSD_CHUNK)
    npairs = nseq // STREAMS
    view = lambda a: a.reshape(npairs, STREAMS, nc, SSD_CHUNK, a.shape[-1])
    chunk = lambda width: pl.BlockSpec((None, STREAMS, None, SSD_CHUNK, width),
                                       lambda p, t: (p, 0, (nc - 1 - t) if reverse else t, 0, 0))
    widen = jnp.tile(jnp.repeat(jnp.eye(SSD_HEADS, dtype=BF16), SSD_HEADDIM, axis=1), (3, 1))
    in_specs = [chunk(SSD_XBC_W), chunk(2 * SSD_HEADS),
                _const_spec((1, 2 * SSD_HEADS)), _const_spec((1, 2 * SSD_HEADS))]
    args = [view(xbc), view(dt), dt_bias, a_log]
    if has_init:
        in_specs.append(_const_spec(init.shape))
        args.append(init)
    in_specs.append(_const_spec(widen.shape))
    args.append(widen)
    scratch = [pltpu.VMEM((STREAMS, SSD_STATE, SSD_DI), F32)]
    if final:
        y_other, z, d_exp, norm_g = extra
        in_specs += [chunk(SSD_DI), chunk(SSD_DI), _const_spec((1, SSD_DI)), _const_spec((1, SSD_DI))]
        args += [view(y_other), view(z), d_exp, norm_g]
        scratch.append(pltpu.VMEM((STREAMS, SSD_CHUNK, SSD_DI), F32))
    out_specs = [chunk(SSD_DI)]
    out_shape = [jax.ShapeDtypeStruct((npairs, STREAMS, nc, SSD_CHUNK, SSD_DI), BF16 if final else F32)]
    if emit_state:
        out_specs.append(pl.BlockSpec((None, STREAMS, SSD_DI, SSD_STATE), lambda p, t: (p, 0, 0, 0)))
        out_shape.append(jax.ShapeDtypeStruct((npairs, STREAMS, SSD_DI, SSD_STATE), F32))
    outs = pl.pallas_call(
        functools.partial(_ssd_kernel, reverse=reverse, final=final, has_init=has_init, emit_state=emit_state),
        grid=(npairs, nc),
        in_specs=in_specs,
        out_specs=out_specs,
        out_shape=out_shape,
        scratch_shapes=scratch,
        compiler_params=_cparams("arbitrary", "arbitrary"),
        name="ssd_scan_bwd" if reverse else "ssd_scan_fwd",
    )(*args)
    y = outs[0].reshape(xbc.shape[0], SSD_DI)
    return (y, outs[1].reshape(nseq, SSD_DI, SSD_STATE)) if emit_state else y


def kernel(x_prompt, x_sample, cache_k, cache_v, state_ssm_fwd, state_ssm_bwd, c, c_ctx, ada_w, ada_b, norm_mix_g, norm_ffn_g, att_w_in, att_lambda, att_subln_g, pool_w, pool_scale, att_w_out, ssd_w_in, ssd_conv_w, ssd_conv_b, ssd_dt_bias, ssd_a_log, ssd_d, ssd_norm_g, ssd_w_out, ffn_w_up, ffn_conv_w, ffn_conv_b, ffn_w_down, final_norm_g):
    x = (x_prompt.reshape(ROWS_P, D_MODEL), x_sample.reshape(ROWS_S, D_MODEL))
    cvec = jnp.concatenate([c_ctx[None], c, jnp.zeros((MOD_PAD - N_MOD, D_MODEL), F32)], axis=0)
    mods_all = _adaln(cvec, ada_w, ada_b).reshape(DEPTH, MOD_PAD, 6, D_MODEL)
    cos, sin = _rope_tables()
    ffn_up, ffn_down = ffn_w_up.astype(BF16), ffn_w_down.astype(BF16)
    outs = {}
    for l in range(DEPTH):
        i = l // 2
        mods = mods_all[l, :N_MOD]
        if l % 2 == 0:
            lam_init = 0.8 - 0.6 * math.exp(-0.3 * l)
            if not isinstance(x, tuple):
                x = (x[:ROWS_P], x[ROWS_P:])
            q, kt_p, k_s, v_p, v_s, xpool = _att_in(*x, mods, norm_mix_g[l][None], att_w_in[i].astype(BF16), cos, sin)
            ckt = cache_k[:, i].transpose(0, 2, 3, 4, 1).reshape(DEC_BATCH, DA_HEADS, 2 * DA_HD, PAST_LEN)
            cv = cache_v[:, i].reshape(DEC_BATCH * PAST_LEN, DA_V_W)
            att = functools.partial(_attention, lam_vecs=att_lambda[i], subln_g=att_subln_g[i][None],
                                    lam_init=lam_init)
            o_p = att(q, [], kt_p, [v_p], batch=BATCH, lq=SEQ, tq=SEQ, q_row0=0, heads_per_step=DA_HEADS)
            o_s = att(q, [k_s], ckt, [v_s, cv], batch=DEC_BATCH, lq=DEC_SEQ, tq=TQ, q_row0=ROWS_P, heads_per_step=1)
            w_out = att_w_out[i].astype(BF16)
            x = _mix_out(x, mods, (o_p, o_s), xpool, pool_w[i].astype(BF16), pool_scale[i][None],
                         w_out[:DA_V_W], w_out[DA_V_W:])
            outs.setdefault("k", []).append(
                kt_p.reshape(BATCH, DA_HEADS, 2, DA_HD, SEQ).transpose(0, 4, 1, 2, 3))
            outs.setdefault("v", []).append(v_p.reshape(BATCH, SEQ, DA_HEADS, 2 * DA_HD))
        else:
            w_in = ssd_w_in[i]
            z, xbc, dt = _ssd_in(x, mods, norm_mix_g[l][None], w_in[:, :SSD_DI].astype(BF16),
                                 w_in[:, SSD_DI:SSD_DI + SSD_XBC_W].astype(BF16),
                                 w_in[:, SSD_DI + SSD_XBC_W:].astype(BF16),
                                 ssd_conv_w[i], ssd_conv_b[i][None])
            to_scan_layout = lambda s: s.transpose(0, 3, 1, 2).reshape(DEC_BATCH, SSD_STATE, SSD_DI)
            dtb, alog = ssd_dt_bias[i].reshape(1, -1), ssd_a_log[i].reshape(1, -1)
            d_exp, norm_g = jnp.repeat(ssd_d[i], SSD_HEADDIM)[None], ssd_norm_g[i][None]
            scan_p = functools.partial(_ssd_scan, xbc[0], dt[0], dtb, alog, None, nseq=BATCH, emit_state=True)
            y_f, s_f = scan_p(None, reverse=False)
            y_p, s_b = scan_p((y_f, z[0], d_exp, norm_g), reverse=True)
            scan_s = functools.partial(_ssd_scan, xbc[1], dt[1], dtb, alog, nseq=DEC_BATCH, emit_state=False)
            y_f = scan_s(to_scan_layout(state_ssm_fwd[:, i]), None, reverse=False)
            y_s = scan_s(to_scan_layout(state_ssm_bwd[:, i]), (y_f, z[1], d_exp, norm_g), reverse=True)
            x = _proj_res(x, mods, [(y_p, y_s)], [ssd_w_out[i].astype(BF16)], gate_row=2)
            outs.setdefault("sf", []).append(s_f.reshape(BATCH, SSD_HEADS, SSD_HEADDIM, SSD_STATE))
            outs.setdefault("sb", []).append(s_b.reshape(BATCH, SSD_HEADS, SSD_HEADDIM, SSD_STATE))
        x = _conv_ffn(x, mods, norm_ffn_g[l][None], ffn_up, ffn_conv_w[l], ffn_conv_b[l][None],
                      ffn_down, final_norm_g[None], layer=l, last_layer=(l == DEPTH - 1))
    y_prompt = x[0].reshape(BATCH, SEQ, D_MODEL)
    y_sample = x[1].reshape(DEC_BATCH, DEC_SEQ, D_MODEL)
    return (y_prompt, y_sample, jnp.stack(outs["k"], axis=1), jnp.stack(outs["v"], axis=1),
            jnp.stack(outs["sf"], axis=1), jnp.stack(outs["sb"], axis=1))
```

```python
import functools
import math

import jax
import jax.numpy as jnp
from jax import lax
from jax.experimental import pallas as pl
from jax.experimental.pallas import tpu as pltpu

F32 = jnp.float32
BF16 = jnp.bfloat16
HIGHEST = lax.Precision.HIGHEST

D_MODEL = 1024
BATCH = 32
SEQ = 256
DEPTH = 2
DEC_BATCH = 2
DEC_SEQ = 2048
PAST_LEN = 512
GRID_W = 64
EPS = 1e-6
DA_HEADS = 4
DA_HD = 64
DA_QK_W = DA_HEADS * 2 * DA_HD
DA_V_W = DA_HEADS * 2 * DA_HD
POOL_W = D_MODEL - DA_V_W
POOL_WINDOWS = (2, 4, 8, 16)
POOL_GROUPS = 4
POOL_GC = POOL_W // POOL_GROUPS
ATT_IN_W = 2 * DA_QK_W + DA_V_W + POOL_W
LOG2_E = math.log2(math.e)
ROPE_THETA = 10000.0
ROPE_NF = DA_HD // 4
SSD_DI = 2 * D_MODEL
SSD_HEADDIM = 64
SSD_HEADS = SSD_DI // SSD_HEADDIM
SSD_GROUPS = 4
SSD_STATE = 128
SSD_CHUNK = 128
SSD_BC_W = SSD_GROUPS * SSD_STATE
SSD_XBC_W = SSD_DI + 2 * SSD_BC_W
D_FF = 2816

ROWS_P = BATCH * SEQ
ROWS_S = DEC_BATCH * DEC_SEQ
ROWS = ROWS_P + ROWS_S
N_MOD = 1 + DEC_BATCH
MOD_PAD = 8

LANES = 128
SUBLANES = 8
HALO = 2 * SUBLANES
VMEM_LIMIT = 56 * 1024 * 1024

TM = 512
TM_CONV = 512
TF = 256
TQ = 256


def _cparams(*sem):
    return pltpu.CompilerParams(dimension_semantics=sem, vmem_limit_bytes=VMEM_LIMIT)


def _mod_index(i, tm):
    n_p, per_seq = ROWS_P // tm, DEC_SEQ // tm
    return jnp.where(i < n_p, 0, 1 + (i - n_p) // per_seq)


def _silu(x):
    return x / (1.0 + jnp.exp2(x * -LOG2_E))


def _modulate(x, g, shift, scale):
    ms = jnp.mean(x * x, axis=-1, keepdims=True)
    y = x * lax.rsqrt(ms + EPS) * g
    return y * (1.0 + scale) + shift


def _const_spec(shape):
    nd = len(shape)
    return pl.BlockSpec(shape, lambda *_: (0,) * nd, pipeline_mode=pl.Buffered(1))


def _layer_spec(stacked_shape, layer):
    nd = len(stacked_shape) - 1
    return pl.BlockSpec((None, *stacked_shape[1:]), lambda *_: (layer,) + (0,) * nd, pipeline_mode=pl.Buffered(1))


def _group_specs(tm, width):
    n_p = ROWS_P // tm
    return (pl.BlockSpec((tm, width), lambda i: (jnp.minimum(i, n_p - 1), 0)),
            pl.BlockSpec((tm, width), lambda i: (jnp.maximum(i - n_p, 0), 0)))


def _group_rows(i, tm, p_ref, s_ref):
    return jnp.where(i < ROWS_P // tm, p_ref[...], s_ref[...])


def _halo_specs(tm, width):
    per = tm // SUBLANES
    last = ROWS // SUBLANES - 1
    prev = pl.BlockSpec((SUBLANES, width), lambda i: (jnp.maximum(i * per - 1, 0), 0))
    nxt = pl.BlockSpec((SUBLANES, width), lambda i: (jnp.minimum((i + 1) * per, last), 0))
    return prev, nxt


def _fill_with_halo(h_ref, x_ref, xp_ref, xn_ref, modulate, i, tm):
    m = jnp.where(i < ROWS_P // tm, SEQ - 1, DEC_SEQ - 1)
    starts = ((i * tm) & m) == 0
    ends = (((i + 1) * tm) & m) == 0
    h_ref[0:tm] = modulate(x_ref[...]).astype(BF16)
    halo = jnp.concatenate([jnp.where(ends, 0.0, modulate(xn_ref[...])),
                            jnp.where(starts, 0.0, modulate(xp_ref[...]))], axis=0)
    h_ref[tm:tm + HALO] = halo.astype(BF16)


def _conv3(u, cw, cb, tm, prompt):
    n = u.shape[0]
    up = pltpu.roll(u, 1, axis=0)[0:tm]
    un = pltpu.roll(u, n - 1, axis=0)[0:tm]
    if tm > SEQ and prompt is not False:
        s = SUBLANES
        sub = lax.broadcasted_iota(jnp.int32, (s, u.shape[1]), 0)
        at_edge = (lambda m: m) if prompt is True else (lambda m: prompt & m)
        ups, uns, lo = [], [], 0
        for edge in range(SEQ, tm, SEQ):
            ups += [up[lo:edge], jnp.where(at_edge(sub == 0), 0.0, up[edge:edge + s])]
            uns += [un[lo - s if lo else 0:edge - s], jnp.where(at_edge(sub == s - 1), 0.0, un[edge - s:edge])]
            lo = edge + s
        up = jnp.concatenate(ups + [up[lo:tm]], axis=0)
        un = jnp.concatenate(uns + [un[lo - s:tm]], axis=0)
    return cw[1:2] * u[0:tm] + cb + cw[0:1] * up + cw[2:3] * un


def _adaln_kernel(c_ref, w_ref, b_ref, o_ref):
    s = _silu(c_ref[...])
    o_ref[0] = jnp.dot(s, w_ref[0], precision=HIGHEST, preferred_element_type=F32) + b_ref[0]


def _adaln(cvec, ada_w, ada_b):
    tn = 1536
    n = 6 * D_MODEL
    return pl.pallas_call(
        _adaln_kernel,
        grid=(DEPTH, n // tn),
        in_specs=[_const_spec((MOD_PAD, D_MODEL)),
                  pl.BlockSpec((1, D_MODEL, tn), lambda l, j: (l, 0, j)),
                  pl.BlockSpec((1, 1, tn), lambda l, j: (l, 0, j))],
        out_specs=pl.BlockSpec((1, MOD_PAD, tn), lambda l, j: (l, 0, j)),
        out_shape=jax.ShapeDtypeStruct((DEPTH, MOD_PAD, n), F32),
        compiler_params=_cparams("parallel", "parallel"),
        name="adaln",
    )(cvec, ada_w, ada_b.reshape(DEPTH, 1, n))


def _rope(x, cos, sin_signed):
    lane = lax.broadcasted_iota(jnp.int32, cos.shape, 1)
    lower = (lane & 31) < ROPE_NF
    out = []
    for s in range(x.shape[1] // LANES):
        xs = x[:, s * LANES:(s + 1) * LANES]
        partner = jnp.where(lower, pltpu.roll(xs, LANES - ROPE_NF, axis=1), pltpu.roll(xs, ROPE_NF, axis=1))
        out.append(xs * cos + partner * sin_signed)
    return jnp.concatenate(out, axis=1)


def _att_in_kernel(xp_ref, xs_ref, mod_ref, g_ref, w_ref, cos_ref, sin_ref,
                   q_ref, kpt_ref, ks_ref, vp_ref, vs_ref, p_ref):
    i = pl.program_id(0)
    x = _group_rows(i, TM, xp_ref, xs_ref)
    h = _modulate(x, g_ref[...], mod_ref[0, 0:1], mod_ref[0, 1:2]).astype(BF16)
    proj = jnp.dot(h, w_ref[...], preferred_element_type=F32)
    q = proj[:, :DA_QK_W]
    k = proj[:, DA_QK_W:2 * DA_QK_W]
    v = proj[:, 2 * DA_QK_W:2 * DA_QK_W + DA_V_W]
    p_ref[...] = proj[:, 2 * DA_QK_W + DA_V_W:]
    latent = i >= ROWS_P // TM

    @pl.when(latent)
    def _():
        q_ref[...] = _rope(q, cos_ref[...], sin_ref[...])
        ks_ref[...] = _rope(k, cos_ref[...], sin_ref[...])
        vs_ref[...] = v

    @pl.when(jnp.logical_not(latent))
    def _():
        q_ref[...] = q
        vp_ref[...] = v
        hw = 2 * DA_HD
        for s in range(TM // SEQ):
            for hd in range(DA_HEADS):
                kpt_ref[s, hd] = k[s * SEQ:(s + 1) * SEQ, hd * hw:(hd + 1) * hw].T


def _rope_tables():
    t = jnp.arange(DEC_SEQ, dtype=F32)
    r, col = jnp.floor(t / GRID_W), t % GRID_W
    inv = ROPE_THETA ** (-jnp.arange(ROPE_NF, dtype=F32) / ROPE_NF)
    ar, ac = r[:, None] * inv, col[:, None] * inv
    cos = jnp.concatenate([jnp.cos(ar), jnp.cos(ar), jnp.cos(ac), jnp.cos(ac)], axis=1)
    sin = jnp.concatenate([-jnp.sin(ar), jnp.sin(ar), -jnp.sin(ac), jnp.sin(ac)], axis=1)
    return jnp.tile(cos, (1, 2)), jnp.tile(sin, (1, 2))


def _att_in(xp, xs, mods, g, w, cos, sin):
    n_p, per_seq = ROWS_P // TM, DEC_SEQ // TM
    tab = pl.BlockSpec((TM, LANES), lambda i: (jnp.maximum(i - n_p, 0) % per_seq, 0))
    out = pl.BlockSpec((TM, DA_QK_W), lambda i: (i, 0))
    out_p, out_s = _group_specs(TM, DA_QK_W)
    full, grp_p, grp_s = (jax.ShapeDtypeStruct((r, DA_QK_W), F32) for r in (ROWS, ROWS_P, ROWS_S))
    seqs = TM // SEQ
    out_kt = pl.BlockSpec((seqs, DA_HEADS, 2 * DA_HD, SEQ), lambda i: (jnp.minimum(i, n_p - 1), 0, 0, 0))
    kt = jax.ShapeDtypeStruct((BATCH, DA_HEADS, 2 * DA_HD, SEQ), F32)
    return pl.pallas_call(
        _att_in_kernel,
        grid=(ROWS // TM,),
        in_specs=[*_group_specs(TM, D_MODEL),
                  pl.BlockSpec((1, 6, D_MODEL), lambda i: (_mod_index(i, TM), 0, 0)),
                  _const_spec((1, D_MODEL)),
                  _const_spec((D_MODEL, ATT_IN_W)),
                  tab, tab],
        out_specs=[out, out_kt, out_s, out_p, out_s, out],
        out_shape=[full, kt, grp_s, grp_p, grp_s, full],
        compiler_params=_cparams("arbitrary"),
        name="att_in",
    )(xp, xs, mods, g, w, cos, sin)


def _attn_kernel(lam_ref, g_ref, q_ref, *rest, n_row_keys, n_values, lam_init):
    k_refs, kt_ref = rest[:n_row_keys], rest[n_row_keys]
    v_refs, o_ref = rest[n_row_keys + 1:n_row_keys + 1 + n_values], rest[-1]
    lv = lam_ref[...]
    lam = (jnp.exp(jnp.sum(lv[0:1] * lv[1:2], keepdims=True))
           - jnp.exp(jnp.sum(lv[2:3] * lv[3:4], keepdims=True)) + lam_init)
    dn = (((1,), (1,)), ((), ()))
    hw = 2 * DA_HD
    lane = lax.broadcasted_iota(jnp.int32, (q_ref.shape[0], hw), 1)
    for h in range(q_ref.shape[1] // hw):
        sl = slice(h * hw, (h + 1) * hw)
        q = q_ref[:, sl] * (DA_HD ** -0.5 * LOG2_E)
        kbs = [k_ref[:, sl].astype(BF16) for k_ref in k_refs]
        ktb = kt_ref[h].astype(BF16)
        vb = jnp.concatenate([v_ref[:, sl].astype(BF16) for v_ref in v_refs], axis=0)

        def unnormalised(qc):
            qb = qc.astype(BF16)
            s = jnp.concatenate([lax.dot_general(qb, kb, dn, preferred_element_type=F32) for kb in kbs]
                                + [jnp.dot(qb, ktb, preferred_element_type=F32)], axis=1)
            e = jnp.exp2(s - jnp.max(s, axis=-1, keepdims=True))
            return (jnp.dot(e.astype(BF16), vb, preferred_element_type=F32),
                    jnp.sum(e, axis=-1, keepdims=True))

        o1, l1 = unnormalised(jnp.where(lane < DA_HD, q, 0.0))
        o2, l2 = unnormalised(jnp.where(lane < DA_HD, 0.0, q))
        o = o1 / l1 - o2 * (lam / l2)
        ms = jnp.mean(o * o, axis=-1, keepdims=True)
        o_ref[:, sl] = (o * lax.rsqrt(ms + EPS) * g_ref[...] * (1.0 - lam_init)).astype(o_ref.dtype)


def _attention(q, row_keys, keys_t, values, lam_vecs, subln_g, *, batch, lq, tq, q_row0, heads_per_step, lam_init):
    nq = lq // tq
    q0 = q_row0 // tq
    hw = 2 * DA_HD
    bw = heads_per_step * hw
    seq_block = lambda a: pl.BlockSpec((a.shape[0] // batch, bw), lambda b, h, i: (b, h))
    return pl.pallas_call(
        functools.partial(_attn_kernel, n_row_keys=len(row_keys), n_values=len(values), lam_init=lam_init),
        grid=(batch, DA_HEADS // heads_per_step, nq),
        in_specs=[_const_spec((4, DA_HD)),
                  _const_spec((1, hw)),
                  pl.BlockSpec((tq, bw), lambda b, h, i: (q0 + b * nq + i, h))]
                 + [seq_block(a) for a in row_keys]
                 + [pl.BlockSpec((None, heads_per_step, hw, keys_t.shape[-1]), lambda b, h, i: (b, h, 0, 0))]
                 + [seq_block(a) for a in values],
        out_specs=pl.BlockSpec((tq, bw), lambda b, h, i: (b * nq + i, h)),
        out_shape=jax.ShapeDtypeStruct((batch * lq, DA_V_W), BF16),
        compiler_params=_cparams("parallel", "parallel", "parallel"),
        name="diff_attn",
    )(lam_vecs, subln_g, q, *row_keys, keys_t, *values)


assert max(POOL_WINDOWS) // 2 <= SUBLANES


def _pooled(ext, pos_ext, seq_len, w_ref, sc_ref, tm):
    n = ext.shape[0]
    masks = {}

    def shifted(a, k):
        if k not in masks:
            masks[k] = (pos_ext + k < seq_len) if k > 0 else (pos_ext + k >= 0)
        return jnp.where(masks[k], pltpu.roll(a, (-k) % n, axis=0), 0.0)

    pos = pos_ext[0:tm]
    out = []
    for gi, win in enumerate(POOL_WINDOWS):
        half = win // 2
        sl = slice(gi * POOL_GC, (gi + 1) * POOL_GC)
        xg = ext[:, sl]
        right, left, step = xg, xg, 1
        while step < half:
            right, left, step = right + shifted(right, step), left + shifted(left, -step), 2 * step
        acc = (right + shifted(left, -1))[0:tm]
        cnt = (jnp.minimum(pos + half, seq_len) - jnp.maximum(pos - half, 0)).astype(F32)
        pooled = acc / cnt - xg[0:tm]
        y = jnp.dot(pooled.astype(BF16), w_ref[gi], preferred_element_type=F32)
        out.append((y * sc_ref[:, sl]).astype(BF16))
    return jnp.concatenate(out, axis=1)


def _mix_out_kernel(mod_ref, xp_ref, xs_ref, op_ref, os_ref, pool_ref, poolp_ref, pooln_ref,
                    pw_ref, ps_ref, wa_ref, wb_ref, o_ref):
    i = pl.program_id(0)
    prompt = i < ROWS_P // TM
    seq_len = jnp.where(prompt, SEQ, DEC_SEQ)
    ext = jnp.concatenate([pool_ref[...], pooln_ref[...], poolp_ref[...]], axis=0)
    n = ext.shape[0]
    r = lax.broadcasted_iota(jnp.int32, (n, POOL_GC), 0)
    r = jnp.where(r < TM + SUBLANES, r, r - n)
    pooled = _pooled(ext, (i * TM + r) & (seq_len - 1), seq_len, pw_ref, ps_ref, TM)
    att = _group_rows(i, TM, op_ref, os_ref)
    acc = (jnp.dot(att, wa_ref[...], preferred_element_type=F32)
           + jnp.dot(pooled, wb_ref[...], preferred_element_type=F32))
    o_ref[...] = _group_rows(i, TM, xp_ref, xs_ref) + mod_ref[0, 2:3] * acc


def _mix_out(x, mods, att, xpool, pool_w, pool_scale, w_att, w_pool):
    prev, nxt = _halo_specs(TM, POOL_W)
    return pl.pallas_call(
        _mix_out_kernel,
        grid=(ROWS // TM,),
        in_specs=[pl.BlockSpec((1, 6, D_MODEL), lambda i: (_mod_index(i, TM), 0, 0)),
                  *_group_specs(TM, D_MODEL), *_group_specs(TM, DA_V_W),
                  pl.BlockSpec((TM, POOL_W), lambda i: (i, 0)), prev, nxt,
                  _const_spec(pool_w.shape), _const_spec(pool_scale.shape),
                  _const_spec(w_att.shape), _const_spec(w_pool.shape)],
        out_specs=pl.BlockSpec((TM, D_MODEL), lambda i: (i, 0)),
        out_shape=jax.ShapeDtypeStruct((ROWS, D_MODEL), F32),
        compiler_params=_cparams("parallel"),
        name="mix_out",
    )(mods, *x, *att, xpool, xpool, xpool, pool_w, pool_scale, w_att, w_pool)


def _proj_res_kernel(*refs, gate_row, grouped):
    i = pl.program_id(0)
    mod_ref, o_ref = refs[0], refs[-1]
    pos, rows = 1, []
    for is_pair in grouped:
        rows.append(_group_rows(i, TM, refs[pos], refs[pos + 1]) if is_pair else refs[pos][...])
        pos += 2 if is_pair else 1
    acc = None
    for a, w_ref in zip(rows[1:], refs[pos:-1]):
        d = jnp.dot(a.astype(BF16), w_ref[...], preferred_element_type=F32)
        acc = d if acc is None else acc + d
    o_ref[...] = rows[0] + mod_ref[0, gate_row:gate_row + 1] * acc


def _proj_res(x, mods, acts, ws, *, gate_row):
    operands, specs, grouped = [], [], []
    for a in (x, *acts):
        is_pair = isinstance(a, tuple)
        grouped.append(is_pair)
        if is_pair:
            operands += list(a)
            specs += list(_group_specs(TM, a[0].shape[1]))
        else:
            operands.append(a)
            specs.append(pl.BlockSpec((TM, a.shape[1]), lambda i: (i, 0)))
    return pl.pallas_call(
        functools.partial(_proj_res_kernel, gate_row=gate_row, grouped=tuple(grouped)),
        grid=(ROWS // TM,),
        in_specs=[pl.BlockSpec((1, 6, D_MODEL), lambda i: (_mod_index(i, TM), 0, 0))]
                 + specs + [_const_spec(w.shape) for w in ws],
        out_specs=pl.BlockSpec((TM, D_MODEL), lambda i: (i, 0)),
        out_shape=jax.ShapeDtypeStruct((ROWS, D_MODEL), F32),
        compiler_params=_cparams("parallel"),
        name="proj_res",
    )(mods, *operands, *ws)


def _ffn_kernel(x_ref, xp_ref, xn_ref, mod_ref, g_ref, wup_ref, cw_ref, cb_ref, wd_ref, fg_ref,
                *rest, last_layer):
    h_ref, act_ref = rest[-2:]
    i = pl.program_id(0)
    tm = TM_CONV
    modulate = functools.partial(_modulate, g=g_ref[...], shift=mod_ref[0, 3:4], scale=mod_ref[0, 4:5])
    _fill_with_halo(h_ref, x_ref, xp_ref, xn_ref, modulate, i, tm)
    h = h_ref[...]
    prompt = i < ROWS_P // tm
    for c in range(D_FF // TF):
        gs, vs = slice(c * TF, (c + 1) * TF), slice(D_FF + c * TF, D_FF + (c + 1) * TF)
        ug = jnp.dot(h, wup_ref[:, gs], preferred_element_type=F32)
        uv = jnp.dot(h, wup_ref[:, vs], preferred_element_type=F32)
        cg = _conv3(ug, cw_ref[:, gs], cb_ref[:, gs], tm, prompt)
        cv = _conv3(uv, cw_ref[:, vs], cb_ref[:, vs], tm, prompt)
        act_ref[:, gs] = (_silu(cg) * cv).astype(BF16)
    y = x_ref[...] + mod_ref[0, 5:6] * jnp.dot(act_ref[...], wd_ref[...], preferred_element_type=F32)
    if not last_layer:
        rest[0][...] = y
        return
    ms = jnp.mean(y * y, axis=-1, keepdims=True)
    y = y * lax.rsqrt(ms + EPS) * fg_ref[...]
    op_ref, os_ref = rest[:2]

    @pl.when(prompt)
    def _():
        op_ref[...] = y

    @pl.when(jnp.logical_not(prompt))
    def _():
        os_ref[...] = y


def _conv_ffn(x, mods, g, w_up, conv_w, conv_b, w_down, final_g, *, layer, last_layer):
    tm = TM_CONV
    prev, nxt = _halo_specs(tm, D_MODEL)
    if last_layer:
        out_specs = list(_group_specs(tm, D_MODEL))
        out_shape = [jax.ShapeDtypeStruct((ROWS_P, D_MODEL), F32), jax.ShapeDtypeStruct((ROWS_S, D_MODEL), F32)]
    else:
        out_specs = pl.BlockSpec((tm, D_MODEL), lambda i: (i, 0))
        out_shape = jax.ShapeDtypeStruct((ROWS, D_MODEL), F32)
    return pl.pallas_call(
        functools.partial(_ffn_kernel, last_layer=last_layer),
        grid=(ROWS // tm,),
        in_specs=[pl.BlockSpec((tm, D_MODEL), lambda i: (i, 0)), prev, nxt,
                  pl.BlockSpec((1, 6, D_MODEL), lambda i: (_mod_index(i, tm), 0, 0)),
                  _const_spec((1, D_MODEL)),
                  _layer_spec(w_up.shape, layer), _const_spec(conv_w.shape), _const_spec(conv_b.shape),
                  _layer_spec(w_down.shape, layer), _const_spec((1, D_MODEL))],
        out_specs=out_specs,
        out_shape=out_shape,
        scratch_shapes=[pltpu.VMEM((tm + HALO, D_MODEL), BF16), pltpu.VMEM((tm, D_FF), BF16)],
        compiler_params=_cparams("arbitrary"),
        name="conv_ffn",
    )(x, x, x, mods, g, w_up, conv_w, conv_b, w_down, final_g)


XBC_COLS = 256


def _ssd_xbc_kernel(x_ref, xp_ref, xn_ref, mod_ref, g_ref, wx_ref, cw_ref, cb_ref, xbcp_ref, xbcs_ref, h_ref):
    i = pl.program_id(0)
    tm = TM_CONV
    modulate = functools.partial(_modulate, g=g_ref[...], shift=mod_ref[0, 0:1], scale=mod_ref[0, 1:2])
    _fill_with_halo(h_ref, x_ref, xp_ref, xn_ref, modulate, i, tm)

    def project(xbc_ref, is_prompt):
        h = h_ref[...]
        for c in range(SSD_XBC_W // XBC_COLS):
            sl = slice(c * XBC_COLS, (c + 1) * XBC_COLS)
            u = jnp.dot(h, wx_ref[:, sl], preferred_element_type=F32)
            xbc_ref[:, sl] = _silu(_conv3(u, cw_ref[:, sl], cb_ref[:, sl], tm, is_prompt))

    prompt = i < ROWS_P // tm
    pl.when(prompt)(lambda: project(xbcp_ref, True))
    pl.when(jnp.logical_not(prompt))(lambda: project(xbcs_ref, False))


def _ssd_zdt_kernel(x_ref, mod_ref, g_ref, wz_ref, wdt_ref, zp_ref, zs_ref, dtp_ref, dts_ref):
    i = pl.program_id(0)
    h = _modulate(x_ref[...], g_ref[...], mod_ref[0, 0:1], mod_ref[0, 1:2]).astype(BF16)

    def project(z_ref, dt_ref):
        z_ref[...] = _silu(jnp.dot(h, wz_ref[...], preferred_element_type=F32))
        dt_ref[...] = jnp.dot(h, wdt_ref[...], preferred_element_type=F32)

    prompt = i < ROWS_P // TM
    pl.when(prompt)(lambda: project(zp_ref, dtp_ref))
    pl.when(jnp.logical_not(prompt))(lambda: project(zs_ref, dts_ref))


def _ssd_in(x, mods, g, wz, wx, wdt, conv_w, conv_b):
    tm = TM_CONV
    prev, nxt = _halo_specs(tm, D_MODEL)
    mod = lambda t: pl.BlockSpec((1, 6, D_MODEL), lambda i: (_mod_index(i, t), 0, 0))
    group_shapes = lambda w: [jax.ShapeDtypeStruct((r, w), F32) for r in (ROWS_P, ROWS_S)]
    xbc = pl.pallas_call(
        _ssd_xbc_kernel,
        grid=(ROWS // tm,),
        in_specs=[pl.BlockSpec((tm, D_MODEL), lambda i: (i, 0)), prev, nxt, mod(tm), _const_spec((1, D_MODEL)),
                  _const_spec(wx.shape), _const_spec(conv_w.shape), _const_spec(conv_b.shape)],
        out_specs=list(_group_specs(tm, SSD_XBC_W)),
        out_shape=group_shapes(SSD_XBC_W),
        scratch_shapes=[pltpu.VMEM((tm + HALO, D_MODEL), BF16)],
        compiler_params=_cparams("arbitrary"),
        name="ssd_xbc",
    )(x, x, x, mods, g, wx, conv_w, conv_b)
    zp, zs, dtp, dts = pl.pallas_call(
        _ssd_zdt_kernel,
        grid=(ROWS // TM,),
        in_specs=[pl.BlockSpec((TM, D_MODEL), lambda i: (i, 0)), mod(TM), _const_spec((1, D_MODEL)),
                  _const_spec(wz.shape), _const_spec(wdt.shape)],
        out_specs=[*_group_specs(TM, SSD_DI), *_group_specs(TM, 2 * SSD_HEADS)],
        out_shape=group_shapes(SSD_DI) + group_shapes(2 * SSD_HEADS),
        compiler_params=_cparams("arbitrary"),
        name="ssd_zdt",
    )(x, mods, g, wz, wdt)
    return (zp, zs), tuple(xbc), (dtp, dts)


STREAMS = 2


def _split3(v):
    hi = v.astype(BF16)
    r1 = v - hi.astype(F32)
    mid = r1.astype(BF16)
    return hi, mid, (r1 - mid.astype(F32)).astype(BF16)


def _ssd_kernel(*refs, reverse, final, has_init, emit_state):
    refs = list(refs)
    xbc_ref, dt_ref, dtb_ref, alog_ref = refs[:4]
    del refs[:4]
    init_ref = refs.pop(0) if has_init else None
    wide_ref = refs.pop(0)
    if final:
        yin_ref, z_ref, dexp_ref, ng_ref = refs[:4]
        del refs[:4]
    y_ref = refs.pop(0)
    sfin_ref = refs.pop(0) if emit_state else None
    s_ref = refs.pop(0)
    yacc_ref = refs.pop(0) if final else None
    pos = pl.program_id(1)
    q = SSD_CHUNK
    hh = SSD_HEADS
    gw = SSD_DI // SSD_GROUPS

    @pl.when(pos == 0)
    def _():
        s_ref[...] = init_ref[...] if has_init else jnp.zeros_like(s_ref)

    d0 = hh if reverse else 0
    a = -jnp.exp(alog_ref[:, d0:d0 + hh])
    row = lax.broadcasted_iota(jnp.int32, (q, q), 0)
    col = lax.broadcasted_iota(jnp.int32, (q, q), 1)
    tri = (row <= col) if reverse else (row >= col)
    lane_lo = col < SSD_HEADDIM
    dn_t = (((1,), (1,)), ((), ()))

    def chunk_factors(k):
        dt_in = dt_ref[k, :, d0:d0 + hh] + dtb_ref[:, d0:d0 + hh]
        dt = jnp.maximum(dt_in, 0.0) + jnp.log1p(jnp.exp(-jnp.abs(dt_in)))
        acs = jnp.dot(tri.astype(F32), dt * a, precision=HIGHEST, preferred_element_type=F32)
        a_end = acs[0:1] if reverse else acs[q - 1:q]
        factors = jnp.concatenate([jnp.exp(acs), jnp.exp(a_end - acs) * dt,
                                   jnp.broadcast_to(jnp.exp(a_end), (SUBLANES, hh))], axis=0)
        wide = jnp.dot(jnp.concatenate(_split3(factors), axis=1), wide_ref[...], preferred_element_type=F32)
        acs2 = acs * LOG2_E
        mt = jnp.concatenate([acs2 - jnp.log2(dt), jnp.zeros((q, LANES - hh), F32)], axis=1).T
        return wide[0:q], wide[q:2 * q], wide[2 * q:2 * q + 1], acs2, mt

    def group(k, g, eacs_w, wst_w, dec_w, acs2, mt):
        gs = slice(g * gw, (g + 1) * gw)
        b_g = xbc_ref[k, :, SSD_DI + g * SSD_STATE:SSD_DI + (g + 1) * SSD_STATE]
        c_g = xbc_ref[k, :, SSD_DI + SSD_BC_W + g * SSD_STATE:SSD_DI + SSD_BC_W + (g + 1) * SSD_STATE]
        cgb = c_g.astype(BF16)
        cb = lax.dot_general(cgb, b_g.astype(BF16), dn_t, preferred_element_type=F32)
        s_g = s_ref[k, :, gs]
        y_off = jnp.dot(cgb, s_g.astype(BF16), preferred_element_type=F32) * eacs_w[:, gs]
        x_g = xbc_ref[k, :, gs]
        s_new = jnp.dot(b_g.T.astype(BF16), (x_g * wst_w[:, gs]).astype(BF16), preferred_element_type=F32)
        s_ref[k, :, gs] = s_g * dec_w[:, gs] + s_new
        for p in range(gw // LANES):
            ps = slice(p * LANES, (p + 1) * LANES)
            sl = slice(g * gw + p * LANES, g * gw + (p + 1) * LANES)
            xp = x_g[:, ps]
            xpb = xp.astype(BF16)
            zero = jnp.zeros_like(xpb)
            rhs = jnp.concatenate([jnp.where(lane_lo, xpb, zero), jnp.where(lane_lo, zero, xpb)], axis=0)
            w_l = []
            for h in (2 * (g * 4 + p), 2 * (g * 4 + p) + 1):
                seg = jnp.broadcast_to(acs2[:, h:h + 1], (q, q)) - mt[h:h + 1]
                w_l.append((cb * jnp.exp2(jnp.where(tri, seg, -jnp.inf))).astype(BF16))
            yp = jnp.dot(jnp.concatenate(w_l, axis=1), rhs, preferred_element_type=F32) + y_off[:, ps]
            if final:
                yacc_ref[k, :, sl] = yp
            else:
                y_ref[k, :, sl] = yp

    per_stream = [chunk_factors(k) for k in range(STREAMS)]
    for g in range(SSD_GROUPS):
        for k in range(STREAMS):
            group(k, g, *per_stream[k])

    if final:
        for k in range(STREAMS):
            yt = (yacc_ref[k] + yin_ref[k] + xbc_ref[k, :, 0:SSD_DI] * dexp_ref[...]) * z_ref[k]
            ms = jnp.mean(yt * yt, axis=-1, keepdims=True)
            y_ref[k] = (yt * lax.rsqrt(ms + EPS) * ng_ref[...]).astype(y_ref.dtype)

    if emit_state:
        @pl.when(pos == pl.num_programs(1) - 1)
        def _():
            for k in range(STREAMS):
                sfin_ref[k] = s_ref[k].T


def _ssd_scan(xbc, dt, dt_bias, a_log, init, extra, *, reverse, nseq, emit_state):
    final = extra is not None
    has_init = init is not None
    assert nseq % STREAMS == 0 and (not has_init or nseq == STREAMS)
    nc = xbc.shape[0] // (nseq * SSD_CHUNK)
    npairs = nseq // STREAMS
    view = lambda a: a.reshape(npairs, STREAMS, nc, SSD_CHUNK, a.shape[-1])
    chunk = lambda width: pl.BlockSpec((None, STREAMS, None, SSD_CHUNK, width),
                                       lambda p, t: (p, 0, (nc - 1 - t) if reverse else t, 0, 0))
    widen = jnp.tile(jnp.repeat(jnp.eye(SSD_HEADS, dtype=BF16), SSD_HEADDIM, axis=1), (3, 1))
    in_specs = [chunk(SSD_XBC_W), chunk(2 * SSD_HEADS),
                _const_spec((1, 2 * SSD_HEADS)), _const_spec((1, 2 * SSD_HEADS))]
    args = [view(xbc), view(dt), dt_bias, a_log]
    if has_init:
        in_specs.append(_const_spec(init.shape))
        args.append(init)
    in_specs.append(_const_spec(widen.shape))
    args.append(widen)
    scratch = [pltpu.VMEM((STREAMS, SSD_STATE, SSD_DI), F32)]
    if final:
        y_other, z, d_exp, norm_g = extra
        in_specs += [chunk(SSD_DI), chunk(SSD_DI), _const_spec((1, SSD_DI)), _const_spec((1, SSD_DI))]
        args += [view(y_other), view(z), d_exp, norm_g]
        scratch.append(pltpu.VMEM((STREAMS, SSD_CHUNK, SSD_DI), F32))
    out_specs = [chunk(SSD_DI)]
    out_shape = [jax.ShapeDtypeStruct((npairs, STREAMS, nc, SSD_CHUNK, SSD_DI), BF16 if final else F32)]
    if emit_state:
        out_specs.append(pl.BlockSpec((None, STREAMS, SSD_DI, SSD_STATE), lambda p, t: (p, 0, 0, 0)))
        out_shape.append(jax.ShapeDtypeStruct((npairs, STREAMS, SSD_DI, SSD_STATE), F32))
    outs = pl.pallas_call(
        functools.partial(_ssd_kernel, reverse=reverse, final=final, has_init=has_init, emit_state=emit_state),
        grid=(npairs, nc),
        in_specs=in_specs,
        out_specs=out_specs,
        out_shape=out_shape,
        scratch_shapes=scratch,
        compiler_params=_cparams("arbitrary", "arbitrary"),
        name="ssd_scan_bwd" if reverse else "ssd_scan_fwd",
    )(*args)
    y = outs[0].reshape(xbc.shape[0], SSD_DI)
    return (y, outs[1].reshape(nseq, SSD_DI, SSD_STATE)) if emit_state else y


def kernel(x_prompt, x_sample, cache_k, cache_v, state_ssm_fwd, state_ssm_bwd, c, c_ctx, ada_w, ada_b, norm_mix_g, norm_ffn_g, att_w_in, att_lambda, att_subln_g, pool_w, pool_scale, att_w_out, ssd_w_in, ssd_conv_w, ssd_conv_b, ssd_dt_bias, ssd_a_log, ssd_d, ssd_norm_g, ssd_w_out, ffn_w_up, ffn_conv_w, ffn_conv_b, ffn_w_down, final_norm_g):
    x = (x_prompt.reshape(ROWS_P, D_MODEL), x_sample.reshape(ROWS_S, D_MODEL))
    cvec = jnp.concatenate([c_ctx[None], c, jnp.zeros((MOD_PAD - N_MOD, D_MODEL), F32)], axis=0)
    mods_all = _adaln(cvec, ada_w, ada_b).reshape(DEPTH, MOD_PAD, 6, D_MODEL)
    cos, sin = _rope_tables()
    ffn_up, ffn_down = ffn_w_up.astype(BF16), ffn_w_down.astype(BF16)
    outs = {}
    for l in range(DEPTH):
        i = l // 2
        mods = mods_all[l, :N_MOD]
        if l % 2 == 0:
            lam_init = 0.8 - 0.6 * math.exp(-0.3 * l)
            if not isinstance(x, tuple):
                x = (x[:ROWS_P], x[ROWS_P:])
            q, kt_p, k_s, v_p, v_s, xpool = _att_in(*x, mods, norm_mix_g[l][None], att_w_in[i].astype(BF16), cos, sin)
            ckt = cache_k[:, i].transpose(0, 2, 3, 4, 1).reshape(DEC_BATCH, DA_HEADS, 2 * DA_HD, PAST_LEN)
            cv = cache_v[:, i].reshape(DEC_BATCH * PAST_LEN, DA_V_W)
            att = functools.partial(_attention, lam_vecs=att_lambda[i], subln_g=att_subln_g[i][None],
                                    lam_init=lam_init)
            o_p = att(q, [], kt_p, [v_p], batch=BATCH, lq=SEQ, tq=SEQ, q_row0=0, heads_per_step=DA_HEADS)
            o_s = att(q, [k_s], ckt, [v_s, cv], batch=DEC_BATCH, lq=DEC_SEQ, tq=TQ, q_row0=ROWS_P, heads_per_step=1)
            w_out = att_w_out[i].astype(BF16)
            x = _mix_out(x, mods, (o_p, o_s), xpool, pool_w[i].astype(BF16), pool_scale[i][None],
                         w_out[:DA_V_W], w_out[DA_V_W:])
            outs.setdefault("k", []).append(
                kt_p.reshape(BATCH, DA_HEADS, 2, DA_HD, SEQ).transpose(0, 4, 1, 2, 3))
            outs.setdefault("v", []).append(v_p.reshape(BATCH, SEQ, DA_HEADS, 2 * DA_HD))
        else:
            w_in = ssd_w_in[i]
            z, xbc, dt = _ssd_in(x, mods, norm_mix_g[l][None], w_in[:, :SSD_DI].astype(BF16),
                                 w_in[:, SSD_DI:SSD_DI + SSD_XBC_W].astype(BF16),
                                 w_in[:, SSD_DI + SSD_XBC_W:].astype(BF16),
                                 ssd_conv_w[i], ssd_conv_b[i][None])
            to_scan_layout = lambda s: s.transpose(0, 3, 1, 2).reshape(DEC_BATCH, SSD_STATE, SSD_DI)
            dtb, alog = ssd_dt_bias[i].reshape(1, -1), ssd_a_log[i].reshape(1, -1)
            d_exp, norm_g = jnp.repeat(ssd_d[i], SSD_HEADDIM)[None], ssd_norm_g[i][None]
            scan_p = functools.partial(_ssd_scan, xbc[0], dt[0], dtb, alog, None, nseq=BATCH, emit_state=True)
            y_f, s_f = scan_p(None, reverse=False)
            y_p, s_b = scan_p((y_f, z[0], d_exp, norm_g), reverse=True)
            scan_s = functools.partial(_ssd_scan, xbc[1], dt[1], dtb, alog, nseq=DEC_BATCH, emit_state=False)
            y_f = scan_s(to_scan_layout(state_ssm_fwd[:, i]), None, reverse=False)
            y_s = scan_s(to_scan_layout(state_ssm_bwd[:, i]), (y_f, z[1], d_exp, norm_g), reverse=True)
            x = _proj_res(x, mods, [(y_p, y_s)], [ssd_w_out[i].astype(BF16)], gate_row=2)
            outs.setdefault("sf", []).append(s_f.reshape(BATCH, SSD_HEADS, SSD_HEADDIM, SSD_STATE))
            outs.setdefault("sb", []).append(s_b.reshape(BATCH, SSD_HEADS, SSD_HEADDIM, SSD_STATE))
        x = _conv_ffn(x, mods, norm_ffn_g[l][None], ffn_up, ffn_conv_w[l], ffn_conv_b[l][None],
                      ffn_down, final_norm_g[None], layer=l, last_layer=(l == DEPTH - 1))
    y_prompt = x[0].reshape(BATCH, SEQ, D_MODEL)
    y_sample = x[1].reshape(DEC_BATCH, DEC_SEQ, D_MODEL)
    return (y_prompt, y_sample, jnp.stack(outs["k"], axis=1), jnp.stack(outs["v"], axis=1),
            jnp.stack(outs["sf"], axis=1), jnp.stack(outs["sb"], axis=1))
```

```python
import functools
import math

import jax
import jax.numpy as jnp
from jax import lax
from jax.experimental import pallas as pl
from jax.experimental.pallas import tpu as pltpu

F32 = jnp.float32
BF16 = jnp.bfloat16
HIGHEST = lax.Precision.HIGHEST

D_MODEL = 1024
BATCH = 32
SEQ = 256
DEPTH = 2
DEC_BATCH = 2
DEC_SEQ = 2048
PAST_LEN = 512
GRID_W = 64
EPS = 1e-6
DA_HEADS = 4
DA_HD = 64
DA_QK_W = DA_HEADS * 2 * DA_HD
DA_V_W = DA_HEADS * 2 * DA_HD
POOL_W = D_MODEL - DA_V_W
POOL_WINDOWS = (2, 4, 8, 16)
POOL_GROUPS = 4
POOL_GC = POOL_W // POOL_GROUPS
ATT_IN_W = 2 * DA_QK_W + DA_V_W + POOL_W
LOG2_E = math.log2(math.e)
ROPE_THETA = 10000.0
ROPE_NF = DA_HD // 4
SSD_DI = 2 * D_MODEL
SSD_HEADDIM = 64
SSD_HEADS = SSD_DI // SSD_HEADDIM
SSD_GROUPS = 4
SSD_STATE = 128
SSD_CHUNK = 128
SSD_BC_W = SSD_GROUPS * SSD_STATE
SSD_XBC_W = SSD_DI + 2 * SSD_BC_W
D_FF = 2816

ROWS_P = BATCH * SEQ
ROWS_S = DEC_BATCH * DEC_SEQ
ROWS = ROWS_P + ROWS_S
N_MOD = 1 + DEC_BATCH
MOD_PAD = 8

LANES = 128
SUBLANES = 8
HALO = 2 * SUBLANES
VMEM_LIMIT = 56 * 1024 * 1024

TM = 512
TM_CONV = 512
TF = 256
TQ = 256


def _cparams(*sem):
    return pltpu.CompilerParams(dimension_semantics=sem, vmem_limit_bytes=VMEM_LIMIT)


def _mod_index(i, tm):
    n_p, per_seq = ROWS_P // tm, DEC_SEQ // tm
    return jnp.where(i < n_p, 0, 1 + (i - n_p) // per_seq)


def _silu(x):
    return x / (1.0 + jnp.exp2(x * -LOG2_E))


def _modulate(x, g, shift, scale):
    ms = jnp.mean(x * x, axis=-1, keepdims=True)
    y = x * lax.rsqrt(ms + EPS) * g
    return y * (1.0 + scale) + shift


def _const_spec(shape):
    nd = len(shape)
    return pl.BlockSpec(shape, lambda *_: (0,) * nd, pipeline_mode=pl.Buffered(1))


def _layer_spec(stacked_shape, layer):
    nd = len(stacked_shape) - 1
    return pl.BlockSpec((None, *stacked_shape[1:]), lambda *_: (layer,) + (0,) * nd, pipeline_mode=pl.Buffered(1))


def _group_specs(tm, width):
    n_p = ROWS_P // tm
    return (pl.BlockSpec((tm, width), lambda i: (jnp.minimum(i, n_p - 1), 0)),
            pl.BlockSpec((tm, width), lambda i: (jnp.maximum(i - n_p, 0), 0)))


def _group_rows(i, tm, p_ref, s_ref):
    return jnp.where(i < ROWS_P // tm, p_ref[...], s_ref[...])


def _halo_specs(tm, width):
    per = tm // SUBLANES
    last = ROWS // SUBLANES - 1
    prev = pl.BlockSpec((SUBLANES, width), lambda i: (jnp.maximum(i * per - 1, 0), 0))
    nxt = pl.BlockSpec((SUBLANES, width), lambda i: (jnp.minimum((i + 1) * per, last), 0))
    return prev, nxt


def _fill_with_halo(h_ref, x_ref, xp_ref, xn_ref, modulate, i, tm):
    m = jnp.where(i < ROWS_P // tm, SEQ - 1, DEC_SEQ - 1)
    starts = ((i * tm) & m) == 0
    ends = (((i + 1) * tm) & m) == 0
    h_ref[0:tm] = modulate(x_ref[...]).astype(BF16)
    halo = jnp.concatenate([jnp.where(ends, 0.0, modulate(xn_ref[...])),
                            jnp.where(starts, 0.0, modulate(xp_ref[...]))], axis=0)
    h_ref[tm:tm + HALO] = halo.astype(BF16)


def _conv3(u, cw, cb, tm, prompt):
    n = u.shape[0]
    up = pltpu.roll(u, 1, axis=0)[0:tm]
    un = pltpu.roll(u, n - 1, axis=0)[0:tm]
    if tm > SEQ and prompt is not False:
        s = SUBLANES
        sub = lax.broadcasted_iota(jnp.int32, (s, u.shape[1]), 0)
        at_edge = (lambda m: m) if prompt is True else (lambda m: prompt & m)
        ups, uns, lo = [], [], 0
        for edge in range(SEQ, tm, SEQ):
            ups += [up[lo:edge], jnp.where(at_edge(sub == 0), 0.0, up[edge:edge + s])]
            uns += [un[lo - s if lo else 0:edge - s], jnp.where(at_edge(sub == s - 1), 0.0, un[edge - s:edge])]
            lo = edge + s
        up = jnp.concatenate(ups + [up[lo:tm]], axis=0)
        un = jnp.concatenate(uns + [un[lo - s:tm]], axis=0)
    return cw[1:2] * u[0:tm] + cb + cw[0:1] * up + cw[2:3] * un


def _adaln_kernel(c_ref, w_ref, b_ref, o_ref):
    s = _silu(c_ref[...])
    o_ref[0] = jnp.dot(s, w_ref[0], precision=HIGHEST, preferred_element_type=F32) + b_ref[0]


def _adaln(cvec, ada_w, ada_b):
    tn = 1536
    n = 6 * D_MODEL
    return pl.pallas_call(
        _adaln_kernel,
        grid=(DEPTH, n // tn),
        in_specs=[_const_spec((MOD_PAD, D_MODEL)),
                  pl.BlockSpec((1, D_MODEL, tn), lambda l, j: (l, 0, j)),
                  pl.BlockSpec((1, 1, tn), lambda l, j: (l, 0, j))],
        out_specs=pl.BlockSpec((1, MOD_PAD, tn), lambda l, j: (l, 0, j)),
        out_shape=jax.ShapeDtypeStruct((DEPTH, MOD_PAD, n), F32),
        compiler_params=_cparams("parallel", "parallel"),
        name="adaln",
    )(cvec, ada_w, ada_b.reshape(DEPTH, 1, n))


def _rope(x, cos, sin_signed):
    lane = lax.broadcasted_iota(jnp.int32, cos.shape, 1)
    lower = (lane & 31) < ROPE_NF
    out = []
    for s in range(x.shape[1] // LANES):
        xs = x[:, s * LANES:(s + 1) * LANES]
        partner = jnp.where(lower, pltpu.roll(xs, LANES - ROPE_NF, axis=1), pltpu.roll(xs, ROPE_NF, axis=1))
        out.append(xs * cos + partner * sin_signed)
    return jnp.concatenate(out, axis=1)


def _att_in_kernel(xp_ref, xs_ref, mod_ref, g_ref, w_ref, cos_ref, sin_ref,
                   q_ref, kpt_ref, ks_ref, vp_ref, vs_ref, p_ref):
    i = pl.program_id(0)
    x = _group_rows(i, TM, xp_ref, xs_ref)
    h = _modulate(x, g_ref[...], mod_ref[0, 0:1], mod_ref[0, 1:2]).astype(BF16)
    proj = jnp.dot(h, w_ref[...], preferred_element_type=F32)
    q = proj[:, :DA_QK_W]
    k = proj[:, DA_QK_W:2 * DA_QK_W]
    v = proj[:, 2 * DA_QK_W:2 * DA_QK_W + DA_V_W]
    p_ref[...] = proj[:, 2 * DA_QK_W + DA_V_W:]
    latent = i >= ROWS_P // TM

    @pl.when(latent)
    def _():
        q_ref[...] = _rope(q, cos_ref[...], sin_ref[...])
        ks_ref[...] = _rope(k, cos_ref[...], sin_ref[...])
        vs_ref[...] = v

    @pl.when(jnp.logical_not(latent))
    def _():
        q_ref[...] = q
        vp_ref[...] = v
        hw = 2 * DA_HD
        for s in range(TM // SEQ):
            for hd in range(DA_HEADS):
                kpt_ref[s, hd] = k[s * SEQ:(s + 1) * SEQ, hd * hw:(hd + 1) * hw].T


def _rope_tables():
    t = jnp.arange(DEC_SEQ, dtype=F32)
    r, col = jnp.floor(t / GRID_W), t % GRID_W
    inv = ROPE_THETA ** (-jnp.arange(ROPE_NF, dtype=F32) / ROPE_NF)
    ar, ac = r[:, None] * inv, col[:, None] * inv
    cos = jnp.concatenate([jnp.cos(ar), jnp.cos(ar), jnp.cos(ac), jnp.cos(ac)], axis=1)
    sin = jnp.concatenate([-jnp.sin(ar), jnp.sin(ar), -jnp.sin(ac), jnp.sin(ac)], axis=1)
    return jnp.tile(cos, (1, 2)), jnp.tile(sin, (1, 2))


def _att_in(xp, xs, mods, g, w, cos, sin):
    n_p, per_seq = ROWS_P // TM, DEC_SEQ // TM
    tab = pl.BlockSpec((TM, LANES), lambda i: (jnp.maximum(i - n_p, 0) % per_seq, 0))
    out = pl.BlockSpec((TM, DA_QK_W), lambda i: (i, 0))
    out_p, out_s = _group_specs(TM, DA_QK_W)
    full, grp_p, grp_s = (jax.ShapeDtypeStruct((r, DA_QK_W), F32) for r in (ROWS, ROWS_P, ROWS_S))
    seqs = TM // SEQ
    out_kt = pl.BlockSpec((seqs, DA_HEADS, 2 * DA_HD, SEQ), lambda i: (jnp.minimum(i, n_p - 1), 0, 0, 0))
    kt = jax.ShapeDtypeStruct((BATCH, DA_HEADS, 2 * DA_HD, SEQ), F32)
    return pl.pallas_call(
        _att_in_kernel,
        grid=(ROWS // TM,),
        in_specs=[*_group_specs(TM, D_MODEL),
                  pl.BlockSpec((1, 6, D_MODEL), lambda i: (_mod_index(i, TM), 0, 0)),
                  _const_spec((1, D_MODEL)),
                  _const_spec((D_MODEL, ATT_IN_W)),
                  tab, tab],
        out_specs=[out, out_kt, out_s, out_p, out_s, out],
        out_shape=[full, kt, grp_s, grp_p, grp_s, full],
        compiler_params=_cparams("arbitrary"),
        name="att_in",
    )(xp, xs, mods, g, w, cos, sin)


def _attn_kernel(lam_ref, g_ref, q_ref, *rest, n_row_keys, n_values, lam_init):
    k_refs, kt_ref = rest[:n_row_keys], rest[n_row_keys]
    v_refs, o_ref = rest[n_row_keys + 1:n_row_keys + 1 + n_values], rest[-1]
    lv = lam_ref[...]
    lam = (jnp.exp(jnp.sum(lv[0:1] * lv[1:2], keepdims=True))
           - jnp.exp(jnp.sum(lv[2:3] * lv[3:4], keepdims=True)) + lam_init)
    dn = (((1,), (1,)), ((), ()))
    hw = 2 * DA_HD
    lane = lax.broadcasted_iota(jnp.int32, (q_ref.shape[0], hw), 1)
    for h in range(q_ref.shape[1] // hw):
        sl = slice(h * hw, (h + 1) * hw)
        q = q_ref[:, sl] * (DA_HD ** -0.5 * LOG2_E)
        kbs = [k_ref[:, sl].astype(BF16) for k_ref in k_refs]
        ktb = kt_ref[h].astype(BF16)
        vb = jnp.concatenate([v_ref[:, sl].astype(BF16) for v_ref in v_refs], axis=0)

        def unnormalised(qc):
            qb = qc.astype(BF16)
            s = jnp.concatenate([lax.dot_general(qb, kb, dn, preferred_element_type=F32) for kb in kbs]
                                + [jnp.dot(qb, ktb, preferred_element_type=F32)], axis=1)
            e = jnp.exp2(s - jnp.max(s, axis=-1, keepdims=True))
            return (jnp.dot(e.astype(BF16), vb, preferred_element_type=F32),
                    jnp.sum(e, axis=-1, keepdims=True))

        o1, l1 = unnormalised(jnp.where(lane < DA_HD, q, 0.0))
        o2, l2 = unnormalised(jnp.where(lane < DA_HD, 0.0, q))
        o = o1 / l1 - o2 * (lam / l2)
        ms = jnp.mean(o * o, axis=-1, keepdims=True)
        o_ref[:, sl] = (o * lax.rsqrt(ms + EPS) * g_ref[...] * (1.0 - lam_init)).astype(o_ref.dtype)


def _attention(q, row_keys, keys_t, values, lam_vecs, subln_g, *, batch, lq, tq, q_row0, heads_per_step, lam_init):
    nq = lq // tq
    q0 = q_row0 // tq
    hw = 2 * DA_HD
    bw = heads_per_step * hw
    seq_block = lambda a: pl.BlockSpec((a.shape[0] // batch, bw), lambda b, h, i: (b, h))
    return pl.pallas_call(
        functools.partial(_attn_kernel, n_row_keys=len(row_keys), n_values=len(values), lam_init=lam_init),
        grid=(batch, DA_HEADS // heads_per_step, nq),
        in_specs=[_const_spec((4, DA_HD)),
                  _const_spec((1, hw)),
                  pl.BlockSpec((tq, bw), lambda b, h, i: (q0 + b * nq + i, h))]
                 + [seq_block(a) for a in row_keys]
                 + [pl.BlockSpec((None, heads_per_step, hw, keys_t.shape[-1]), lambda b, h, i: (b, h, 0, 0))]
                 + [seq_block(a) for a in values],
        out_specs=pl.BlockSpec((tq, bw), lambda b, h, i: (b * nq + i, h)),
        out_shape=jax.ShapeDtypeStruct((batch * lq, DA_V_W), BF16),
        compiler_params=_cparams("parallel", "parallel", "parallel"),
        name="diff_attn",
    )(lam_vecs, subln_g, q, *row_keys, keys_t, *values)


assert max(POOL_WINDOWS) // 2 <= SUBLANES


def _pooled(ext, pos_ext, seq_len, w_ref, sc_ref, tm):
    n = ext.shape[0]
    masks = {}

    def shifted(a, k):
        if k not in masks:
            masks[k] = (pos_ext + k < seq_len) if k > 0 else (pos_ext + k >= 0)
        return jnp.where(masks[k], pltpu.roll(a, (-k) % n, axis=0), 0.0)

    pos = pos_ext[0:tm]
    out = []
    for gi, win in enumerate(POOL_WINDOWS):
        half = win // 2
        sl = slice(gi * POOL_GC, (gi + 1) * POOL_GC)
        xg = ext[:, sl]
        right, left, step = xg, xg, 1
        while step < half:
            right, left, step = right + shifted(right, step), left + shifted(left, -step), 2 * step
        acc = (right + shifted(left, -1))[0:tm]
        cnt = (jnp.minimum(pos + half, seq_len) - jnp.maximum(pos - half, 0)).astype(F32)
        pooled = acc / cnt - xg[0:tm]
        y = jnp.dot(pooled.astype(BF16), w_ref[gi], preferred_element_type=F32)
        out.append((y * sc_ref[:, sl]).astype(BF16))
    return jnp.concatenate(out, axis=1)


def _mix_out_kernel(mod_ref, xp_ref, xs_ref, op_ref, os_ref, pool_ref, poolp_ref, pooln_ref,
                    pw_ref, ps_ref, wa_ref, wb_ref, o_ref):
    i = pl.program_id(0)
    prompt = i < ROWS_P // TM
    seq_len = jnp.where(prompt, SEQ, DEC_SEQ)
    ext = jnp.concatenate([pool_ref[...], pooln_ref[...], poolp_ref[...]], axis=0)
    n = ext.shape[0]
    r = lax.broadcasted_iota(jnp.int32, (n, POOL_GC), 0)
    r = jnp.where(r < TM + SUBLANES, r, r - n)
    pooled = _pooled(ext, (i * TM + r) & (seq_len - 1), seq_len, pw_ref, ps_ref, TM)
    att = _group_rows(i, TM, op_ref, os_ref)
    acc = (jnp.dot(att, wa_ref[...], preferred_element_type=F32)
           + jnp.dot(pooled, wb_ref[...], preferred_element_type=F32))
    o_ref[...] = _group_rows(i, TM, xp_ref, xs_ref) + mod_ref[0, 2:3] * acc


def _mix_out(x, mods, att, xpool, pool_w, pool_scale, w_att, w_pool):
    prev, nxt = _halo_specs(TM, POOL_W)
    return pl.pallas_call(
        _mix_out_kernel,
        grid=(ROWS // TM,),
        in_specs=[pl.BlockSpec((1, 6, D_MODEL), lambda i: (_mod_index(i, TM), 0, 0)),
                  *_group_specs(TM, D_MODEL), *_group_specs(TM, DA_V_W),
                  pl.BlockSpec((TM, POOL_W), lambda i: (i, 0)), prev, nxt,
                  _const_spec(pool_w.shape), _const_spec(pool_scale.shape),
                  _const_spec(w_att.shape), _const_spec(w_pool.shape)],
        out_specs=pl.BlockSpec((TM, D_MODEL), lambda i: (i, 0)),
        out_shape=jax.ShapeDtypeStruct((ROWS, D_MODEL), F32),
        compiler_params=_cparams("parallel"),
        name="mix_out",
    )(mods, *x, *att, xpool, xpool, xpool, pool_w, pool_scale, w_att, w_pool)


def _proj_res_kernel(*refs, gate_row, grouped):
    i = pl.program_id(0)
    mod_ref, o_ref = refs[0], refs[-1]
    pos, rows = 1, []
    for is_pair in grouped:
        rows.append(_group_rows(i, TM, refs[pos], refs[pos + 1]) if is_pair else refs[pos][...])
        pos += 2 if is_pair else 1
    acc = None
    for a, w_ref in zip(rows[1:], refs[pos:-1]):
        d = jnp.dot(a.astype(BF16), w_ref[...], preferred_element_type=F32)
        acc = d if acc is None else acc + d
    o_ref[...] = rows[0] + mod_ref[0, gate_row:gate_row + 1] * acc


def _proj_res(x, mods, acts, ws, *, gate_row):
    operands, specs, grouped = [], [], []
    for a in (x, *acts):
        is_pair = isinstance(a, tuple)
        grouped.append(is_pair)
        if is_pair:
            operands += list(a)
            specs += list(_group_specs(TM, a[0].shape[1]))
        else:
            operands.append(a)
            specs.append(pl.BlockSpec((TM, a.shape[1]), lambda i: (i, 0)))
    return pl.pallas_call(
        functools.partial(_proj_res_kernel, gate_row=gate_row, grouped=tuple(grouped)),
        grid=(ROWS // TM,),
        in_specs=[pl.BlockSpec((1, 6, D_MODEL), lambda i: (_mod_index(i, TM), 0, 0))]
                 + specs + [_const_spec(w.shape) for w in ws],
        out_specs=pl.BlockSpec((TM, D_MODEL), lambda i: (i, 0)),
        out_shape=jax.ShapeDtypeStruct((ROWS, D_MODEL), F32),
        compiler_params=_cparams("parallel"),
        name="proj_res",
    )(mods, *operands, *ws)


def _ffn_kernel(x_ref, xp_ref, xn_ref, mod_ref, g_ref, wup_ref, cw_ref, cb_ref, wd_ref, fg_ref,
                *rest, last_layer):
    h_ref, act_ref = rest[-2:]
    i = pl.program_id(0)
    tm = TM_CONV
    modulate = functools.partial(_modulate, g=g_ref[...], shift=mod_ref[0, 3:4], scale=mod_ref[0, 4:5])
    _fill_with_halo(h_ref, x_ref, xp_ref, xn_ref, modulate, i, tm)
    h = h_ref[...]
    prompt = i < ROWS_P // tm
    for c in range(D_FF // TF):
        gs, vs = slice(c * TF, (c + 1) * TF), slice(D_FF + c * TF, D_FF + (c + 1) * TF)
        ug = jnp.dot(h, wup_ref[:, gs], preferred_element_type=F32)
        uv = jnp.dot(h, wup_ref[:, vs], preferred_element_type=F32)
        cg = _conv3(ug, cw_ref[:, gs], cb_ref[:, gs], tm, prompt)
        cv = _conv3(uv, cw_ref[:, vs], cb_ref[:, vs], tm, prompt)
        act_ref[:, gs] = (_silu(cg) * cv).astype(BF16)
    y = x_ref[...] + mod_ref[0, 5:6] * jnp.dot(act_ref[...], wd_ref[...], preferred_element_type=F32)
    if not last_layer:
        rest[0][...] = y
        return
    ms = jnp.mean(y * y, axis=-1, keepdims=True)
    y = y * lax.rsqrt(ms + EPS) * fg_ref[...]
    op_ref, os_ref = rest[:2]

    @pl.when(prompt)
    def _():
        op_ref[...] = y

    @pl.when(jnp.logical_not(prompt))
    def _():
        os_ref[...] = y


def _conv_ffn(x, mods, g, w_up, conv_w, conv_b, w_down, final_g, *, layer, last_layer):
    tm = TM_CONV
    prev, nxt = _halo_specs(tm, D_MODEL)
    if last_layer:
        out_specs = list(_group_specs(tm, D_MODEL))
        out_shape = [jax.ShapeDtypeStruct((ROWS_P, D_MODEL), F32), jax.ShapeDtypeStruct((ROWS_S, D_MODEL), F32)]
    else:
        out_specs = pl.BlockSpec((tm, D_MODEL), lambda i: (i, 0))
        out_shape = jax.ShapeDtypeStruct((ROWS, D_MODEL), F32)
    return pl.pallas_call(
        functools.partial(_ffn_kernel, last_layer=last_layer),
        grid=(ROWS // tm,),
        in_specs=[pl.BlockSpec((tm, D_MODEL), lambda i: (i, 0)), prev, nxt,
                  pl.BlockSpec((1, 6, D_MODEL), lambda i: (_mod_index(i, tm), 0, 0)),
                  _const_spec((1, D_MODEL)),
                  _layer_spec(w_up.shape, layer), _const_spec(conv_w.shape), _const_spec(conv_b.shape),
                  _layer_spec(w_down.shape, layer), _const_spec((1, D_MODEL))],
        out_specs=out_specs,
        out_shape=out_shape,
        scratch_shapes=[pltpu.VMEM((tm + HALO, D_MODEL), BF16), pltpu.VMEM((tm, D_FF), BF16)],
        compiler_params=_cparams("arbitrary"),
        name="conv_ffn",
    )(x, x, x, mods, g, w_up, conv_w, conv_b, w_down, final_g)


XBC_COLS = 256


def _ssd_xbc_kernel(x_ref, xp_ref, xn_ref, mod_ref, g_ref, wx_ref, cw_ref, cb_ref, xbcp_ref, xbcs_ref, h_ref):
    i = pl.program_id(0)
    tm = TM_CONV
    modulate = functools.partial(_modulate, g=g_ref[...], shift=mod_ref[0, 0:1], scale=mod_ref[0, 1:2])
    _fill_with_halo(h_ref, x_ref, xp_ref, xn_ref, modulate, i, tm)

    def project(xbc_ref, is_prompt):
        h = h_ref[...]
        for c in range(SSD_XBC_W // XBC_COLS):
            sl = slice(c * XBC_COLS, (c + 1) * XBC_COLS)
            u = jnp.dot(h, wx_ref[:, sl], preferred_element_type=F32)
            xbc_ref[:, sl] = _silu(_conv3(u, cw_ref[:, sl], cb_ref[:, sl], tm, is_prompt))

    prompt = i < ROWS_P // tm
    pl.when(prompt)(lambda: project(xbcp_ref, True))
    pl.when(jnp.logical_not(prompt))(lambda: project(xbcs_ref, False))


def _ssd_zdt_kernel(x_ref, mod_ref, g_ref, wz_ref, wdt_ref, zp_ref, zs_ref, dtp_ref, dts_ref):
    i = pl.program_id(0)
    h = _modulate(x_ref[...], g_ref[...], mod_ref[0, 0:1], mod_ref[0, 1:2]).astype(BF16)

    def project(z_ref, dt_ref):
        z_ref[...] = _silu(jnp.dot(h, wz_ref[...], preferred_element_type=F32))
        dt_ref[...] = jnp.dot(h, wdt_ref[...], preferred_element_type=F32)

    prompt = i < ROWS_P // TM
    pl.when(prompt)(lambda: project(zp_ref, dtp_ref))
    pl.when(jnp.logical_not(prompt))(lambda: project(zs_ref, dts_ref))


def _ssd_in(x, mods, g, wz, wx, wdt, conv_w, conv_b):
    tm = TM_CONV
    prev, nxt = _halo_specs(tm, D_MODEL)
    mod = lambda t: pl.BlockSpec((1, 6, D_MODEL), lambda i: (_mod_index(i, t), 0, 0))
    group_shapes = lambda w: [jax.ShapeDtypeStruct((r, w), F32) for r in (ROWS_P, ROWS_S)]
    xbc = pl.pallas_call(
        _ssd_xbc_kernel,
        grid=(ROWS // tm,),
        in_specs=[pl.BlockSpec((tm, D_MODEL), lambda i: (i, 0)), prev, nxt, mod(tm), _const_spec((1, D_MODEL)),
                  _const_spec(wx.shape), _const_spec(conv_w.shape), _const_spec(conv_b.shape)],
        out_specs=list(_group_specs(tm, SSD_XBC_W)),
        out_shape=group_shapes(SSD_XBC_W),
        scratch_shapes=[pltpu.VMEM((tm + HALO, D_MODEL), BF16)],
        compiler_params=_cparams("arbitrary"),
        name="ssd_xbc",
    )(x, x, x, mods, g, wx, conv_w, conv_b)
    zp, zs, dtp, dts = pl.pallas_call(
        _ssd_zdt_kernel,
        grid=(ROWS // TM,),
        in_specs=[pl.BlockSpec((TM, D_MODEL), lambda i: (i, 0)), mod(TM), _const_spec((1, D_MODEL)),
                  _const_spec(wz.shape), _const_spec(wdt.shape)],
        out_specs=[*_group_specs(TM, SSD_DI), *_group_specs(TM, 2 * SSD_HEADS)],
        out_shape=group_shapes(SSD_DI) + group_shapes(2 * SSD_HEADS),
        compiler_params=_cparams("arbitrary"),
        name="ssd_zdt",
    )(x, mods, g, wz, wdt)
    return (zp, zs), tuple(xbc), (dtp, dts)


STREAMS = 2


def _split3(v):
    hi = v.astype(BF16)
    r1 = v - hi.astype(F32)
    mid = r1.astype(BF16)
    return hi, mid, (r1 - mid.astype(F32)).astype(BF16)


def _ssd_kernel(*refs, reverse, final, has_init, emit_state):
    refs = list(refs)
    xbc_ref, dt_ref, dtb_ref, alog_ref = refs[:4]
    del refs[:4]
    init_ref = refs.pop(0) if has_init else None
    wide_ref = refs.pop(0)
    if final:
        yin_ref, z_ref, dexp_ref, ng_ref = refs[:4]
        del refs[:4]
    y_ref = refs.pop(0)
    sfin_ref = refs.pop(0) if emit_state else None
    s_ref = refs.pop(0)
    yacc_ref = refs.pop(0) if final else None
    pos = pl.program_id(1)
    q = SSD_CHUNK
    hh = SSD_HEADS
    gw = SSD_DI // SSD_GROUPS

    @pl.when(pos == 0)
    def _():
        s_ref[...] = init_ref[...] if has_init else jnp.zeros_like(s_ref)

    d0 = hh if reverse else 0
    a = -jnp.exp(alog_ref[:, d0:d0 + hh])
    row = lax.broadcasted_iota(jnp.int32, (q, q), 0)
    col = lax.broadcasted_iota(jnp.int32, (q, q), 1)
    tri = (row <= col) if reverse else (row >= col)
    lane_lo = col < SSD_HEADDIM
    dn_t = (((1,), (1,)), ((), ()))

    def chunk_factors(k):
        dt_in = dt_ref[k, :, d0:d0 + hh] + dtb_ref[:, d0:d0 + hh]
        dt = jnp.maximum(dt_in, 0.0) + jnp.log1p(jnp.exp(-jnp.abs(dt_in)))
        acs = jnp.dot(tri.astype(F32), dt * a, precision=HIGHEST, preferred_element_type=F32)
        a_end = acs[0:1] if reverse else acs[q - 1:q]
        factors = jnp.concatenate([jnp.exp(acs), jnp.exp(a_end - acs) * dt,
                                   jnp.broadcast_to(jnp.exp(a_end), (SUBLANES, hh))], axis=0)
        wide = jnp.dot(jnp.concatenate(_split3(factors), axis=1), wide_ref[...], preferred_element_type=F32)
        acs2 = acs * LOG2_E
        mt = jnp.concatenate([acs2 - jnp.log2(dt), jnp.zeros((q, LANES - hh), F32)], axis=1).T
        return wide[0:q], wide[q:2 * q], wide[2 * q:2 * q + 1], acs2, mt

    def group(k, g, eacs_w, wst_w, dec_w, acs2, mt):
        gs = slice(g * gw, (g + 1) * gw)
        b_g = xbc_ref[k, :, SSD_DI + g * SSD_STATE:SSD_DI + (g + 1) * SSD_STATE]
        c_g = xbc_ref[k, :, SSD_DI + SSD_BC_W + g * SSD_STATE:SSD_DI + SSD_BC_W + (g + 1) * SSD_STATE]
        cgb = c_g.astype(BF16)
        cb = lax.dot_general(cgb, b_g.astype(BF16), dn_t, preferred_element_type=F32)
        s_g = s_ref[k, :, gs]
        y_off = jnp.dot(cgb, s_g.astype(BF16), preferred_element_type=F32) * eacs_w[:, gs]
        x_g = xbc_ref[k, :, gs]
        s_new = jnp.dot(b_g.T.astype(BF16), (x_g * wst_w[:, gs]).astype(BF16), preferred_element_type=F32)
        s_ref[k, :, gs] = s_g * dec_w[:, gs] + s_new
        for p in range(gw // LANES):
            ps = slice(p * LANES, (p + 1) * LANES)
            sl = slice(g * gw + p * LANES, g * gw + (p + 1) * LANES)
            xp = x_g[:, ps]
            xpb = xp.astype(BF16)
            zero = jnp.zeros_like(xpb)
            rhs = jnp.concatenate([jnp.where(lane_lo, xpb, zero), jnp.where(lane_lo, zero, xpb)], axis=0)
            w_l = []
            for h in (2 * (g * 4 + p), 2 * (g * 4 + p) + 1):
                seg = jnp.broadcast_to(acs2[:, h:h + 1], (q, q)) - mt[h:h + 1]
                w_l.append((cb * jnp.exp2(jnp.where(tri, seg, -jnp.inf))).astype(BF16))
            yp = jnp.dot(jnp.concatenate(w_l, axis=1), rhs, preferred_element_type=F32) + y_off[:, ps]
            if final:
                yacc_ref[k, :, sl] = yp
            else:
                y_ref[k, :, sl] = yp

    per_stream = [chunk_factors(k) for k in range(STREAMS)]
    for g in range(SSD_GROUPS):
        for k in range(STREAMS):
            group(k, g, *per_stream[k])

    if final:
        for k in range(STREAMS):
            yt = (yacc_ref[k] + yin_ref[k] + xbc_ref[k, :, 0:SSD_DI] * dexp_ref[...]) * z_ref[k]
            ms = jnp.mean(yt * yt, axis=-1, keepdims=True)
            y_ref[k] = (yt * lax.rsqrt(ms + EPS) * ng_ref[...]).astype(y_ref.dtype)

    if emit_state:
        @pl.when(pos == pl.num_programs(1) - 1)
        def _():
            for k in range(STREAMS):
                sfin_ref[k] = s_ref[k].T


def _ssd_scan(xbc, dt, dt_bias, a_log, init, extra, *, reverse, nseq, emit_state):
    final = extra is not None
    has_init = init is not None
    assert nseq % STREAMS == 0 and (not has_init or nseq == STREAMS)
    nc = xbc.shape[0] // (nseq * SSD_CHUNK)
    npairs = nseq // STREAMS
    view = lambda a: a.reshape(npairs, STREAMS, nc, SSD_CHUNK, a.shape[-1])
    chunk = lambda width: pl.BlockSpec((None, STREAMS, None, SSD_CHUNK, width),
                                       lambda p, t: (p, 0, (nc - 1 - t) if reverse else t, 0, 0))
    widen = jnp.tile(jnp.repeat(jnp.eye(SSD_HEADS, dtype=BF16), SSD_HEADDIM, axis=1), (3, 1))
    in_specs = [chunk(SSD_XBC_W), chunk(2 * SSD_HEADS),
                _const_spec((1, 2 * SSD_HEADS)), _const_spec((1, 2 * SSD_HEADS))]
    args = [view(xbc), view(dt), dt_bias, a_log]
    if has_init:
        in_specs.append(_const_spec(init.shape))
        args.append(init)
    in_specs.append(_const_spec(widen.shape))
    args.append(widen)
    scratch = [pltpu.VMEM((STREAMS, SSD_STATE, SSD_DI), F32)]
    if final:
        y_other, z, d_exp, norm_g = extra
        in_specs += [chunk(SSD_DI), chunk(SSD_DI), _const_spec((1, SSD_DI)), _const_spec((1, SSD_DI))]
        args += [view(y_other), view(z), d_exp, norm_g]
        scratch.append(pltpu.VMEM((STREAMS, SSD_CHUNK, SSD_DI), F32))
    out_specs = [chunk(SSD_DI)]
    out_shape = [jax.ShapeDtypeStruct((npairs, STREAMS, nc, SSD_CHUNK, SSD_DI), BF16 if final else F32)]
    if emit_state:
        out_specs.append(pl.BlockSpec((None, STREAMS, SSD_DI, SSD_STATE), lambda p, t: (p, 0, 0, 0)))
        out_shape.append(jax.ShapeDtypeStruct((npairs, STREAMS, SSD_DI, SSD_STATE), F32))
    outs = pl.pallas_call(
        functools.partial(_ssd_kernel, reverse=reverse, final=final, has_init=has_init, emit_state=emit_state),
        grid=(npairs, nc),
        in_specs=in_specs,
        out_specs=out_specs,
        out_shape=out_shape,
        scratch_shapes=scratch,
        compiler_params=_cparams("arbitrary", "arbitrary"),
        name="ssd_scan_bwd" if reverse else "ssd_scan_fwd",
    )(*args)
    y = outs[0].reshape(xbc.shape[0], SSD_DI)
    return (y, outs[1].reshape(nseq, SSD_DI, SSD_STATE)) if emit_state else y


def kernel(x_prompt, x_sample, cache_k, cache_v, state_ssm_fwd, state_ssm_bwd, c, c_ctx, ada_w, ada_b, norm_mix_g, norm_ffn_g, att_w_in, att_lambda, att_subln_g, pool_w, pool_scale, att_w_out, ssd_w_in, ssd_conv_w, ssd_conv_b, ssd_dt_bias, ssd_a_log, ssd_d, ssd_norm_g, ssd_w_out, ffn_w_up, ffn_conv_w, ffn_conv_b, ffn_w_down, final_norm_g):
    x = (x_prompt.reshape(ROWS_P, D_MODEL), x_sample.reshape(ROWS_S, D_MODEL))
    cvec = jnp.concatenate([c_ctx[None], c, jnp.zeros((MOD_PAD - N_MOD, D_MODEL), F32)], axis=0)
    mods_all = _adaln(cvec, ada_w, ada_b).reshape(DEPTH, MOD_PAD, 6, D_MODEL)
    cos, sin = _rope_tables()
    ffn_up, ffn_down = ffn_w_up.astype(BF16), ffn_w_down.astype(BF16)
    outs = {}
    for l in range(DEPTH):
        i = l // 2
        mods = mods_all[l, :N_MOD]
        if l % 2 == 0:
            lam_init = 0.8 - 0.6 * math.exp(-0.3 * l)
            if not isinstance(x, tuple):
                x = (x[:ROWS_P], x[ROWS_P:])
            q, kt_p, k_s, v_p, v_s, xpool = _att_in(*x, mods, norm_mix_g[l][None], att_w_in[i].astype(BF16), cos, sin)
            ckt = cache_k[:, i].transpose(0, 2, 3, 4, 1).reshape(DEC_BATCH, DA_HEADS, 2 * DA_HD, PAST_LEN)
            cv = cache_v[:, i].reshape(DEC_BATCH * PAST_LEN, DA_V_W)
            att = functools.partial(_attention, lam_vecs=att_lambda[i], subln_g=att_subln_g[i][None],
                                    lam_init=lam_init)
            o_p = att(q, [], kt_p, [v_p], batch=BATCH, lq=SEQ, tq=SEQ, q_row0=0, heads_per_step=DA_HEADS)
            o_s = att(q, [k_s], ckt, [v_s, cv], batch=DEC_BATCH, lq=DEC_SEQ, tq=TQ, q_row0=ROWS_P, heads_per_step=DA_HEADS)
            w_out = att_w_out[i].astype(BF16)
            x = _mix_out(x, mods, (o_p, o_s), xpool, pool_w[i].astype(BF16), pool_scale[i][None],
                         w_out[:DA_V_W], w_out[DA_V_W:])
            outs.setdefault("k", []).append(
                kt_p.reshape(BATCH, DA_HEADS, 2, DA_HD, SEQ).transpose(0, 4, 1, 2, 3))
            outs.setdefault("v", []).append(v_p.reshape(BATCH, SEQ, DA_HEADS, 2 * DA_HD))
        else:
            w_in = ssd_w_in[i]
            z, xbc, dt = _ssd_in(x, mods, norm_mix_g[l][None], w_in[:, :SSD_DI].astype(BF16),
                                 w_in[:, SSD_DI:SSD_DI + SSD_XBC_W].astype(BF16),
                                 w_in[:, SSD_DI + SSD_XBC_W:].astype(BF16),
                                 ssd_conv_w[i], ssd_conv_b[i][None])
            to_scan_layout = lambda s: s.transpose(0, 3, 1, 2).reshape(DEC_BATCH, SSD_STATE, SSD_DI)
            dtb, alog = ssd_dt_bias[i].reshape(1, -1), ssd_a_log[i].reshape(1, -1)
            d_exp, norm_g = jnp.repeat(ssd_d[i], SSD_HEADDIM)[None], ssd_norm_g[i][None]
            scan_p = functools.partial(_ssd_scan, xbc[0], dt[0], dtb, alog, None, nseq=BATCH, emit_state=True)
            y_f, s_f = scan_p(None, reverse=False)
            y_p, s_b = scan_p((y_f, z[0], d_exp, norm_g), reverse=True)
            scan_s = functools.partial(_ssd_scan, xbc[1], dt[1], dtb, alog, nseq=DEC_BATCH, emit_state=False)
            y_f = scan_s(to_scan_layout(state_ssm_fwd[:, i]), None, reverse=False)
            y_s = scan_s(to_scan_layout(state_ssm_bwd[:, i]), (y_f, z[1], d_exp, norm_g), reverse=True)
            x = _proj_res(x, mods, [(y_p, y_s)], [ssd_w_out[i].astype(BF16)], gate_row=2)
            outs.setdefault("sf", []).append(s_f.reshape(BATCH, SSD_HEADS, SSD_HEADDIM, SSD_STATE))
            outs.setdefault("sb", []).append(s_b.reshape(BATCH, SSD_HEADS, SSD_HEADDIM, SSD_STATE))
        x = _conv_ffn(x, mods, norm_ffn_g[l][None], ffn_up, ffn_conv_w[l], ffn_conv_b[l][None],
                      ffn_down, final_norm_g[None], layer=l, last_layer=(l == DEPTH - 1))
    y_prompt = x[0].reshape(BATCH, SEQ, D_MODEL)
    y_sample = x[1].reshape(DEC_BATCH, DEC_SEQ, D_MODEL)
    return (y_prompt, y_sample, jnp.stack(outs["k"], axis=1), jnp.stack(outs["v"], axis=1),
            jnp.stack(outs["sf"], axis=1), jnp.stack(outs["sb"], axis=1))
```

```python
import functools
import math

import jax
import jax.numpy as jnp
from jax import lax
from jax.experimental import pallas as pl
from jax.experimental.pallas import tpu as pltpu

F32 = jnp.float32
BF16 = jnp.bfloat16
HIGHEST = lax.Precision.HIGHEST

D_MODEL = 1024
BATCH = 32
SEQ = 256
DEPTH = 2
DEC_BATCH = 2
DEC_SEQ = 2048
PAST_LEN = 512
GRID_W = 64
EPS = 1e-6
DA_HEADS = 4
DA_HD = 64
DA_QK_W = DA_HEADS * 2 * DA_HD
DA_V_W = DA_HEADS * 2 * DA_HD
POOL_W = D_MODEL - DA_V_W
POOL_WINDOWS = (2, 4, 8, 16)
POOL_GROUPS = 4
POOL_GC = POOL_W // POOL_GROUPS
ATT_IN_W = 2 * DA_QK_W + DA_V_W + POOL_W
LOG2_E = math.log2(math.e)
ROPE_THETA = 10000.0
ROPE_NF = DA_HD // 4
SSD_DI = 2 * D_MODEL
SSD_HEADDIM = 64
SSD_HEADS = SSD_DI // SSD_HEADDIM
SSD_GROUPS = 4
SSD_STATE = 128
SSD_CHUNK = 128
SSD_BC_W = SSD_GROUPS * SSD_STATE
SSD_XBC_W = SSD_DI + 2 * SSD_BC_W
D_FF = 2816

ROWS_P = BATCH * SEQ
ROWS_S = DEC_BATCH * DEC_SEQ
ROWS = ROWS_P + ROWS_S
N_MOD = 1 + DEC_BATCH
MOD_PAD = 8

LANES = 128
SUBLANES = 8
HALO = 2 * SUBLANES
VMEM_LIMIT = 56 * 1024 * 1024

TM = 512
TM_CONV = 512
TF = 256
TQ = 256


def _cparams(*sem):
    return pltpu.CompilerParams(dimension_semantics=sem, vmem_limit_bytes=VMEM_LIMIT)


def _mod_index(i, tm):
    n_p, per_seq = ROWS_P // tm, DEC_SEQ // tm
    return jnp.where(i < n_p, 0, 1 + (i - n_p) // per_seq)


def _silu(x):
    return x / (1.0 + jnp.exp2(x * -LOG2_E))


def _modulate(x, g, shift, scale):
    ms = jnp.mean(x * x, axis=-1, keepdims=True)
    y = x * lax.rsqrt(ms + EPS) * g
    return y * (1.0 + scale) + shift


def _const_spec(shape):
    nd = len(shape)
    return pl.BlockSpec(shape, lambda *_: (0,) * nd, pipeline_mode=pl.Buffered(1))


def _layer_spec(stacked_shape, layer):
    nd = len(stacked_shape) - 1
    return pl.BlockSpec((None, *stacked_shape[1:]), lambda *_: (layer,) + (0,) * nd, pipeline_mode=pl.Buffered(1))


def _group_specs(tm, width):
    n_p = ROWS_P // tm
    return (pl.BlockSpec((tm, width), lambda i: (jnp.minimum(i, n_p - 1), 0)),
            pl.BlockSpec((tm, width), lambda i: (jnp.maximum(i - n_p, 0), 0)))


def _group_rows(i, tm, p_ref, s_ref):
    return jnp.where(i < ROWS_P // tm, p_ref[...], s_ref[...])


def _halo_specs(tm, width):
    per = tm // SUBLANES
    last = ROWS // SUBLANES - 1
    prev = pl.BlockSpec((SUBLANES, width), lambda i: (jnp.maximum(i * per - 1, 0), 0))
    nxt = pl.BlockSpec((SUBLANES, width), lambda i: (jnp.minimum((i + 1) * per, last), 0))
    return prev, nxt


def _fill_with_halo(h_ref, x_ref, xp_ref, xn_ref, modulate, i, tm):
    m = jnp.where(i < ROWS_P // tm, SEQ - 1, DEC_SEQ - 1)
    starts = ((i * tm) & m) == 0
    ends = (((i + 1) * tm) & m) == 0
    h_ref[0:tm] = modulate(x_ref[...]).astype(BF16)
    halo = jnp.concatenate([jnp.where(ends, 0.0, modulate(xn_ref[...])),
                            jnp.where(starts, 0.0, modulate(xp_ref[...]))], axis=0)
    h_ref[tm:tm + HALO] = halo.astype(BF16)


def _conv3(u, cw, cb, tm, prompt):
    n = u.shape[0]
    up = pltpu.roll(u, 1, axis=0)[0:tm]
    un = pltpu.roll(u, n - 1, axis=0)[0:tm]
    if tm > SEQ and prompt is not False:
        s = SUBLANES
        sub = lax.broadcasted_iota(jnp.int32, (s, u.shape[1]), 0)
        at_edge = (lambda m: m) if prompt is True else (lambda m: prompt & m)
        ups, uns, lo = [], [], 0
        for edge in range(SEQ, tm, SEQ):
            ups += [up[lo:edge], jnp.where(at_edge(sub == 0), 0.0, up[edge:edge + s])]
            uns += [un[lo - s if lo else 0:edge - s], jnp.where(at_edge(sub == s - 1), 0.0, un[edge - s:edge])]
            lo = edge + s
        up = jnp.concatenate(ups + [up[lo:tm]], axis=0)
        un = jnp.concatenate(uns + [un[lo - s:tm]], axis=0)
    return cw[1:2] * u[0:tm] + cb + cw[0:1] * up + cw[2:3] * un


def _adaln_kernel(ct_ref, w_ref, b_ref, o_ref):
    st = _silu(ct_ref[...])
    w = w_ref[0]
    rows = [jnp.sum(w * st[:, r:r + 1], axis=0, keepdims=True) for r in range(N_MOD)]
    rows.append(jnp.zeros((MOD_PAD - N_MOD, w.shape[1]), F32))
    o_ref[0] = jnp.concatenate(rows, axis=0) + b_ref[0]


def _adaln(cvec, ada_w, ada_b):
    tn = 1536
    n = 6 * D_MODEL
    return pl.pallas_call(
        _adaln_kernel,
        grid=(DEPTH, n // tn),
        in_specs=[_const_spec((D_MODEL, MOD_PAD)),
                  pl.BlockSpec((1, D_MODEL, tn), lambda l, j: (l, 0, j)),
                  pl.BlockSpec((1, 1, tn), lambda l, j: (l, 0, j))],
        out_specs=pl.BlockSpec((1, MOD_PAD, tn), lambda l, j: (l, 0, j)),
        out_shape=jax.ShapeDtypeStruct((DEPTH, MOD_PAD, n), F32),
        compiler_params=_cparams("parallel", "parallel"),
        name="adaln",
    )(cvec, ada_w, ada_b.reshape(DEPTH, 1, n))


def _rope(x, cos, sin_signed):
    lane = lax.broadcasted_iota(jnp.int32, cos.shape, 1)
    lower = (lane & 31) < ROPE_NF
    out = []
    for s in range(x.shape[1] // LANES):
        xs = x[:, s * LANES:(s + 1) * LANES]
        partner = jnp.where(lower, pltpu.roll(xs, LANES - ROPE_NF, axis=1), pltpu.roll(xs, ROPE_NF, axis=1))
        out.append(xs * cos + partner * sin_signed)
    return jnp.concatenate(out, axis=1)


def _att_in_kernel(xp_ref, xs_ref, mod_ref, g_ref, w_ref, cos_ref, sin_ref,
                   q_ref, kpt_ref, ks_ref, vp_ref, vs_ref, p_ref):
    i = pl.program_id(0)
    x = _group_rows(i, TM, xp_ref, xs_ref)
    h = _modulate(x, g_ref[...], mod_ref[0, 0:1], mod_ref[0, 1:2]).astype(BF16)
    proj = jnp.dot(h, w_ref[...], preferred_element_type=F32)
    q = proj[:, :DA_QK_W]
    k = proj[:, DA_QK_W:2 * DA_QK_W]
    v = proj[:, 2 * DA_QK_W:2 * DA_QK_W + DA_V_W]
    p_ref[...] = proj[:, 2 * DA_QK_W + DA_V_W:]
    latent = i >= ROWS_P // TM

    @pl.when(latent)
    def _():
        q_ref[...] = _rope(q, cos_ref[...], sin_ref[...])
        ks_ref[...] = _rope(k, cos_ref[...], sin_ref[...])
        vs_ref[...] = v

    @pl.when(jnp.logical_not(latent))
    def _():
        q_ref[...] = q
        vp_ref[...] = v
        hw = 2 * DA_HD
        for s in range(TM // SEQ):
            for hd in range(DA_HEADS):
                kpt_ref[s, hd] = k[s * SEQ:(s + 1) * SEQ, hd * hw:(hd + 1) * hw].T


def _rope_tables():
    t = jnp.arange(DEC_SEQ, dtype=F32)
    r, col = jnp.floor(t / GRID_W), t % GRID_W
    inv = ROPE_THETA ** (-jnp.arange(ROPE_NF, dtype=F32) / ROPE_NF)
    ar, ac = r[:, None] * inv, col[:, None] * inv
    cos = jnp.concatenate([jnp.cos(ar), jnp.cos(ar), jnp.cos(ac), jnp.cos(ac)], axis=1)
    sin = jnp.concatenate([-jnp.sin(ar), jnp.sin(ar), -jnp.sin(ac), jnp.sin(ac)], axis=1)
    return jnp.tile(cos, (1, 2)), jnp.tile(sin, (1, 2))


def _att_in(xp, xs, mods, g, w, cos, sin):
    n_p, per_seq = ROWS_P // TM, DEC_SEQ // TM
    tab = pl.BlockSpec((TM, LANES), lambda i: (jnp.maximum(i - n_p, 0) % per_seq, 0))
    out = pl.BlockSpec((TM, DA_QK_W), lambda i: (i, 0))
    out_p, out_s = _group_specs(TM, DA_QK_W)
    full, grp_p, grp_s = (jax.ShapeDtypeStruct((r, DA_QK_W), F32) for r in (ROWS, ROWS_P, ROWS_S))
    seqs = TM // SEQ
    out_kt = pl.BlockSpec((seqs, DA_HEADS, 2 * DA_HD, SEQ), lambda i: (jnp.minimum(i, n_p - 1), 0, 0, 0))
    kt = jax.ShapeDtypeStruct((BATCH, DA_HEADS, 2 * DA_HD, SEQ), F32)
    return pl.pallas_call(
        _att_in_kernel,
        grid=(ROWS // TM,),
        in_specs=[*_group_specs(TM, D_MODEL),
                  pl.BlockSpec((1, 6, D_MODEL), lambda i: (_mod_index(i, TM), 0, 0)),
                  _const_spec((1, D_MODEL)),
                  _const_spec((D_MODEL, ATT_IN_W)),
                  tab, tab],
        out_specs=[out, out_kt, out_s, out_p, out_s, out],
        out_shape=[full, kt, grp_s, grp_p, grp_s, full],
        compiler_params=_cparams("arbitrary"),
        name="att_in",
    )(xp, xs, mods, g, w, cos, sin)


def _attn_kernel(lam_ref, g_ref, q_ref, *rest, n_row_keys, n_values, lam_init):
    k_refs, kt_ref = rest[:n_row_keys], rest[n_row_keys]
    v_refs, o_ref = rest[n_row_keys + 1:n_row_keys + 1 + n_values], rest[-1]
    lv = lam_ref[...]
    lam = (jnp.exp(jnp.sum(lv[0:1] * lv[1:2], keepdims=True))
           - jnp.exp(jnp.sum(lv[2:3] * lv[3:4], keepdims=True)) + lam_init)
    dn = (((1,), (1,)), ((), ()))
    hw = 2 * DA_HD
    lane = lax.broadcasted_iota(jnp.int32, (q_ref.shape[0], hw), 1)
    for h in range(q_ref.shape[1] // hw):
        sl = slice(h * hw, (h + 1) * hw)
        q = q_ref[:, sl] * (DA_HD ** -0.5 * LOG2_E)
        kbs = [k_ref[:, sl].astype(BF16) for k_ref in k_refs]
        ktb = kt_ref[h].astype(BF16)
        vb = jnp.concatenate([v_ref[:, sl].astype(BF16) for v_ref in v_refs], axis=0)

        def unnormalised(qc):
            qb = qc.astype(BF16)
            s = jnp.concatenate([lax.dot_general(qb, kb, dn, preferred_element_type=F32) for kb in kbs]
                                + [jnp.dot(qb, ktb, preferred_element_type=F32)], axis=1)
            e = jnp.exp2(s - jnp.max(s, axis=-1, keepdims=True))
            return (jnp.dot(e.astype(BF16), vb, preferred_element_type=F32),
                    jnp.sum(e, axis=-1, keepdims=True))

        o1, l1 = unnormalised(jnp.where(lane < DA_HD, q, 0.0))
        o2, l2 = unnormalised(jnp.where(lane < DA_HD, 0.0, q))
        o = o1 / l1 - o2 * (lam / l2)
        ms = jnp.mean(o * o, axis=-1, keepdims=True)
        o_ref[:, sl] = (o * lax.rsqrt(ms + EPS) * g_ref[...] * (1.0 - lam_init)).astype(o_ref.dtype)


def _attention(q, row_keys, keys_t, values, lam_vecs, subln_g, *, batch, lq, tq, q_row0, heads_per_step, lam_init):
    nq = lq // tq
    q0 = q_row0 // tq
    hw = 2 * DA_HD
    bw = heads_per_step * hw
    seq_block = lambda a: pl.BlockSpec((a.shape[0] // batch, bw), lambda b, h, i: (b, h))
    return pl.pallas_call(
        functools.partial(_attn_kernel, n_row_keys=len(row_keys), n_values=len(values), lam_init=lam_init),
        grid=(batch, DA_HEADS // heads_per_step, nq),
        in_specs=[_const_spec((4, DA_HD)),
                  _const_spec((1, hw)),
                  pl.BlockSpec((tq, bw), lambda b, h, i: (q0 + b * nq + i, h))]
                 + [seq_block(a) for a in row_keys]
                 + [pl.BlockSpec((None, heads_per_step, hw, keys_t.shape[-1]), lambda b, h, i: (b, h, 0, 0))]
                 + [seq_block(a) for a in values],
        out_specs=pl.BlockSpec((tq, bw), lambda b, h, i: (b * nq + i, h)),
        out_shape=jax.ShapeDtypeStruct((batch * lq, DA_V_W), BF16),
        compiler_params=_cparams("parallel", "parallel", "parallel"),
        name="diff_attn",
    )(lam_vecs, subln_g, q, *row_keys, keys_t, *values)


assert max(POOL_WINDOWS) // 2 <= SUBLANES


def _pooled(ext, pos_ext, seq_len, w_ref, sc_ref, tm):
    n = ext.shape[0]
    masks = {}

    def shifted(a, k):
        if k not in masks:
            masks[k] = (pos_ext + k < seq_len) if k > 0 else (pos_ext + k >= 0)
        return jnp.where(masks[k], pltpu.roll(a, (-k) % n, axis=0), 0.0)

    pos = pos_ext[0:tm]
    out = []
    for gi, win in enumerate(POOL_WINDOWS):
        half = win // 2
        sl = slice(gi * POOL_GC, (gi + 1) * POOL_GC)
        xg = ext[:, sl]
        right, left, step = xg, xg, 1
        while step < half:
            right, left, step = right + shifted(right, step), left + shifted(left, -step), 2 * step
        acc = (right + shifted(left, -1))[0:tm]
        cnt = (jnp.minimum(pos + half, seq_len) - jnp.maximum(pos - half, 0)).astype(F32)
        pooled = acc / cnt - xg[0:tm]
        y = jnp.dot(pooled.astype(BF16), w_ref[gi], preferred_element_type=F32)
        out.append((y * sc_ref[:, sl]).astype(BF16))
    return jnp.concatenate(out, axis=1)


def _mix_out_kernel(mod_ref, xp_ref, xs_ref, op_ref, os_ref, pool_ref, poolp_ref, pooln_ref,
                    pw_ref, ps_ref, wa_ref, wb_ref, o_ref):
    i = pl.program_id(0)
    prompt = i < ROWS_P // TM
    seq_len = jnp.where(prompt, SEQ, DEC_SEQ)
    ext = jnp.concatenate([pool_ref[...], pooln_ref[...], poolp_ref[...]], axis=0)
    n = ext.shape[0]
    r = lax.broadcasted_iota(jnp.int32, (n, POOL_GC), 0)
    r = jnp.where(r < TM + SUBLANES, r, r - n)
    pooled = _pooled(ext, (i * TM + r) & (seq_len - 1), seq_len, pw_ref, ps_ref, TM)
    att = _group_rows(i, TM, op_ref, os_ref)
    acc = (jnp.dot(att, wa_ref[...], preferred_element_type=F32)
           + jnp.dot(pooled, wb_ref[...], preferred_element_type=F32))
    o_ref[...] = _group_rows(i, TM, xp_ref, xs_ref) + mod_ref[0, 2:3] * acc


def _mix_out(x, mods, att, xpool, pool_w, pool_scale, w_att, w_pool):
    prev, nxt = _halo_specs(TM, POOL_W)
    return pl.pallas_call(
        _mix_out_kernel,
        grid=(ROWS // TM,),
        in_specs=[pl.BlockSpec((1, 6, D_MODEL), lambda i: (_mod_index(i, TM), 0, 0)),
                  *_group_specs(TM, D_MODEL), *_group_specs(TM, DA_V_W),
                  pl.BlockSpec((TM, POOL_W), lambda i: (i, 0)), prev, nxt,
                  _const_spec(pool_w.shape), _const_spec(pool_scale.shape),
                  _const_spec(w_att.shape), _const_spec(w_pool.shape)],
        out_specs=pl.BlockSpec((TM, D_MODEL), lambda i: (i, 0)),
        out_shape=jax.ShapeDtypeStruct((ROWS, D_MODEL), F32),
        compiler_params=_cparams("parallel"),
        name="mix_out",
    )(mods, *x, *att, xpool, xpool, xpool, pool_w, pool_scale, w_att, w_pool)


def _proj_res_kernel(*refs, gate_row, grouped):
    i = pl.program_id(0)
    mod_ref, o_ref = refs[0], refs[-1]
    pos, rows = 1, []
    for is_pair in grouped:
        rows.append(_group_rows(i, TM, refs[pos], refs[pos + 1]) if is_pair else refs[pos][...])
        pos += 2 if is_pair else 1
    acc = None
    for a, w_ref in zip(rows[1:], refs[pos:-1]):
        d = jnp.dot(a.astype(BF16), w_ref[...], preferred_element_type=F32)
        acc = d if acc is None else acc + d
    o_ref[...] = rows[0] + mod_ref[0, gate_row:gate_row + 1] * acc


def _proj_res(x, mods, acts, ws, *, gate_row):
    operands, specs, grouped = [], [], []
    for a in (x, *acts):
        is_pair = isinstance(a, tuple)
        grouped.append(is_pair)
        if is_pair:
            operands += list(a)
            specs += list(_group_specs(TM, a[0].shape[1]))
        else:
            operands.append(a)
            specs.append(pl.BlockSpec((TM, a.shape[1]), lambda i: (i, 0)))
    return pl.pallas_call(
        functools.partial(_proj_res_kernel, gate_row=gate_row, grouped=tuple(grouped)),
        grid=(ROWS // TM,),
        in_specs=[pl.BlockSpec((1, 6, D_MODEL), lambda i: (_mod_index(i, TM), 0, 0))]
                 + specs + [_const_spec(w.shape) for w in ws],
        out_specs=pl.BlockSpec((TM, D_MODEL), lambda i: (i, 0)),
        out_shape=jax.ShapeDtypeStruct((ROWS, D_MODEL), F32),
        compiler_params=_cparams("parallel"),
        name="proj_res",
    )(mods, *operands, *ws)


def _ffn_kernel(x_ref, xp_ref, xn_ref, mod_ref, g_ref, wup_ref, cw_ref, cb_ref, wd_ref, fg_ref,
                *rest, last_layer):
    h_ref, act_ref = rest[-2:]
    i = pl.program_id(0)
    tm = TM_CONV
    modulate = functools.partial(_modulate, g=g_ref[...], shift=mod_ref[0, 3:4], scale=mod_ref[0, 4:5])
    _fill_with_halo(h_ref, x_ref, xp_ref, xn_ref, modulate, i, tm)
    h = h_ref[...]
    prompt = i < ROWS_P // tm
    for c in range(D_FF // TF):
        gs, vs = slice(c * TF, (c + 1) * TF), slice(D_FF + c * TF, D_FF + (c + 1) * TF)
        ug = jnp.dot(h, wup_ref[:, gs], preferred_element_type=F32)
        uv = jnp.dot(h, wup_ref[:, vs], preferred_element_type=F32)
        cg = _conv3(ug, cw_ref[:, gs], cb_ref[:, gs], tm, prompt)
        cv = _conv3(uv, cw_ref[:, vs], cb_ref[:, vs], tm, prompt)
        act_ref[:, gs] = (_silu(cg) * cv).astype(BF16)
    y = x_ref[...] + mod_ref[0, 5:6] * jnp.dot(act_ref[...], wd_ref[...], preferred_element_type=F32)
    if not last_layer:
        rest[0][...] = y
        return
    ms = jnp.mean(y * y, axis=-1, keepdims=True)
    y = y * lax.rsqrt(ms + EPS) * fg_ref[...]
    op_ref, os_ref = rest[:2]

    @pl.when(prompt)
    def _():
        op_ref[...] = y

    @pl.when(jnp.logical_not(prompt))
    def _():
        os_ref[...] = y


def _conv_ffn(x, mods, g, w_up, conv_w, conv_b, w_down, final_g, *, layer, last_layer):
    tm = TM_CONV
    prev, nxt = _halo_specs(tm, D_MODEL)
    if last_layer:
        out_specs = list(_group_specs(tm, D_MODEL))
        out_shape = [jax.ShapeDtypeStruct((ROWS_P, D_MODEL), F32), jax.ShapeDtypeStruct((ROWS_S, D_MODEL), F32)]
    else:
        out_specs = pl.BlockSpec((tm, D_MODEL), lambda i: (i, 0))
        out_shape = jax.ShapeDtypeStruct((ROWS, D_MODEL), F32)
    return pl.pallas_call(
        functools.partial(_ffn_kernel, last_layer=last_layer),
        grid=(ROWS // tm,),
        in_specs=[pl.BlockSpec((tm, D_MODEL), lambda i: (i, 0)), prev, nxt,
                  pl.BlockSpec((1, 6, D_MODEL), lambda i: (_mod_index(i, tm), 0, 0)),
                  _const_spec((1, D_MODEL)),
                  _layer_spec(w_up.shape, layer), _const_spec(conv_w.shape), _const_spec(conv_b.shape),
                  _layer_spec(w_down.shape, layer), _const_spec((1, D_MODEL))],
        out_specs=out_specs,
        out_shape=out_shape,
        scratch_shapes=[pltpu.VMEM((tm + HALO, D_MODEL), BF16), pltpu.VMEM((tm, D_FF), BF16)],
        compiler_params=_cparams("arbitrary"),
        name="conv_ffn",
    )(x, x, x, mods, g, w_up, conv_w, conv_b, w_down, final_g)


XBC_COLS = 256


XBC_ROWS = 64


def _ssd_xbc_kernel(x_ref, xp_ref, xn_ref, mod_ref, g_ref, wx_ref, cw_ref, cb_ref, xbcp_ref, xbcs_ref,
                    h_ref, u_ref):
    i = pl.program_id(0)
    tm = TM_CONV
    s, rb = SUBLANES, XBC_ROWS
    modulate = functools.partial(_modulate, g=g_ref[...], shift=mod_ref[0, 0:1], scale=mod_ref[0, 1:2])
    _fill_with_halo(h_ref, x_ref, xp_ref, xn_ref, modulate, i, tm)

    def project(xbc_ref, is_prompt):
        h = h_ref[...]
        sub = lax.broadcasted_iota(jnp.int32, (s, XBC_COLS), 0)
        for c in range(SSD_XBC_W // XBC_COLS):
            sl = slice(c * XBC_COLS, (c + 1) * XBC_COLS)
            u = jnp.dot(h, wx_ref[:, sl], preferred_element_type=F32)
            uc_ref = u_ref.at[c % 2]
            uc_ref[0:s] = u[tm + s:tm + 2 * s]
            uc_ref[s:tm + 2 * s] = u[0:tm + s]
            cw, cb = cw_ref[:, sl], cb_ref[:, sl]
            for r0 in range(0, tm, rb):
                blk = uc_ref[r0:r0 + rb + 2 * s]
                up = pltpu.roll(blk, 1, axis=0)[s:s + rb]
                un = pltpu.roll(blk, rb + 2 * s - 1, axis=0)[s:s + rb]
                if is_prompt and r0 and r0 % SEQ == 0:
                    up = jnp.concatenate([jnp.where(sub == 0, 0.0, up[0:s]), up[s:]], axis=0)
                if is_prompt and r0 + rb < tm and (r0 + rb) % SEQ == 0:
                    un = jnp.concatenate([un[:rb - s], jnp.where(sub == s - 1, 0.0, un[rb - s:])], axis=0)
                xbc_ref[r0:r0 + rb, sl] = _silu(cw[1:2] * blk[s:s + rb] + cb + cw[0:1] * up + cw[2:3] * un)

    prompt = i < ROWS_P // tm
    pl.when(prompt)(lambda: project(xbcp_ref, True))
    pl.when(jnp.logical_not(prompt))(lambda: project(xbcs_ref, False))


def _ssd_zdt_kernel(x_ref, mod_ref, g_ref, wz_ref, wdt_ref, zp_ref, zs_ref, dtp_ref, dts_ref):
    i = pl.program_id(0)
    h = _modulate(x_ref[...], g_ref[...], mod_ref[0, 0:1], mod_ref[0, 1:2]).astype(BF16)

    def project(z_ref, dt_ref):
        z_ref[...] = _silu(jnp.dot(h, wz_ref[...], preferred_element_type=F32))
        dt_ref[...] = jnp.dot(h, wdt_ref[...], preferred_element_type=F32)

    prompt = i < ROWS_P // TM
    pl.when(prompt)(lambda: project(zp_ref, dtp_ref))
    pl.when(jnp.logical_not(prompt))(lambda: project(zs_ref, dts_ref))


def _ssd_in(x, mods, g, wz, wx, wdt, conv_w, conv_b):
    tm = TM_CONV
    prev, nxt = _halo_specs(tm, D_MODEL)
    mod = lambda t: pl.BlockSpec((1, 6, D_MODEL), lambda i: (_mod_index(i, t), 0, 0))
    group_shapes = lambda w: [jax.ShapeDtypeStruct((r, w), F32) for r in (ROWS_P, ROWS_S)]
    xbc = pl.pallas_call(
        _ssd_xbc_kernel,
        grid=(ROWS // tm,),
        in_specs=[pl.BlockSpec((tm, D_MODEL), lambda i: (i, 0)), prev, nxt, mod(tm), _const_spec((1, D_MODEL)),
                  _const_spec(wx.shape), _const_spec(conv_w.shape), _const_spec(conv_b.shape)],
        out_specs=list(_group_specs(tm, SSD_XBC_W)),
        out_shape=group_shapes(SSD_XBC_W),
        scratch_shapes=[pltpu.VMEM((tm + HALO, D_MODEL), BF16), pltpu.VMEM((2, tm + HALO, XBC_COLS), F32)],
        compiler_params=_cparams("arbitrary"),
        name="ssd_xbc",
    )(x, x, x, mods, g, wx, conv_w, conv_b)
    zp, zs, dtp, dts = pl.pallas_call(
        _ssd_zdt_kernel,
        grid=(ROWS // TM,),
        in_specs=[pl.BlockSpec((TM, D_MODEL), lambda i: (i, 0)), mod(TM), _const_spec((1, D_MODEL)),
                  _const_spec(wz.shape), _const_spec(wdt.shape)],
        out_specs=[*_group_specs(TM, SSD_DI), *_group_specs(TM, 2 * SSD_HEADS)],
        out_shape=group_shapes(SSD_DI) + group_shapes(2 * SSD_HEADS),
        compiler_params=_cparams("arbitrary"),
        name="ssd_zdt",
    )(x, mods, g, wz, wdt)
    return (zp, zs), tuple(xbc), (dtp, dts)


STREAMS = 2


def _split3(v):
    hi = v.astype(BF16)
    r1 = v - hi.astype(F32)
    mid = r1.astype(BF16)
    return hi, mid, (r1 - mid.astype(F32)).astype(BF16)


def _ssd_kernel(*refs, reverse, final, has_init, emit_state):
    refs = list(refs)
    xbc_ref, dt_ref, dtb_ref, alog_ref = refs[:4]
    del refs[:4]
    init_ref = refs.pop(0) if has_init else None
    wide_ref = refs.pop(0)
    if final:
        yin_ref, z_ref, dexp_ref, ng_ref = refs[:4]
        del refs[:4]
    y_ref = refs.pop(0)
    sfin_ref = refs.pop(0) if emit_state else None
    s_ref = refs.pop(0)
    yacc_ref = refs.pop(0) if final else None
    pos = pl.program_id(1)
    q = SSD_CHUNK
    hh = SSD_HEADS
    gw = SSD_DI // SSD_GROUPS

    @pl.when(pos == 0)
    def _():
        s_ref[...] = init_ref[...] if has_init else jnp.zeros_like(s_ref)

    d0 = hh if reverse else 0
    a = -jnp.exp(alog_ref[:, d0:d0 + hh])
    row = lax.broadcasted_iota(jnp.int32, (q, q), 0)
    col = lax.broadcasted_iota(jnp.int32, (q, q), 1)
    tri = (row <= col) if reverse else (row >= col)
    lane_lo = col < SSD_HEADDIM
    dn_t = (((1,), (1,)), ((), ()))

    def chunk_factors(k):
        dt_in = dt_ref[k, :, d0:d0 + hh] + dtb_ref[:, d0:d0 + hh]
        dt = jnp.maximum(dt_in, 0.0) + jnp.log1p(jnp.exp(-jnp.abs(dt_in)))
        acs = jnp.dot(tri.astype(F32), dt * a, precision=HIGHEST, preferred_element_type=F32)
        a_end = acs[0:1] if reverse else acs[q - 1:q]
        factors = jnp.concatenate([jnp.exp(acs), jnp.exp(a_end - acs) * dt,
                                   jnp.broadcast_to(jnp.exp(a_end), (SUBLANES, hh))], axis=0)
        wide = jnp.dot(jnp.concatenate(_split3(factors), axis=1), wide_ref[...], preferred_element_type=F32)
        acs2 = acs * LOG2_E
        mt = jnp.concatenate([acs2 - jnp.log2(dt), jnp.zeros((q, LANES - hh), F32)], axis=1).T
        return wide[0:q], wide[q:2 * q], wide[2 * q:2 * q + 1], acs2, mt

    def group(k, g, eacs_w, wst_w, dec_w, acs2, mt):
        gs = slice(g * gw, (g + 1) * gw)
        b_g = xbc_ref[k, :, SSD_DI + g * SSD_STATE:SSD_DI + (g + 1) * SSD_STATE]
        c_g = xbc_ref[k, :, SSD_DI + SSD_BC_W + g * SSD_STATE:SSD_DI + SSD_BC_W + (g + 1) * SSD_STATE]
        cgb = c_g.astype(BF16)
        cb = lax.dot_general(cgb, b_g.astype(BF16), dn_t, preferred_element_type=F32)
        s_g = s_ref[k, :, gs]
        y_off = jnp.dot(cgb, s_g.astype(BF16), preferred_element_type=F32) * eacs_w[:, gs]
        x_g = xbc_ref[k, :, gs]
        s_new = jnp.dot(b_g.T.astype(BF16), (x_g * wst_w[:, gs]).astype(BF16), preferred_element_type=F32)
        s_ref[k, :, gs] = s_g * dec_w[:, gs] + s_new
        for p in range(gw // LANES):
            ps = slice(p * LANES, (p + 1) * LANES)
            sl = slice(g * gw + p * LANES, g * gw + (p + 1) * LANES)
            xp = x_g[:, ps]
            xpb = xp.astype(BF16)
            zero = jnp.zeros_like(xpb)
            rhs = jnp.concatenate([jnp.where(lane_lo, xpb, zero), jnp.where(lane_lo, zero, xpb)], axis=0)
            w_l = []
            for h in (2 * (g * 4 + p), 2 * (g * 4 + p) + 1):
                seg = jnp.broadcast_to(acs2[:, h:h + 1], (q, q)) - mt[h:h + 1]
                w_l.append((cb * jnp.exp2(jnp.where(tri, seg, -jnp.inf))).astype(BF16))
            yp = jnp.dot(jnp.concatenate(w_l, axis=1), rhs, preferred_element_type=F32) + y_off[:, ps]
            if final:
                yacc_ref[k, :, sl] = yp
            else:
                y_ref[k, :, sl] = yp

    per_stream = [chunk_factors(k) for k in range(STREAMS)]
    for g in range(SSD_GROUPS):
        for k in range(STREAMS):
            group(k, g, *per_stream[k])

    if final:
        for k in range(STREAMS):
            yt = (yacc_ref[k] + yin_ref[k] + xbc_ref[k, :, 0:SSD_DI] * dexp_ref[...]) * z_ref[k]
            ms = jnp.mean(yt * yt, axis=-1, keepdims=True)
            y_ref[k] = (yt * lax.rsqrt(ms + EPS) * ng_ref[...]).astype(y_ref.dtype)

    if emit_state:
        @pl.when(pos == pl.num_programs(1) - 1)
        def _():
            for k in range(STREAMS):
                sfin_ref[k] = s_ref[k].T


def _ssd_scan(xbc, dt, dt_bias, a_log, init, extra, *, reverse, nseq, emit_state):
    final = extra is not None
    has_init = init is not None
    assert nseq % STREAMS == 0 and (not has_init or nseq == STREAMS)
    nc = xbc.shape[0] // (nseq * SSD_CHUNK)
    npairs = nseq // STREAMS
    view = lambda a: a.reshape(npairs, STREAMS, nc, SSD_CHUNK, a.shape[-1])
    chunk = lambda width: pl.BlockSpec((None, STREAMS, None, SSD_CHUNK, width),
                                       lambda p, t: (p, 0, (nc - 1 - t) if reverse else t, 0, 0))
    widen = jnp.tile(jnp.repeat(jnp.eye(SSD_HEADS, dtype=BF16), SSD_HEADDIM, axis=1), (3, 1))
    in_specs = [chunk(SSD_XBC_W), chunk(2 * SSD_HEADS),
                _const_spec((1, 2 * SSD_HEADS)), _const_spec((1, 2 * SSD_HEADS))]
    args = [view(xbc), view(dt), dt_bias, a_log]
    if has_init:
        in_specs.append(_const_spec(init.shape))
        args.append(init)
    in_specs.append(_const_spec(widen.shape))
    args.append(widen)
    scratch = [pltpu.VMEM((STREAMS, SSD_STATE, SSD_DI), F32)]
    if final:
        y_other, z, d_exp, norm_g = extra
        in_specs += [chunk(SSD_DI), chunk(SSD_DI), _const_spec((1, SSD_DI)), _const_spec((1, SSD_DI))]
        args += [view(y_other), view(z), d_exp, norm_g]
        scratch.append(pltpu.VMEM((STREAMS, SSD_CHUNK, SSD_DI), F32))
    out_specs = [chunk(SSD_DI)]
    out_shape = [jax.ShapeDtypeStruct((npairs, STREAMS, nc, SSD_CHUNK, SSD_DI), BF16 if final else F32)]
    if emit_state:
        out_specs.append(pl.BlockSpec((None, STREAMS, SSD_DI, SSD_STATE), lambda p, t: (p, 0, 0, 0)))
        out_shape.append(jax.ShapeDtypeStruct((npairs, STREAMS, SSD_DI, SSD_STATE), F32))
    outs = pl.pallas_call(
        functools.partial(_ssd_kernel, reverse=reverse, final=final, has_init=has_init, emit_state=emit_state),
        grid=(npairs, nc),
        in_specs=in_specs,
        out_specs=out_specs,
        out_shape=out_shape,
        scratch_shapes=scratch,
        compiler_params=_cparams("arbitrary", "arbitrary"),
        name="ssd_scan_bwd" if reverse else "ssd_scan_fwd",
    )(*args)
    y = outs[0].reshape(xbc.shape[0], SSD_DI)
    return (y, outs[1].reshape(nseq, SSD_DI, SSD_STATE)) if emit_state else y


def kernel(x_prompt, x_sample, cache_k, cache_v, state_ssm_fwd, state_ssm_bwd, c, c_ctx, ada_w, ada_b, norm_mix_g, norm_ffn_g, att_w_in, att_lambda, att_subln_g, pool_w, pool_scale, att_w_out, ssd_w_in, ssd_conv_w, ssd_conv_b, ssd_dt_bias, ssd_a_log, ssd_d, ssd_norm_g, ssd_w_out, ffn_w_up, ffn_conv_w, ffn_conv_b, ffn_w_down, final_norm_g):
    x = (x_prompt.reshape(ROWS_P, D_MODEL), x_sample.reshape(ROWS_S, D_MODEL))
    cvec = jnp.concatenate([c_ctx[None], c, jnp.zeros((MOD_PAD - N_MOD, D_MODEL), F32)], axis=0).T
    mods_all = _adaln(cvec, ada_w, ada_b).reshape(DEPTH, MOD_PAD, 6, D_MODEL)
    cos, sin = _rope_tables()
    ffn_up, ffn_down = ffn_w_up.astype(BF16), ffn_w_down.astype(BF16)
    outs = {}
    for l in range(DEPTH):
        i = l // 2
        mods = mods_all[l, :N_MOD]
        if l % 2 == 0:
            lam_init = 0.8 - 0.6 * math.exp(-0.3 * l)
            if not isinstance(x, tuple):
                x = (x[:ROWS_P], x[ROWS_P:])
            q, kt_p, k_s, v_p, v_s, xpool = _att_in(*x, mods, norm_mix_g[l][None], att_w_in[i].astype(BF16), cos, sin)
            ckt = cache_k[:, i].transpose(0, 2, 3, 4, 1).reshape(DEC_BATCH, DA_HEADS, 2 * DA_HD, PAST_LEN)
            cv = cache_v[:, i].reshape(DEC_BATCH * PAST_LEN, DA_V_W)
            att = functools.partial(_attention, lam_vecs=att_lambda[i], subln_g=att_subln_g[i][None],
                                    lam_init=lam_init)
            o_p = att(q, [], kt_p, [v_p], batch=BATCH, lq=SEQ, tq=SEQ, q_row0=0, heads_per_step=DA_HEADS)
            o_s = att(q, [k_s], ckt, [v_s, cv], batch=DEC_BATCH, lq=DEC_SEQ, tq=TQ, q_row0=ROWS_P, heads_per_step=DA_HEADS)
            w_out = att_w_out[i].astype(BF16)
            x = _mix_out(x, mods, (o_p, o_s), xpool, pool_w[i].astype(BF16), pool_scale[i][None],
                         w_out[:DA_V_W], w_out[DA_V_W:])
            outs.setdefault("k", []).append(
                kt_p.reshape(BATCH, DA_HEADS, 2, DA_HD, SEQ).transpose(0, 4, 1, 2, 3))
            outs.setdefault("v", []).append(v_p.reshape(BATCH, SEQ, DA_HEADS, 2 * DA_HD))
        else:
            w_in = ssd_w_in[i]
            z, xbc, dt = _ssd_in(x, mods, norm_mix_g[l][None], w_in[:, :SSD_DI].astype(BF16),
                                 w_in[:, SSD_DI:SSD_DI + SSD_XBC_W].astype(BF16),
                                 w_in[:, SSD_DI + SSD_XBC_W:].astype(BF16),
                                 ssd_conv_w[i], ssd_conv_b[i][None])
            to_scan_layout = lambda s: s.transpose(0, 3, 1, 2).reshape(DEC_BATCH, SSD_STATE, SSD_DI)
            dtb, alog = ssd_dt_bias[i].reshape(1, -1), ssd_a_log[i].reshape(1, -1)
            d_exp, norm_g = jnp.repeat(ssd_d[i], SSD_HEADDIM)[None], ssd_norm_g[i][None]
            scan_p = functools.partial(_ssd_scan, xbc[0], dt[0], dtb, alog, None, nseq=BATCH, emit_state=True)
            y_f, s_f = scan_p(None, reverse=False)
            y_p, s_b = scan_p((y_f, z[0], d_exp, norm_g), reverse=True)
            scan_s = functools.partial(_ssd_scan, xbc[1], dt[1], dtb, alog, nseq=DEC_BATCH, emit_state=False)
            y_f = scan_s(to_scan_layout(state_ssm_fwd[:, i]), None, reverse=False)
            y_s = scan_s(to_scan_layout(state_ssm_bwd[:, i]), (y_f, z[1], d_exp, norm_g), reverse=True)
            x = _proj_res(x, mods, [(y_p, y_s)], [ssd_w_out[i].astype(BF16)], gate_row=2)
            outs.setdefault("sf", []).append(s_f.reshape(BATCH, SSD_HEADS, SSD_HEADDIM, SSD_STATE))
            outs.setdefault("sb", []).append(s_b.reshape(BATCH, SSD_HEADS, SSD_HEADDIM, SSD_STATE))
        x = _conv_ffn(x, mods, norm_ffn_g[l][None], ffn_up, ffn_conv_w[l], ffn_conv_b[l][None],
                      ffn_down, final_norm_g[None], layer=l, last_layer=(l == DEPTH - 1))
    y_prompt = x[0].reshape(BATCH, SEQ, D_MODEL)
    y_sample = x[1].reshape(DEC_BATCH, DEC_SEQ, D_MODEL)
    return (y_prompt, y_sample, jnp.stack(outs["k"], axis=1), jnp.stack(outs["v"], axis=1),
            jnp.stack(outs["sf"], axis=1), jnp.stack(outs["sb"], axis=1))
```

```python
import functools
import math

import jax
import jax.numpy as jnp
from jax import lax
from jax.experimental import pallas as pl
from jax.experimental.pallas import tpu as pltpu

F32 = jnp.float32
BF16 = jnp.bfloat16
HIGHEST = lax.Precision.HIGHEST

D_MODEL = 1024
BATCH = 32
SEQ = 256
DEPTH = 2
DEC_BATCH = 2
DEC_SEQ = 2048
PAST_LEN = 512
GRID_W = 64
EPS = 1e-6
DA_HEADS = 4
DA_HD = 64
DA_QK_W = DA_HEADS * 2 * DA_HD
DA_V_W = DA_HEADS * 2 * DA_HD
POOL_W = D_MODEL - DA_V_W
POOL_WINDOWS = (2, 4, 8, 16)
POOL_GROUPS = 4
POOL_GC = POOL_W // POOL_GROUPS
ATT_IN_W = 2 * DA_QK_W + DA_V_W + POOL_W
LOG2_E = math.log2(math.e)
ROPE_THETA = 10000.0
ROPE_NF = DA_HD // 4
SSD_DI = 2 * D_MODEL
SSD_HEADDIM = 64
SSD_HEADS = SSD_DI // SSD_HEADDIM
SSD_GROUPS = 4
SSD_STATE = 128
SSD_CHUNK = 128
SSD_BC_W = SSD_GROUPS * SSD_STATE
SSD_XBC_W = SSD_DI + 2 * SSD_BC_W
D_FF = 2816

ROWS_P = BATCH * SEQ
ROWS_S = DEC_BATCH * DEC_SEQ
ROWS = ROWS_P + ROWS_S
N_MOD = 1 + DEC_BATCH
MOD_PAD = 8

LANES = 128
SUBLANES = 8
HALO = 2 * SUBLANES
VMEM_LIMIT = 56 * 1024 * 1024

TM = 512
TM_CONV = 512
TF = 256
TQ = 256


def _cparams(*sem):
    return pltpu.CompilerParams(dimension_semantics=sem, vmem_limit_bytes=VMEM_LIMIT)


def _mod_index(i, tm):
    n_p, per_seq = ROWS_P // tm, DEC_SEQ // tm
    return jnp.where(i < n_p, 0, 1 + (i - n_p) // per_seq)


def _silu(x):
    return x / (1.0 + jnp.exp2(x * -LOG2_E))


def _modulate(x, g, shift, scale):
    ms = jnp.mean(x * x, axis=-1, keepdims=True)
    y = x * lax.rsqrt(ms + EPS) * g
    return y * (1.0 + scale) + shift


def _const_spec(shape):
    nd = len(shape)
    return pl.BlockSpec(shape, lambda *_: (0,) * nd, pipeline_mode=pl.Buffered(1))


def _layer_spec(stacked_shape, layer):
    nd = len(stacked_shape) - 1
    return pl.BlockSpec((None, *stacked_shape[1:]), lambda *_: (layer,) + (0,) * nd, pipeline_mode=pl.Buffered(1))


def _group_specs(tm, width):
    n_p = ROWS_P // tm
    return (pl.BlockSpec((tm, width), lambda i: (jnp.minimum(i, n_p - 1), 0)),
            pl.BlockSpec((tm, width), lambda i: (jnp.maximum(i - n_p, 0), 0)))


def _group_rows(i, tm, p_ref, s_ref):
    return jnp.where(i < ROWS_P // tm, p_ref[...], s_ref[...])


def _halo_specs(tm, width):
    per = tm // SUBLANES
    last = ROWS // SUBLANES - 1
    prev = pl.BlockSpec((SUBLANES, width), lambda i: (jnp.maximum(i * per - 1, 0), 0))
    nxt = pl.BlockSpec((SUBLANES, width), lambda i: (jnp.minimum((i + 1) * per, last), 0))
    return prev, nxt


def _fill_with_halo(h_ref, x_ref, xp_ref, xn_ref, modulate, i, tm):
    m = jnp.where(i < ROWS_P // tm, SEQ - 1, DEC_SEQ - 1)
    starts = ((i * tm) & m) == 0
    ends = (((i + 1) * tm) & m) == 0
    h_ref[0:tm] = modulate(x_ref[...]).astype(BF16)
    halo = jnp.concatenate([jnp.where(ends, 0.0, modulate(xn_ref[...])),
                            jnp.where(starts, 0.0, modulate(xp_ref[...]))], axis=0)
    h_ref[tm:tm + HALO] = halo.astype(BF16)


def _conv3(u, cw, cb, tm, prompt):
    n = u.shape[0]
    up = pltpu.roll(u, 1, axis=0)[0:tm]
    un = pltpu.roll(u, n - 1, axis=0)[0:tm]
    if tm > SEQ and prompt is not False:
        s = SUBLANES
        sub = lax.broadcasted_iota(jnp.int32, (s, u.shape[1]), 0)
        at_edge = (lambda m: m) if prompt is True else (lambda m: prompt & m)
        ups, uns, lo = [], [], 0
        for edge in range(SEQ, tm, SEQ):
            ups += [up[lo:edge], jnp.where(at_edge(sub == 0), 0.0, up[edge:edge + s])]
            uns += [un[lo - s if lo else 0:edge - s], jnp.where(at_edge(sub == s - 1), 0.0, un[edge - s:edge])]
            lo = edge + s
        up = jnp.concatenate(ups + [up[lo:tm]], axis=0)
        un = jnp.concatenate(uns + [un[lo - s:tm]], axis=0)
    return cw[1:2] * u[0:tm] + cb + cw[0:1] * up + cw[2:3] * un


def _adaln_kernel(ct_ref, w_ref, b_ref, o_ref):
    st = _silu(ct_ref[...])
    w = w_ref[0]
    rows = [jnp.sum(w * st[:, r:r + 1], axis=0, keepdims=True) for r in range(N_MOD)]
    rows.append(jnp.zeros((MOD_PAD - N_MOD, w.shape[1]), F32))
    o_ref[0] = jnp.concatenate(rows, axis=0) + b_ref[0]


def _adaln(cvec, ada_w, ada_b):
    tn = 1536
    n = 6 * D_MODEL
    return pl.pallas_call(
        _adaln_kernel,
        grid=(DEPTH, n // tn),
        in_specs=[_const_spec((D_MODEL, MOD_PAD)),
                  pl.BlockSpec((1, D_MODEL, tn), lambda l, j: (l, 0, j)),
                  pl.BlockSpec((1, 1, tn), lambda l, j: (l, 0, j))],
        out_specs=pl.BlockSpec((1, MOD_PAD, tn), lambda l, j: (l, 0, j)),
        out_shape=jax.ShapeDtypeStruct((DEPTH, MOD_PAD, n), F32),
        compiler_params=_cparams("parallel", "parallel"),
        name="adaln",
    )(cvec, ada_w, ada_b.reshape(DEPTH, 1, n))


def _rope(x, cos, sin_signed):
    lane = lax.broadcasted_iota(jnp.int32, cos.shape, 1)
    lower = (lane & 31) < ROPE_NF
    out = []
    for s in range(x.shape[1] // LANES):
        xs = x[:, s * LANES:(s + 1) * LANES]
        partner = jnp.where(lower, pltpu.roll(xs, LANES - ROPE_NF, axis=1), pltpu.roll(xs, ROPE_NF, axis=1))
        out.append(xs * cos + partner * sin_signed)
    return jnp.concatenate(out, axis=1)


def _att_in_kernel(xp_ref, xs_ref, mod_ref, g_ref, w_ref, cos_ref, sin_ref,
                   q_ref, kpt_ref, ks_ref, vp_ref, vs_ref, p_ref):
    i = pl.program_id(0)
    x = _group_rows(i, TM, xp_ref, xs_ref)
    h = _modulate(x, g_ref[...], mod_ref[0, 0:1], mod_ref[0, 1:2]).astype(BF16)
    hw = 2 * DA_HD

    def proj(col0, width):
        return jnp.dot(h, w_ref[:, col0:col0 + width], preferred_element_type=F32)

    q0, k0, v0, p0 = 0, DA_QK_W, 2 * DA_QK_W, 2 * DA_QK_W + DA_V_W
    p_ref[...] = proj(p0, POOL_W)
    latent = i >= ROWS_P // TM

    @pl.when(latent)
    def _():
        vs_ref[...] = proj(v0, DA_V_W)
        for c0 in range(0, DA_QK_W, 2 * hw):
            sl = slice(c0, c0 + 2 * hw)
            q_ref[:, sl] = _rope(proj(q0 + c0, 2 * hw), cos_ref[...], sin_ref[...])
            ks_ref[:, sl] = _rope(proj(k0 + c0, 2 * hw), cos_ref[...], sin_ref[...])

    @pl.when(jnp.logical_not(latent))
    def _():
        q_ref[...] = proj(q0, DA_QK_W)
        vp_ref[...] = proj(v0, DA_V_W)
        for hd in range(0, DA_HEADS, 2):
            k = proj(k0 + hd * hw, 2 * hw)
            for s in range(TM // SEQ):
                for j in range(2):
                    kpt_ref[s, hd + j] = k[s * SEQ:(s + 1) * SEQ, j * hw:(j + 1) * hw].T


def _rope_tables():
    t = jnp.arange(DEC_SEQ, dtype=F32)
    r, col = jnp.floor(t / GRID_W), t % GRID_W
    inv = ROPE_THETA ** (-jnp.arange(ROPE_NF, dtype=F32) / ROPE_NF)
    ar, ac = r[:, None] * inv, col[:, None] * inv
    cos = jnp.concatenate([jnp.cos(ar), jnp.cos(ar), jnp.cos(ac), jnp.cos(ac)], axis=1)
    sin = jnp.concatenate([-jnp.sin(ar), jnp.sin(ar), -jnp.sin(ac), jnp.sin(ac)], axis=1)
    return jnp.tile(cos, (1, 2)), jnp.tile(sin, (1, 2))


def _att_in(xp, xs, mods, g, w, cos, sin):
    n_p, per_seq = ROWS_P // TM, DEC_SEQ // TM
    tab = pl.BlockSpec((TM, LANES), lambda i: (jnp.maximum(i - n_p, 0) % per_seq, 0))
    out = pl.BlockSpec((TM, DA_QK_W), lambda i: (i, 0))
    out_p, out_s = _group_specs(TM, DA_QK_W)
    full, grp_p, grp_s = (jax.ShapeDtypeStruct((r, DA_QK_W), F32) for r in (ROWS, ROWS_P, ROWS_S))
    seqs = TM // SEQ
    out_kt = pl.BlockSpec((seqs, DA_HEADS, 2 * DA_HD, SEQ), lambda i: (jnp.minimum(i, n_p - 1), 0, 0, 0))
    kt = jax.ShapeDtypeStruct((BATCH, DA_HEADS, 2 * DA_HD, SEQ), F32)
    return pl.pallas_call(
        _att_in_kernel,
        grid=(ROWS // TM,),
        in_specs=[*_group_specs(TM, D_MODEL),
                  pl.BlockSpec((1, 6, D_MODEL), lambda i: (_mod_index(i, TM), 0, 0)),
                  _const_spec((1, D_MODEL)),
                  _const_spec((D_MODEL, ATT_IN_W)),
                  tab, tab],
        out_specs=[out, out_kt, out_s, out_p, out_s, out],
        out_shape=[full, kt, grp_s, grp_p, grp_s, full],
        compiler_params=_cparams("arbitrary"),
        name="att_in",
    )(xp, xs, mods, g, w, cos, sin)


def _attn_kernel(lam_ref, g_ref, q_ref, *rest, n_row_keys, n_values, lam_init):
    k_refs, kt_ref = rest[:n_row_keys], rest[n_row_keys]
    v_refs, o_ref = rest[n_row_keys + 1:n_row_keys + 1 + n_values], rest[-1]
    lv = lam_ref[...]
    lam = (jnp.exp(jnp.sum(lv[0:1] * lv[1:2], keepdims=True))
           - jnp.exp(jnp.sum(lv[2:3] * lv[3:4], keepdims=True)) + lam_init)
    dn = (((1,), (1,)), ((), ()))
    hw = 2 * DA_HD
    lane = lax.broadcasted_iota(jnp.int32, (q_ref.shape[0], hw), 1)
    for h in range(q_ref.shape[1] // hw):
        sl = slice(h * hw, (h + 1) * hw)
        q = q_ref[:, sl] * (DA_HD ** -0.5 * LOG2_E)
        kbs = [k_ref[:, sl].astype(BF16) for k_ref in k_refs]
        ktb = kt_ref[h].astype(BF16)
        vb = jnp.concatenate([v_ref[:, sl].astype(BF16) for v_ref in v_refs], axis=0)

        def unnormalised(qc):
            qb = qc.astype(BF16)
            s = jnp.concatenate([lax.dot_general(qb, kb, dn, preferred_element_type=F32) for kb in kbs]
                                + [jnp.dot(qb, ktb, preferred_element_type=F32)], axis=1)
            e = jnp.exp2(s - jnp.max(s, axis=-1, keepdims=True))
            return (jnp.dot(e.astype(BF16), vb, preferred_element_type=F32),
                    jnp.sum(e, axis=-1, keepdims=True))

        o1, l1 = unnormalised(jnp.where(lane < DA_HD, q, 0.0))
        o2, l2 = unnormalised(jnp.where(lane < DA_HD, 0.0, q))
        o = o1 / l1 - o2 * (lam / l2)
        ms = jnp.mean(o * o, axis=-1, keepdims=True)
        o_ref[:, sl] = (o * lax.rsqrt(ms + EPS) * g_ref[...] * (1.0 - lam_init)).astype(o_ref.dtype)


def _attention(q, row_keys, keys_t, values, lam_vecs, subln_g, *, batch, lq, tq, q_row0, heads_per_step, lam_init):
    nq = lq // tq
    q0 = q_row0 // tq
    hw = 2 * DA_HD
    bw = heads_per_step * hw
    seq_block = lambda a: pl.BlockSpec((a.shape[0] // batch, bw), lambda b, h, i: (b, h))
    return pl.pallas_call(
        functools.partial(_attn_kernel, n_row_keys=len(row_keys), n_values=len(values), lam_init=lam_init),
        grid=(batch, DA_HEADS // heads_per_step, nq),
        in_specs=[_const_spec((4, DA_HD)),
                  _const_spec((1, hw)),
                  pl.BlockSpec((tq, bw), lambda b, h, i: (q0 + b * nq + i, h))]
                 + [seq_block(a) for a in row_keys]
                 + [pl.BlockSpec((None, heads_per_step, hw, keys_t.shape[-1]), lambda b, h, i: (b, h, 0, 0))]
                 + [seq_block(a) for a in values],
        out_specs=pl.BlockSpec((tq, bw), lambda b, h, i: (b * nq + i, h)),
        out_shape=jax.ShapeDtypeStruct((batch * lq, DA_V_W), BF16),
        compiler_params=_cparams("parallel", "parallel", "parallel"),
        name="diff_attn",
    )(lam_vecs, subln_g, q, *row_keys, keys_t, *values)


assert max(POOL_WINDOWS) // 2 <= SUBLANES


def _pooled(ext, pos_ext, seq_len, w_ref, sc_ref, tm):
    n = ext.shape[0]
    masks = {}

    def shifted(a, k):
        if k not in masks:
            masks[k] = (pos_ext + k < seq_len) if k > 0 else (pos_ext + k >= 0)
        return jnp.where(masks[k], pltpu.roll(a, (-k) % n, axis=0), 0.0)

    pos = pos_ext[0:tm]
    out = []
    for gi, win in enumerate(POOL_WINDOWS):
        half = win // 2
        sl = slice(gi * POOL_GC, (gi + 1) * POOL_GC)
        xg = ext[:, sl]
        right, left, step = xg, xg, 1
        while step < half:
            right, left, step = right + shifted(right, step), left + shifted(left, -step), 2 * step
        acc = (right + shifted(left, -1))[0:tm]
        cnt = (jnp.minimum(pos + half, seq_len) - jnp.maximum(pos - half, 0)).astype(F32)
        pooled = acc / cnt - xg[0:tm]
        y = jnp.dot(pooled.astype(BF16), w_ref[gi], preferred_element_type=F32)
        out.append((y * sc_ref[:, sl]).astype(BF16))
    return jnp.concatenate(out, axis=1)


def _mix_out_kernel(mod_ref, xp_ref, xs_ref, op_ref, os_ref, pool_ref, poolp_ref, pooln_ref,
                    pw_ref, ps_ref, wa_ref, wb_ref, o_ref):
    i = pl.program_id(0)
    prompt = i < ROWS_P // TM
    seq_len = jnp.where(prompt, SEQ, DEC_SEQ)
    ext = jnp.concatenate([pool_ref[...], pooln_ref[...], poolp_ref[...]], axis=0)
    n = ext.shape[0]
    r = lax.broadcasted_iota(jnp.int32, (n, POOL_GC), 0)
    r = jnp.where(r < TM + SUBLANES, r, r - n)
    pooled = _pooled(ext, (i * TM + r) & (seq_len - 1), seq_len, pw_ref, ps_ref, TM)
    att = _group_rows(i, TM, op_ref, os_ref)
    acc = (jnp.dot(att, wa_ref[...], preferred_element_type=F32)
           + jnp.dot(pooled, wb_ref[...], preferred_element_type=F32))
    o_ref[...] = _group_rows(i, TM, xp_ref, xs_ref) + mod_ref[0, 2:3] * acc


def _mix_out(x, mods, att, xpool, pool_w, pool_scale, w_att, w_pool):
    prev, nxt = _halo_specs(TM, POOL_W)
    return pl.pallas_call(
        _mix_out_kernel,
        grid=(ROWS // TM,),
        in_specs=[pl.BlockSpec((1, 6, D_MODEL), lambda i: (_mod_index(i, TM), 0, 0)),
                  *_group_specs(TM, D_MODEL), *_group_specs(TM, DA_V_W),
                  pl.BlockSpec((TM, POOL_W), lambda i: (i, 0)), prev, nxt,
                  _const_spec(pool_w.shape), _const_spec(pool_scale.shape),
                  _const_spec(w_att.shape), _const_spec(w_pool.shape)],
        out_specs=pl.BlockSpec((TM, D_MODEL), lambda i: (i, 0)),
        out_shape=jax.ShapeDtypeStruct((ROWS, D_MODEL), F32),
        compiler_params=_cparams("parallel"),
        name="mix_out",
    )(mods, *x, *att, xpool, xpool, xpool, pool_w, pool_scale, w_att, w_pool)


def _proj_res_kernel(*refs, gate_row, grouped):
    i = pl.program_id(0)
    mod_ref, o_ref = refs[0], refs[-1]
    pos, rows = 1, []
    for is_pair in grouped:
        rows.append(_group_rows(i, TM, refs[pos], refs[pos + 1]) if is_pair else refs[pos][...])
        pos += 2 if is_pair else 1
    acc = None
    for a, w_ref in zip(rows[1:], refs[pos:-1]):
        d = jnp.dot(a.astype(BF16), w_ref[...], preferred_element_type=F32)
        acc = d if acc is None else acc + d
    o_ref[...] = rows[0] + mod_ref[0, gate_row:gate_row + 1] * acc


def _proj_res(x, mods, acts, ws, *, gate_row):
    operands, specs, grouped = [], [], []
    for a in (x, *acts):
        is_pair = isinstance(a, tuple)
        grouped.append(is_pair)
        if is_pair:
            operands += list(a)
            specs += list(_group_specs(TM, a[0].shape[1]))
        else:
            operands.append(a)
            specs.append(pl.BlockSpec((TM, a.shape[1]), lambda i: (i, 0)))
    return pl.pallas_call(
        functools.partial(_proj_res_kernel, gate_row=gate_row, grouped=tuple(grouped)),
        grid=(ROWS // TM,),
        in_specs=[pl.BlockSpec((1, 6, D_MODEL), lambda i: (_mod_index(i, TM), 0, 0))]
                 + specs + [_const_spec(w.shape) for w in ws],
        out_specs=pl.BlockSpec((TM, D_MODEL), lambda i: (i, 0)),
        out_shape=jax.ShapeDtypeStruct((ROWS, D_MODEL), F32),
        compiler_params=_cparams("parallel"),
        name="proj_res",
    )(mods, *operands, *ws)


def _ffn_kernel(x_ref, xp_ref, xn_ref, mod_ref, g_ref, wup_ref, cw_ref, cb_ref, wd_ref, fg_ref,
                *rest, last_layer):
    h_ref, act_ref = rest[-2:]
    i = pl.program_id(0)
    tm = TM_CONV
    modulate = functools.partial(_modulate, g=g_ref[...], shift=mod_ref[0, 3:4], scale=mod_ref[0, 4:5])
    _fill_with_halo(h_ref, x_ref, xp_ref, xn_ref, modulate, i, tm)
    h = h_ref[...]
    prompt = i < ROWS_P // tm
    for c in range(D_FF // TF):
        gs, vs = slice(c * TF, (c + 1) * TF), slice(D_FF + c * TF, D_FF + (c + 1) * TF)
        ug = jnp.dot(h, wup_ref[:, gs], preferred_element_type=F32)
        uv = jnp.dot(h, wup_ref[:, vs], preferred_element_type=F32)
        cg = _conv3(ug, cw_ref[:, gs], cb_ref[:, gs], tm, prompt)
        cv = _conv3(uv, cw_ref[:, vs], cb_ref[:, vs], tm, prompt)
        act_ref[:, gs] = (_silu(cg) * cv).astype(BF16)
    y = x_ref[...] + mod_ref[0, 5:6] * jnp.dot(act_ref[...], wd_ref[...], preferred_element_type=F32)
    if not last_layer:
        rest[0][...] = y
        return
    ms = jnp.mean(y * y, axis=-1, keepdims=True)
    y = y * lax.rsqrt(ms + EPS) * fg_ref[...]
    op_ref, os_ref = rest[:2]

    @pl.when(prompt)
    def _():
        op_ref[...] = y

    @pl.when(jnp.logical_not(prompt))
    def _():
        os_ref[...] = y


def _conv_ffn(x, mods, g, w_up, conv_w, conv_b, w_down, final_g, *, layer, last_layer):
    tm = TM_CONV
    prev, nxt = _halo_specs(tm, D_MODEL)
    if last_layer:
        out_specs = list(_group_specs(tm, D_MODEL))
        out_shape = [jax.ShapeDtypeStruct((ROWS_P, D_MODEL), F32), jax.ShapeDtypeStruct((ROWS_S, D_MODEL), F32)]
    else:
        out_specs = pl.BlockSpec((tm, D_MODEL), lambda i: (i, 0))
        out_shape = jax.ShapeDtypeStruct((ROWS, D_MODEL), F32)
    return pl.pallas_call(
        functools.partial(_ffn_kernel, last_layer=last_layer),
        grid=(ROWS // tm,),
        in_specs=[pl.BlockSpec((tm, D_MODEL), lambda i: (i, 0)), prev, nxt,
                  pl.BlockSpec((1, 6, D_MODEL), lambda i: (_mod_index(i, tm), 0, 0)),
                  _const_spec((1, D_MODEL)),
                  _layer_spec(w_up.shape, layer), _const_spec(conv_w.shape), _const_spec(conv_b.shape),
                  _layer_spec(w_down.shape, layer), _const_spec((1, D_MODEL))],
        out_specs=out_specs,
        out_shape=out_shape,
        scratch_shapes=[pltpu.VMEM((tm + HALO, D_MODEL), BF16), pltpu.VMEM((tm, D_FF), BF16)],
        compiler_params=_cparams("arbitrary"),
        name="conv_ffn",
    )(x, x, x, mods, g, w_up, conv_w, conv_b, w_down, final_g)


XBC_COLS = 256


XBC_ROWS = 64


def _ssd_xbc_kernel(x_ref, xp_ref, xn_ref, mod_ref, g_ref, wx_ref, cw_ref, cb_ref, xbcp_ref, xbcs_ref,
                    h_ref, u_ref):
    i = pl.program_id(0)
    tm = TM_CONV
    s, rb = SUBLANES, XBC_ROWS
    modulate = functools.partial(_modulate, g=g_ref[...], shift=mod_ref[0, 0:1], scale=mod_ref[0, 1:2])
    _fill_with_halo(h_ref, x_ref, xp_ref, xn_ref, modulate, i, tm)

    def project(xbc_ref, is_prompt):
        h = h_ref[...]
        sub = lax.broadcasted_iota(jnp.int32, (s, XBC_COLS), 0)
        for c in range(SSD_XBC_W // XBC_COLS):
            sl = slice(c * XBC_COLS, (c + 1) * XBC_COLS)
            u = jnp.dot(h, wx_ref[:, sl], preferred_element_type=F32)
            uc_ref = u_ref.at[c % 2]
            uc_ref[0:s] = u[tm + s:tm + 2 * s]
            uc_ref[s:tm + 2 * s] = u[0:tm + s]
            cw, cb = cw_ref[:, sl], cb_ref[:, sl]
            for r0 in range(0, tm, rb):
                blk = uc_ref[r0:r0 + rb + 2 * s]
                up = pltpu.roll(blk, 1, axis=0)[s:s + rb]
                un = pltpu.roll(blk, rb + 2 * s - 1, axis=0)[s:s + rb]
                if is_prompt and r0 and r0 % SEQ == 0:
                    up = jnp.concatenate([jnp.where(sub == 0, 0.0, up[0:s]), up[s:]], axis=0)
                if is_prompt and r0 + rb < tm and (r0 + rb) % SEQ == 0:
                    un = jnp.concatenate([un[:rb - s], jnp.where(sub == s - 1, 0.0, un[rb - s:])], axis=0)
                xbc_ref[r0:r0 + rb, sl] = _silu(cw[1:2] * blk[s:s + rb] + cb + cw[0:1] * up + cw[2:3] * un)

    prompt = i < ROWS_P // tm
    pl.when(prompt)(lambda: project(xbcp_ref, True))
    pl.when(jnp.logical_not(prompt))(lambda: project(xbcs_ref, False))


def _ssd_zdt_kernel(x_ref, mod_ref, g_ref, wz_ref, wdt_ref, zp_ref, zs_ref, dtp_ref, dts_ref):
    i = pl.program_id(0)
    h = _modulate(x_ref[...], g_ref[...], mod_ref[0, 0:1], mod_ref[0, 1:2]).astype(BF16)

    def project(z_ref, dt_ref):
        z_ref[...] = _silu(jnp.dot(h, wz_ref[...], preferred_element_type=F32))
        dt_ref[...] = jnp.dot(h, wdt_ref[...], preferred_element_type=F32)

    prompt = i < ROWS_P // TM
    pl.when(prompt)(lambda: project(zp_ref, dtp_ref))
    pl.when(jnp.logical_not(prompt))(lambda: project(zs_ref, dts_ref))


def _ssd_in(x, mods, g, wz, wx, wdt, conv_w, conv_b):
    tm = TM_CONV
    prev, nxt = _halo_specs(tm, D_MODEL)
    mod = lambda t: pl.BlockSpec((1, 6, D_MODEL), lambda i: (_mod_index(i, t), 0, 0))
    group_shapes = lambda w: [jax.ShapeDtypeStruct((r, w), F32) for r in (ROWS_P, ROWS_S)]
    xbc = pl.pallas_call(
        _ssd_xbc_kernel,
        grid=(ROWS // tm,),
        in_specs=[pl.BlockSpec((tm, D_MODEL), lambda i: (i, 0)), prev, nxt, mod(tm), _const_spec((1, D_MODEL)),
                  _const_spec(wx.shape), _const_spec(conv_w.shape), _const_spec(conv_b.shape)],
        out_specs=list(_group_specs(tm, SSD_XBC_W)),
        out_shape=group_shapes(SSD_XBC_W),
        scratch_shapes=[pltpu.VMEM((tm + HALO, D_MODEL), BF16), pltpu.VMEM((2, tm + HALO, XBC_COLS), F32)],
        compiler_params=_cparams("arbitrary"),
        name="ssd_xbc",
    )(x, x, x, mods, g, wx, conv_w, conv_b)
    zp, zs, dtp, dts = pl.pallas_call(
        _ssd_zdt_kernel,
        grid=(ROWS // TM,),
        in_specs=[pl.BlockSpec((TM, D_MODEL), lambda i: (i, 0)), mod(TM), _const_spec((1, D_MODEL)),
                  _const_spec(wz.shape), _const_spec(wdt.shape)],
        out_specs=[*_group_specs(TM, SSD_DI), *_group_specs(TM, 2 * SSD_HEADS)],
        out_shape=group_shapes(SSD_DI) + group_shapes(2 * SSD_HEADS),
        compiler_params=_cparams("arbitrary"),
        name="ssd_zdt",
    )(x, mods, g, wz, wdt)
    return (zp, zs), tuple(xbc), (dtp, dts)


STREAMS = 2


def _split3(v):
    hi = v.astype(BF16)
    r1 = v - hi.astype(F32)
    mid = r1.astype(BF16)
    return hi, mid, (r1 - mid.astype(F32)).astype(BF16)


def _ssd_kernel(*refs, reverse, final, has_init, emit_state):
    refs = list(refs)
    xbc_ref, dt_ref, dtb_ref, alog_ref = refs[:4]
    del refs[:4]
    init_ref = refs.pop(0) if has_init else None
    wide_ref = refs.pop(0)
    if final:
        yin_ref, z_ref, dexp_ref, ng_ref = refs[:4]
        del refs[:4]
    y_ref = refs.pop(0)
    sfin_ref = refs.pop(0) if emit_state else None
    s_ref = refs.pop(0)
    yacc_ref = refs.pop(0) if final else None
    pos = pl.program_id(1)
    q = SSD_CHUNK
    hh = SSD_HEADS
    gw = SSD_DI // SSD_GROUPS

    @pl.when(pos == 0)
    def _():
        s_ref[...] = init_ref[...] if has_init else jnp.zeros_like(s_ref)

    d0 = hh if reverse else 0
    a = -jnp.exp(alog_ref[:, d0:d0 + hh])
    row = lax.broadcasted_iota(jnp.int32, (q, q), 0)
    col = lax.broadcasted_iota(jnp.int32, (q, q), 1)
    tri = (row <= col) if reverse else (row >= col)
    lane_lo = col < SSD_HEADDIM
    dn_t = (((1,), (1,)), ((), ()))

    def chunk_factors(k):
        dt_in = dt_ref[k, :, d0:d0 + hh] + dtb_ref[:, d0:d0 + hh]
        dt = jnp.maximum(dt_in, 0.0) + jnp.log1p(jnp.exp(-jnp.abs(dt_in)))
        acs = jnp.dot(tri.astype(F32), dt * a, precision=HIGHEST, preferred_element_type=F32)
        a_end = acs[0:1] if reverse else acs[q - 1:q]
        factors = jnp.concatenate([jnp.exp(acs), jnp.exp(a_end - acs) * dt,
                                   jnp.broadcast_to(jnp.exp(a_end), (SUBLANES, hh))], axis=0)
        wide = jnp.dot(jnp.concatenate(_split3(factors), axis=1), wide_ref[...], preferred_element_type=F32)
        acs2 = acs * LOG2_E
        mt = jnp.concatenate([acs2 - jnp.log2(dt), jnp.zeros((q, LANES - hh), F32)], axis=1).T
        return wide[0:q], wide[q:2 * q], wide[2 * q:2 * q + 1], acs2, mt

    def group(k, g, eacs_w, wst_w, dec_w, acs2, mt):
        gs = slice(g * gw, (g + 1) * gw)
        b_g = xbc_ref[k, :, SSD_DI + g * SSD_STATE:SSD_DI + (g + 1) * SSD_STATE]
        c_g = xbc_ref[k, :, SSD_DI + SSD_BC_W + g * SSD_STATE:SSD_DI + SSD_BC_W + (g + 1) * SSD_STATE]
        cgb = c_g.astype(BF16)
        cb = lax.dot_general(cgb, b_g.astype(BF16), dn_t, preferred_element_type=F32)
        s_g = s_ref[k, :, gs]
        y_off = jnp.dot(cgb, s_g.astype(BF16), preferred_element_type=F32) * eacs_w[:, gs]
        x_g = xbc_ref[k, :, gs]
        s_new = jnp.dot(b_g.T.astype(BF16), (x_g * wst_w[:, gs]).astype(BF16), preferred_element_type=F32)
        s_ref[k, :, gs] = s_g * dec_w[:, gs] + s_new
        for p in range(gw // LANES):
            ps = slice(p * LANES, (p + 1) * LANES)
            sl = slice(g * gw + p * LANES, g * gw + (p + 1) * LANES)
            xp = x_g[:, ps]
            xpb = xp.astype(BF16)
            zero = jnp.zeros_like(xpb)
            rhs = jnp.concatenate([jnp.where(lane_lo, xpb, zero), jnp.where(lane_lo, zero, xpb)], axis=0)
            w_l = []
            for h in (2 * (g * 4 + p), 2 * (g * 4 + p) + 1):
                seg = jnp.broadcast_to(acs2[:, h:h + 1], (q, q)) - mt[h:h + 1]
                w_l.append((cb * jnp.exp2(jnp.where(tri, seg, -jnp.inf))).astype(BF16))
            yp = jnp.dot(jnp.concatenate(w_l, axis=1), rhs, preferred_element_type=F32) + y_off[:, ps]
            if final:
                yacc_ref[k, :, sl] = yp
            else:
                y_ref[k, :, sl] = yp

    per_stream = [chunk_factors(k) for k in range(STREAMS)]
    for g in range(SSD_GROUPS):
        for k in range(STREAMS):
            group(k, g, *per_stream[k])

    if final:
        for k in range(STREAMS):
            ssq = jnp.zeros((q, 1), F32)
            for g in range(SSD_GROUPS):
                gs = slice(g * gw, (g + 1) * gw)
                yt = (yacc_ref[k, :, gs] + yin_ref[k, :, gs] + xbc_ref[k, :, gs] * dexp_ref[:, gs]) * z_ref[k, :, gs]
                ssq = ssq + jnp.sum(yt * yt, axis=-1, keepdims=True)
                yacc_ref[k, :, gs] = yt
            inv = lax.rsqrt(ssq / SSD_DI + EPS)
            for g in range(SSD_GROUPS):
                gs = slice(g * gw, (g + 1) * gw)
                y_ref[k, :, gs] = (yacc_ref[k, :, gs] * inv * ng_ref[:, gs]).astype(y_ref.dtype)

    if emit_state:
        @pl.when(pos == pl.num_programs(1) - 1)
        def _():
            for k in range(STREAMS):
                sfin_ref[k] = s_ref[k].T


def _ssd_scan(xbc, dt, dt_bias, a_log, init, extra, *, reverse, nseq, emit_state):
    final = extra is not None
    has_init = init is not None
    assert nseq % STREAMS == 0 and (not has_init or nseq == STREAMS)
    nc = xbc.shape[0] // (nseq * SSD_CHUNK)
    npairs = nseq // STREAMS
    view = lambda a: a.reshape(npairs, STREAMS, nc, SSD_CHUNK, a.shape[-1])
    chunk = lambda width: pl.BlockSpec((None, STREAMS, None, SSD_CHUNK, width),
                                       lambda p, t: (p, 0, (nc - 1 - t) if reverse else t, 0, 0))
    widen = jnp.tile(jnp.repeat(jnp.eye(SSD_HEADS, dtype=BF16), SSD_HEADDIM, axis=1), (3, 1))
    in_specs = [chunk(SSD_XBC_W), chunk(2 * SSD_HEADS),
                _const_spec((1, 2 * SSD_HEADS)), _const_spec((1, 2 * SSD_HEADS))]
    args = [view(xbc), view(dt), dt_bias, a_log]
    if has_init:
        in_specs.append(_const_spec(init.shape))
        args.append(init)
    in_specs.append(_const_spec(widen.shape))
    args.append(widen)
    scratch = [pltpu.VMEM((STREAMS, SSD_STATE, SSD_DI), F32)]
    if final:
        y_other, z, d_exp, norm_g = extra
        in_specs += [chunk(SSD_DI), chunk(SSD_DI), _const_spec((1, SSD_DI)), _const_spec((1, SSD_DI))]
        args += [view(y_other), view(z), d_exp, norm_g]
        scratch.append(pltpu.VMEM((STREAMS, SSD_CHUNK, SSD_DI), F32))
    out_specs = [chunk(SSD_DI)]
    out_shape = [jax.ShapeDtypeStruct((npairs, STREAMS, nc, SSD_CHUNK, SSD_DI), BF16 if final else F32)]
    if emit_state:
        out_specs.append(pl.BlockSpec((None, STREAMS, SSD_DI, SSD_STATE), lambda p, t: (p, 0, 0, 0)))
        out_shape.append(jax.ShapeDtypeStruct((npairs, STREAMS, SSD_DI, SSD_STATE), F32))
    outs = pl.pallas_call(
        functools.partial(_ssd_kernel, reverse=reverse, final=final, has_init=has_init, emit_state=emit_state),
        grid=(npairs, nc),
        in_specs=in_specs,
        out_specs=out_specs,
        out_shape=out_shape,
        scratch_shapes=scratch,
        compiler_params=_cparams("arbitrary", "arbitrary"),
        name="ssd_scan_bwd" if reverse else "ssd_scan_fwd",
    )(*args)
    y = outs[0].reshape(xbc.shape[0], SSD_DI)
    return (y, outs[1].reshape(nseq, SSD_DI, SSD_STATE)) if emit_state else y


def kernel(x_prompt, x_sample, cache_k, cache_v, state_ssm_fwd, state_ssm_bwd, c, c_ctx, ada_w, ada_b, norm_mix_g, norm_ffn_g, att_w_in, att_lambda, att_subln_g, pool_w, pool_scale, att_w_out, ssd_w_in, ssd_conv_w, ssd_conv_b, ssd_dt_bias, ssd_a_log, ssd_d, ssd_norm_g, ssd_w_out, ffn_w_up, ffn_conv_w, ffn_conv_b, ffn_w_down, final_norm_g):
    x = (x_prompt.reshape(ROWS_P, D_MODEL), x_sample.reshape(ROWS_S, D_MODEL))
    cvec = jnp.concatenate([c_ctx[None], c, jnp.zeros((MOD_PAD - N_MOD, D_MODEL), F32)], axis=0).T
    mods_all = _adaln(cvec, ada_w, ada_b).reshape(DEPTH, MOD_PAD, 6, D_MODEL)
    cos, sin = _rope_tables()
    ffn_up, ffn_down = ffn_w_up.astype(BF16), ffn_w_down.astype(BF16)
    outs = {}
    for l in range(DEPTH):
        i = l // 2
        mods = mods_all[l, :N_MOD]
        if l % 2 == 0:
            lam_init = 0.8 - 0.6 * math.exp(-0.3 * l)
            if not isinstance(x, tuple):
                x = (x[:ROWS_P], x[ROWS_P:])
            q, kt_p, k_s, v_p, v_s, xpool = _att_in(*x, mods, norm_mix_g[l][None], att_w_in[i].astype(BF16), cos, sin)
            ckt = cache_k[:, i].transpose(0, 2, 3, 4, 1).reshape(DEC_BATCH, DA_HEADS, 2 * DA_HD, PAST_LEN)
            cv = cache_v[:, i].reshape(DEC_BATCH * PAST_LEN, DA_V_W)
            att = functools.partial(_attention, lam_vecs=att_lambda[i], subln_g=att_subln_g[i][None],
                                    lam_init=lam_init)
            o_p = att(q, [], kt_p, [v_p], batch=BATCH, lq=SEQ, tq=SEQ, q_row0=0, heads_per_step=DA_HEADS)
            o_s = att(q, [k_s], ckt, [v_s, cv], batch=DEC_BATCH, lq=DEC_SEQ, tq=TQ, q_row0=ROWS_P, heads_per_step=DA_HEADS)
            w_out = att_w_out[i].astype(BF16)
            x = _mix_out(x, mods, (o_p, o_s), xpool, pool_w[i].astype(BF16), pool_scale[i][None],
                         w_out[:DA_V_W], w_out[DA_V_W:])
            outs.setdefault("k", []).append(
                kt_p.reshape(BATCH, DA_HEADS, 2, DA_HD, SEQ).transpose(0, 4, 1, 2, 3))
            outs.setdefault("v", []).append(v_p.reshape(BATCH, SEQ, DA_HEADS, 2 * DA_HD))
        else:
            w_in = ssd_w_in[i]
            z, xbc, dt = _ssd_in(x, mods, norm_mix_g[l][None], w_in[:, :SSD_DI].astype(BF16),
                                 w_in[:, SSD_DI:SSD_DI + SSD_XBC_W].astype(BF16),
                                 w_in[:, SSD_DI + SSD_XBC_W:].astype(BF16),
                                 ssd_conv_w[i], ssd_conv_b[i][None])
            to_scan_layout = lambda s: s.transpose(0, 3, 1, 2).reshape(DEC_BATCH, SSD_STATE, SSD_DI)
            dtb, alog = ssd_dt_bias[i].reshape(1, -1), ssd_a_log[i].reshape(1, -1)
            d_exp, norm_g = jnp.repeat(ssd_d[i], SSD_HEADDIM)[None], ssd_norm_g[i][None]
            scan_p = functools.partial(_ssd_scan, xbc[0], dt[0], dtb, alog, None, nseq=BATCH, emit_state=True)
            y_f, s_f = scan_p(None, reverse=False)
            y_p, s_b = scan_p((y_f, z[0], d_exp, norm_g), reverse=True)
            scan_s = functools.partial(_ssd_scan, xbc[1], dt[1], dtb, alog, nseq=DEC_BATCH, emit_state=False)
            y_f = scan_s(to_scan_layout(state_ssm_fwd[:, i]), None, reverse=False)
            y_s = scan_s(to_scan_layout(state_ssm_bwd[:, i]), (y_f, z[1], d_exp, norm_g), reverse=True)
            x = _proj_res(x, mods, [(y_p, y_s)], [ssd_w_out[i].astype(BF16)], gate_row=2)
            outs.setdefault("sf", []).append(s_f.reshape(BATCH, SSD_HEADS, SSD_HEADDIM, SSD_STATE))
            outs.setdefault("sb", []).append(s_b.reshape(BATCH, SSD_HEADS, SSD_HEADDIM, SSD_STATE))
        x = _conv_ffn(x, mods, norm_ffn_g[l][None], ffn_up, ffn_conv_w[l], ffn_conv_b[l][None],
                      ffn_down, final_norm_g[None], layer=l, last_layer=(l == DEPTH - 1))
    y_prompt = x[0].reshape(BATCH, SEQ, D_MODEL)
    y_sample = x[1].reshape(DEC_BATCH, DEC_SEQ, D_MODEL)
    return (y_prompt, y_sample, jnp.stack(outs["k"], axis=1), jnp.stack(outs["v"], axis=1),
            jnp.stack(outs["sf"], axis=1), jnp.stack(outs["sb"], axis=1))
```
